```python
import math
import jax, jax.numpy as jnp
from jax import lax
import numpy as np

D_MODEL = 1024
BATCH = 16
SEQ = 2048
DEPTH = 4

CHUNK = 64
Q_BLOCK = 128
N_MIXERS = 2
FFN_PERIOD = 2
RMS_EPS = 1e-6

MLA_HEADS = 8
MLA_NOPE = 128
MLA_ROPE = 64
MLA_V = 128
MLA_QK = MLA_NOPE + MLA_ROPE
Q_LORA = 256
KV_LORA = 128
ROPE_THETA = 10000.0

CA_HEADS = 16
CA_HEAD_DIM = 64
LEFT_CHUNKS = 8
BAND_CHUNKS = LEFT_CHUNKS + 1
MAX_REL = 256
REL_BUCKETS = MAX_REL + CHUNK

DENSE_FF = 2816
N_EXPERTS = 8
TOP_K = 2
EXPERT_FF = 3584

kernel_name = "hybrid_mla_chunkband_moe_trunk"


def rmsnorm(x, g):
    xf = x.astype(jnp.float32)
    y = xf * lax.rsqrt(jnp.mean(xf * xf, axis=-1, keepdims=True) + RMS_EPS)
    return (y * g.astype(jnp.float32)).astype(x.dtype)


def rope_angles(positions):
    inv = 1.0 / (ROPE_THETA ** (jnp.arange(0, MLA_ROPE, 2, dtype=jnp.float32) / MLA_ROPE))
    ang = positions.astype(jnp.float32)[..., None] * inv
    return jnp.cos(ang), jnp.sin(ang)


def apply_rope(x, cos, sin):
    x1, x2 = jnp.split(x.astype(jnp.float32), 2, axis=-1)
    c = cos[:, :, None, :]
    s = sin[:, :, None, :]
    return jnp.concatenate([x1 * c - x2 * s, x1 * s + x2 * c], axis=-1).astype(x.dtype)


def chunk_causal_attention(q, k, v):
    B, S, H, Dq = q.shape
    Dv = v.shape[-1]
    nb = S // Q_BLOCK
    scale = Dq ** -0.5
    qb = q.reshape(B, nb, Q_BLOCK, H, Dq).transpose(1, 0, 2, 3, 4)
    key_chunk = jnp.arange(S) // CHUNK

    def one_block(args):
        qi, bi = args
        s = jnp.einsum('bqhd,bkhd->bhqk', qi, k, preferred_element_type=jnp.float32) * scale
        q_chunk = (bi * Q_BLOCK + jnp.arange(Q_BLOCK)) // CHUNK
        mask = key_chunk[None, :] <= q_chunk[:, None]
        s = jnp.where(mask[None, None], s, -jnp.inf)
        p = jax.nn.softmax(s, axis=-1)
        return jnp.einsum('bhqk,bkhd->bqhd', p.astype(v.dtype), v)

    o = lax.map(one_block, (qb, jnp.arange(nb)))
    return o.transpose(1, 0, 2, 3, 4).reshape(B, S, H, Dv)


def mla_mixer(h, cos, sin, w_in, q_a_norm, kv_a_norm, w_uq, w_ukv, q_norm, k_norm, w_o):
    B, S, _ = h.shape
    a = h @ w_in
    q_lat, kv_lat, k_rope = jnp.split(a, [Q_LORA, Q_LORA + KV_LORA], axis=-1)
    q = (rmsnorm(q_lat, q_a_norm) @ w_uq).reshape(B, S, MLA_HEADS, MLA_QK)
    kv = (rmsnorm(kv_lat, kv_a_norm) @ w_ukv).reshape(B, S, MLA_HEADS, MLA_NOPE + MLA_V)
    k_nope, v = jnp.split(kv, [MLA_NOPE], axis=-1)
    k_rope = jnp.broadcast_to(k_rope[:, :, None, :], (B, S, MLA_HEADS, MLA_ROPE))
    k = jnp.concatenate([k_nope, k_rope], axis=-1)
    q = rmsnorm(q, q_norm)
    k = rmsnorm(k, k_norm)
    q = jnp.concatenate([q[..., :MLA_NOPE], apply_rope(q[..., MLA_NOPE:], cos, sin)], axis=-1)
    k = jnp.concatenate([k[..., :MLA_NOPE], apply_rope(k[..., MLA_NOPE:], cos, sin)], axis=-1)
    o = chunk_causal_attention(q, k, v)
    return o.reshape(B, S, MLA_HEADS * MLA_V) @ w_o


def chunked_band_mixer(h, w_qkv, q_norm, k_norm, rel_bias, w_o):
    B, S, _ = h.shape
    NC = S // CHUNK
    BAND = BAND_CHUNKS * CHUNK
    qkv = (h @ w_qkv).reshape(B, S, 3, CA_HEADS, CA_HEAD_DIM)
    q = rmsnorm(qkv[:, :, 0], q_norm)
    k = rmsnorm(qkv[:, :, 1], k_norm)
    v = qkv[:, :, 2]
    qc = q.reshape(B, NC, CHUNK, CA_HEADS, CA_HEAD_DIM)
    pad = ((0, 0), (LEFT_CHUNKS, 0), (0, 0), (0, 0), (0, 0))
    kc = jnp.pad(k.reshape(B, NC, CHUNK, CA_HEADS, CA_HEAD_DIM), pad)
    vc = jnp.pad(v.reshape(B, NC, CHUNK, CA_HEADS, CA_HEAD_DIM), pad)
    band_idx = jnp.arange(NC)[:, None] + jnp.arange(BAND_CHUNKS)[None, :]
    kb = kc[:, band_idx].reshape(B, NC, BAND, CA_HEADS, CA_HEAD_DIM)
    vb = vc[:, band_idx].reshape(B, NC, BAND, CA_HEADS, CA_HEAD_DIM)
    s = jnp.einsum('bcqhd,bckhd->bchqk', qc, kb, preferred_element_type=jnp.float32) * (CA_HEAD_DIM ** -0.5)
    q_off = LEFT_CHUNKS * CHUNK + jnp.arange(CHUNK)
    k_off = jnp.arange(BAND)
    rel = jnp.clip(q_off[:, None] - k_off[None, :], -(CHUNK - 1), MAX_REL) + (CHUNK - 1)
    bias = rel_bias[:, rel].astype(jnp.float32)
    key_chunk = jnp.arange(NC)[:, None] - LEFT_CHUNKS + (k_off // CHUNK)[None, :]
    valid = key_chunk >= 0
    s = jnp.where(valid[None, :, None, None, :], s + bias[None, None], -jnp.inf)
    p = jax.nn.softmax(s, axis=-1)
    o = jnp.einsum('bchqk,bckhd->bcqhd', p.astype(vb.dtype), vb)
    return o.reshape(B, S, CA_HEADS * CA_HEAD_DIM) @ w_o


def swiglu(h, w_gate, w_up, w_down):
    return (jax.nn.silu(h @ w_gate) * (h @ w_up)) @ w_down


def moe_ffn(h, router, w_gate, w_up, w_down):
    B, S, D = h.shape
    t = h.reshape(B * S, D)
    logits = jnp.einsum('td,de->te', t, router, preferred_element_type=jnp.float32)
    top_val, top_idx = lax.top_k(logits, TOP_K)
    top_w = jax.nn.softmax(top_val, axis=-1)
    gates = jnp.sum(jax.nn.one_hot(top_idx, N_EXPERTS, dtype=jnp.float32) * top_w[..., None], axis=1)
    gates = gates.astype(t.dtype)
    out = jnp.zeros_like(t)
    for e in range(N_EXPERTS):
        out = out + gates[:, e:e + 1] * swiglu(t, w_gate[e], w_up[e], w_down[e])
    return out.reshape(B, S, D)


def setup_inputs(seed: int = 0) -> dict:
    key = jax.random.key(seed)
    keys = iter(jax.random.split(key, 128))

    def w(shape, fan_in):
        return jax.random.normal(next(keys), shape, jnp.float32) * fan_in ** -0.5

    def g(n):
        return 1.0 + 0.02 * jax.random.normal(next(keys), (n,), jnp.float32)

    inp = {}
    inp["x"] = jax.random.normal(next(keys), (BATCH, SEQ, D_MODEL), jnp.float32)
    offs = jax.random.randint(next(keys), (BATCH, 1), 0, 64, dtype=jnp.int32) * CHUNK
    inp["positions"] = (offs + jnp.arange(SEQ, dtype=jnp.int32)[None, :]).astype(jnp.int32)
    for i in range(DEPTH):
        p = "l%d_" % i
        inp[p + "attn_norm"] = g(D_MODEL)
        if i % N_MIXERS == 0:
            inp[p + "mla_w_in"] = w((D_MODEL, Q_LORA + KV_LORA + MLA_ROPE), D_MODEL)
            inp[p + "mla_q_a_norm"] = g(Q_LORA)
            inp[p + "mla_kv_a_norm"] = g(KV_LORA)
            inp[p + "mla_w_uq"] = w((Q_LORA, MLA_HEADS * MLA_QK), Q_LORA)
            inp[p + "mla_w_ukv"] = w((KV_LORA, MLA_HEADS * (MLA_NOPE + MLA_V)), KV_LORA)
            inp[p + "mla_q_norm"] = g(MLA_QK)
            inp[p + "mla_k_norm"] = g(MLA_QK)
            inp[p + "mla_w_o"] = w((MLA_HEADS * MLA_V, D_MODEL), MLA_HEADS * MLA_V)
        else:
            inp[p + "ca_w_qkv"] = w((D_MODEL, 3 * CA_HEADS * CA_HEAD_DIM), D_MODEL)
            inp[p + "ca_q_norm"] = g(CA_HEAD_DIM)
            inp[p + "ca_k_norm"] = g(CA_HEAD_DIM)
            inp[p + "ca_rel_bias"] = 0.1 * jax.random.normal(next(keys), (CA_HEADS, REL_BUCKETS), jnp.float32)
            inp[p + "ca_w_o"] = w((CA_HEADS * CA_HEAD_DIM, D_MODEL), CA_HEADS * CA_HEAD_DIM)
        inp[p + "ffn_norm"] = g(D_MODEL)
        if i % FFN_PERIOD == 0:
            inp[p + "ffn_w_gate"] = w((D_MODEL, DENSE_FF), D_MODEL)
            inp[p + "ffn_w_up"] = w((D_MODEL, DENSE_FF), D_MODEL)
            inp[p + "ffn_w_down"] = w((DENSE_FF, D_MODEL), DENSE_FF)
        else:
            inp[p + "moe_router"] = w((D_MODEL, N_EXPERTS), D_MODEL)
            inp[p + "moe_w_gate"] = w((N_EXPERTS, D_MODEL, EXPERT_FF), D_MODEL)
            inp[p + "moe_w_up"] = w((N_EXPERTS, D_MODEL, EXPERT_FF), D_MODEL)
            inp[p + "moe_w_down"] = w((N_EXPERTS, EXPERT_FF, D_MODEL), EXPERT_FF)
    return inp


def reference(x, positions,
              l0_attn_norm, l0_mla_w_in, l0_mla_q_a_norm, l0_mla_kv_a_norm, l0_mla_w_uq, l0_mla_w_ukv,
              l0_mla_q_norm, l0_mla_k_norm, l0_mla_w_o, l0_ffn_norm, l0_ffn_w_gate, l0_ffn_w_up, l0_ffn_w_down,
              l1_attn_norm, l1_ca_w_qkv, l1_ca_q_norm, l1_ca_k_norm, l1_ca_rel_bias, l1_ca_w_o,
              l1_ffn_norm, l1_moe_router, l1_moe_w_gate, l1_moe_w_up, l1_moe_w_down,
              l2_attn_norm, l2_mla_w_in, l2_mla_q_a_norm, l2_mla_kv_a_norm, l2_mla_w_uq, l2_mla_w_ukv,
              l2_mla_q_norm, l2_mla_k_norm, l2_mla_w_o, l2_ffn_norm, l2_ffn_w_gate, l2_ffn_w_up, l2_ffn_w_down,
              l3_attn_norm, l3_ca_w_qkv, l3_ca_q_norm, l3_ca_k_norm, l3_ca_rel_bias, l3_ca_w_o,
              l3_ffn_norm, l3_moe_router, l3_moe_w_gate, l3_moe_w_up, l3_moe_w_down):
    attn_norms = (l0_attn_norm, l1_attn_norm, l2_attn_norm, l3_attn_norm)
    mixers = (
        (l0_mla_w_in, l0_mla_q_a_norm, l0_mla_kv_a_norm, l0_mla_w_uq, l0_mla_w_ukv, l0_mla_q_norm, l0_mla_k_norm, l0_mla_w_o),
        (l1_ca_w_qkv, l1_ca_q_norm, l1_ca_k_norm, l1_ca_rel_bias, l1_ca_w_o),
        (l2_mla_w_in, l2_mla_q_a_norm, l2_mla_kv_a_norm, l2_mla_w_uq, l2_mla_w_ukv, l2_mla_q_norm, l2_mla_k_norm, l2_mla_w_o),
        (l3_ca_w_qkv, l3_ca_q_norm, l3_ca_k_norm, l3_ca_rel_bias, l3_ca_w_o),
    )
    ffn_norms = (l0_ffn_norm, l1_ffn_norm, l2_ffn_norm, l3_ffn_norm)
    ffns = (
        (l0_ffn_w_gate, l0_ffn_w_up, l0_ffn_w_down),
        (l1_moe_router, l1_moe_w_gate, l1_moe_w_up, l1_moe_w_down),
        (l2_ffn_w_gate, l2_ffn_w_up, l2_ffn_w_down),
        (l3_moe_router, l3_moe_w_gate, l3_moe_w_up, l3_moe_w_down),
    )
    cos, sin = rope_angles(positions)
    for i in range(DEPTH):
        h = rmsnorm(x, attn_norms[i])
        if i % N_MIXERS == 0:
            x = x + mla_mixer(h, cos, sin, *mixers[i])
        else:
            x = x + chunked_band_mixer(h, *mixers[i])
        h = rmsnorm(x, ffn_norms[i])
        if i % FFN_PERIOD == 0:
            x = x + swiglu(h, *ffns[i])
        else:
            x = x + moe_ffn(h, *ffns[i])
    return x
```

```python
import functools

import numpy as np
import jax
import jax.numpy as jnp
from jax import lax
from jax.experimental import pallas as pl
from jax.experimental.pallas import tpu as pltpu

F32 = jnp.float32
BF16 = jnp.bfloat16

D_MODEL = 1024
CHUNK = 64
RMS_EPS = 1e-6

MLA_HEADS = 8
MLA_NOPE = 128
MLA_ROPE = 64
MLA_V = 128
MLA_QK = MLA_NOPE + MLA_ROPE
Q_LORA = 256
KV_LORA = 128
ROPE_THETA = 10000.0

CA_HEADS = 16
CA_HEAD_DIM = 64
LEFT_CHUNKS = 8
MAX_REL = 256

N_EXPERTS = 8
TOP_K = 2

LANES = 128
VMEM_LIMIT = 56 * 1024 * 1024

ROW_TILE = 512
ATTN_Q_TILE = 256
BAND_KEYS = 3 * ATTN_Q_TILE
DENSE_FF_TILE = 1408
MOE_FF_TILE = 896
NEG_INF = float("-inf")


def _dot(a, b):
    return jnp.dot(a, b, preferred_element_type=F32)


def _dot_nt(a, b):
    return lax.dot_general(a, b, (((1,), (1,)), ((), ())), preferred_element_type=F32)


def _rms(x, g):
    return x * lax.rsqrt(jnp.mean(x * x, axis=-1, keepdims=True) + RMS_EPS) * g


def _silu(g):
    return g / (1.0 + jnp.exp(-g))


def _params(*sem):
    return pltpu.CompilerParams(dimension_semantics=sem, vmem_limit_bytes=VMEM_LIMIT)


def _mla_prep_kernel(x_ref, pos_ref, gn_ref, win_ref, qan_ref, kvan_ref, wuq_ref, wukv_ref,
                     invf_ref, qtab_ref, ktab_ref, q_out, k_out, v_out):
    h = _rms(x_ref[...], gn_ref[...]).astype(BF16)
    a = _dot(h, win_ref[...])
    qn = _rms(a[:, :Q_LORA], qan_ref[...]).astype(BF16)
    kvn = _rms(a[:, Q_LORA:Q_LORA + KV_LORA], kvan_ref[...]).astype(BF16)
    kr2 = a[:, Q_LORA + KV_LORA:]
    q_all = _dot(qn, wuq_ref[...])
    kv_all = _dot(kvn, wukv_ref[...])

    ang = pos_ref[...].astype(F32) * invf_ref[...]
    cos = jnp.cos(ang)
    sin = jnp.sin(ang)
    lane = lax.broadcasted_iota(jnp.int32, (1, LANES), 1)
    rope_lane = lane < MLA_ROPE

    q_gn, q_ga, q_gb = qtab_ref[0:1, :], qtab_ref[1:2, :], qtab_ref[2:3, :]
    k_gn, k_ga, k_gb = ktab_ref[0:1, :], ktab_ref[1:2, :], ktab_ref[2:3, :]
    q_cos, q_sin = cos * q_ga, sin * q_gb
    k_rot = kr2 * (cos * k_ga) + pltpu.roll(kr2, MLA_ROPE, 1) * (sin * k_gb)
    kr_ss = jnp.sum(jnp.where(rope_lane, kr2 * kr2, 0.0), axis=-1, keepdims=True)
    scale = MLA_QK ** -0.5

    for hd in range(MLA_HEADS):
        lo = hd * 2 * LANES
        qa = q_all[:, lo:lo + LANES]
        qb = q_all[:, lo + LANES:lo + 2 * LANES]
        ss = jnp.sum(qa * qa + jnp.where(rope_lane, qb * qb, 0.0), axis=-1, keepdims=True)
        r = lax.rsqrt(ss * (1.0 / MLA_QK) + RMS_EPS) * scale
        q_rot = (qb * q_cos + pltpu.roll(qb, MLA_ROPE, 1) * q_sin) * r
        q_out[hd, :, :MLA_NOPE] = (qa * q_gn * r).astype(BF16)
        q_out[hd, :, MLA_NOPE:] = q_rot[:, :MLA_ROPE].astype(BF16)

        kn = kv_all[:, lo:lo + LANES]
        ssk = jnp.sum(kn * kn, axis=-1, keepdims=True) + kr_ss
        rk = lax.rsqrt(ssk * (1.0 / MLA_QK) + RMS_EPS)
        k_out[hd, :, :MLA_NOPE] = (kn * k_gn * rk).astype(BF16)
        k_out[hd, :, MLA_NOPE:] = (k_rot * rk)[:, :MLA_ROPE].astype(BF16)
        v_out[hd] = kv_all[:, lo + LANES:lo + 2 * LANES].astype(BF16)


def _rope_partner():
    d = np.arange(MLA_ROPE)
    half = MLA_ROPE // 2
    partner = np.where(d < half, d + half, d - half)
    sign = np.where(d < half, -1.0, 1.0).astype(np.float32)
    return partner, sign


def _mla_tables(g):
    partner, sign = _rope_partner()
    zeros = jnp.zeros((LANES - MLA_ROPE,), F32)
    g_rope = g[MLA_NOPE:]
    rows = [g[:MLA_NOPE],
            jnp.concatenate([g_rope, zeros]),
            jnp.concatenate([g_rope[partner] * sign, zeros])]
    rows += [jnp.zeros((LANES,), F32)] * 5
    return jnp.stack(rows)


def _mla_prep(x2, pos2, attn_norm, w_in, q_a_norm, kv_a_norm, w_uq, w_ukv, q_norm, k_norm):
    T = x2.shape[0]
    tm = ROW_TILE
    partner, _ = _rope_partner()
    rope0 = Q_LORA + KV_LORA
    w_in_ext = jnp.concatenate([w_in, w_in[:, rope0 + partner]], axis=1).astype(BF16)
    wq = w_uq.reshape(Q_LORA, MLA_HEADS, MLA_QK)
    w_uq_ext = jnp.concatenate([wq, wq[:, :, MLA_NOPE + partner]], axis=2)
    w_uq_ext = w_uq_ext.reshape(Q_LORA, MLA_HEADS * 2 * LANES).astype(BF16)
    w_ukv_b = w_ukv.astype(BF16)
    inv = 1.0 / (ROPE_THETA ** (jnp.arange(0, MLA_ROPE, 2, dtype=F32) / MLA_ROPE))
    invf = jnp.concatenate([inv, inv, jnp.zeros((LANES - MLA_ROPE,), F32)])[None, :]

    full = lambda shape: pl.BlockSpec(shape, lambda i: (0,) * len(shape))
    out_qk = jax.ShapeDtypeStruct((MLA_HEADS, T, MLA_QK), BF16)
    out_v = jax.ShapeDtypeStruct((MLA_HEADS, T, MLA_V), BF16)
    return pl.pallas_call(
        _mla_prep_kernel,
        grid=(T // tm,),
        in_specs=[
            pl.BlockSpec((tm, D_MODEL), lambda i: (i, 0)),
            pl.BlockSpec((tm, 1), lambda i: (i, 0)),
            full((1, D_MODEL)),
            full((D_MODEL, 4 * LANES)),
            full((1, Q_LORA)),
            full((1, KV_LORA)),
            full((Q_LORA, MLA_HEADS * 2 * LANES)),
            full((KV_LORA, MLA_HEADS * 2 * LANES)),
            full((1, LANES)),
            full((8, LANES)),
            full((8, LANES)),
        ],
        out_specs=[
            pl.BlockSpec((MLA_HEADS, tm, MLA_QK), lambda i: (0, i, 0)),
            pl.BlockSpec((MLA_HEADS, tm, MLA_QK), lambda i: (0, i, 0)),
            pl.BlockSpec((MLA_HEADS, tm, MLA_V), lambda i: (0, i, 0)),
        ],
        out_shape=[out_qk, out_qk, out_v],
        compiler_params=_params("parallel"),
        name="mla_prep",
    )(x2, pos2, attn_norm[None, :], w_in_ext, q_a_norm[None, :], kv_a_norm[None, :],
      w_uq_ext, w_ukv_b, invf, _mla_tables(q_norm), _mla_tables(k_norm))


def _mla_attn_kernel(q_ref, k_ref, v_ref, o_ref, *, seq, tq):
    row_c = lax.broadcasted_iota(jnp.int32, (tq, tq), 0) // CHUNK
    col_c = lax.broadcasted_iota(jnp.int32, (tq, tq), 1) // CHUNK
    diag_mask = col_c <= row_c
    for qi in range(seq // tq):
        lo, hi = qi * tq, (qi + 1) * tq
        q = q_ref[0, lo:hi, :]
        s_d = jnp.where(diag_mask, _dot_nt(q, k_ref[0, lo:hi, :]), NEG_INF)
        m = jnp.max(s_d, axis=-1, keepdims=True)
        if qi > 0:
            s_o = _dot_nt(q, k_ref[0, :lo, :])
            m = jnp.maximum(m, jnp.max(s_o, axis=-1, keepdims=True))
        p_d = jnp.exp(s_d - m)
        l = jnp.sum(p_d, axis=-1, keepdims=True)
        o = _dot(p_d.astype(BF16), v_ref[0, lo:hi, :])
        if qi > 0:
            p_o = jnp.exp(s_o - m)
            l = l + jnp.sum(p_o, axis=-1, keepdims=True)
            o = o + _dot(p_o.astype(BF16), v_ref[0, :lo, :])
        o_ref[lo:hi, :] = (o / l).astype(BF16)


def _mla_attn(q, k, v, batch, seq):
    T = batch * seq
    kern = functools.partial(_mla_attn_kernel, seq=seq, tq=ATTN_Q_TILE)
    return pl.pallas_call(
        kern,
        grid=(MLA_HEADS, batch),
        in_specs=[
            pl.BlockSpec((1, seq, MLA_QK), lambda h, b: (h, b, 0)),
            pl.BlockSpec((1, seq, MLA_QK), lambda h, b: (h, b, 0)),
            pl.BlockSpec((1, seq, MLA_V), lambda h, b: (h, b, 0)),
        ],
        out_specs=pl.BlockSpec((seq, MLA_V), lambda h, b: (b, h)),
        out_shape=jax.ShapeDtypeStruct((T, MLA_HEADS * MLA_V), BF16),
        compiler_params=_params("parallel", "parallel"),
        name="mla_attn",
    )(q, k, v)


def _proj_dense_ffn_kernel(x_ref, o_ref, wo_ref, gn_ref, wg_ref, wu_ref, wd_ref, out_ref, hs_ref):
    @pl.when(pl.program_id(1) == 0)
    def _():
        x1 = x_ref[...] + _dot(o_ref[...], wo_ref[...])
        out_ref[...] = x1
        hs_ref[...] = _rms(x1, gn_ref[...]).astype(BF16)

    h = hs_ref[...]
    a = (_silu(_dot(h, wg_ref[...])) * _dot(h, wu_ref[...])).astype(BF16)
    out_ref[...] += _dot(a, wd_ref[...])


def _proj_dense_ffn(x2, o2, w_o, ffn_norm, w_gate, w_up, w_down):
    T = x2.shape[0]
    ff = w_gate.shape[1]
    tm, tf = ROW_TILE, DENSE_FF_TILE
    return pl.pallas_call(
        _proj_dense_ffn_kernel,
        grid=(T // tm, ff // tf),
        in_specs=[
            pl.BlockSpec((tm, D_MODEL), lambda i, f: (i, 0)),
            pl.BlockSpec((tm, D_MODEL), lambda i, f: (i, 0)),
            pl.BlockSpec((D_MODEL, D_MODEL), lambda i, f: (0, 0)),
            pl.BlockSpec((1, D_MODEL), lambda i, f: (0, 0)),
            pl.BlockSpec((D_MODEL, tf), lambda i, f: (0, f)),
            pl.BlockSpec((D_MODEL, tf), lambda i, f: (0, f)),
            pl.BlockSpec((tf, D_MODEL), lambda i, f: (f, 0)),
        ],
        out_specs=pl.BlockSpec((tm, D_MODEL), lambda i, f: (i, 0)),
        out_shape=jax.ShapeDtypeStruct((T, D_MODEL), F32),
        scratch_shapes=[pltpu.VMEM((tm, D_MODEL), BF16)],
        compiler_params=_params("parallel", "arbitrary"),
        name="proj_dense_ffn",
    )(x2, o2, w_o.astype(BF16), ffn_norm[None, :], w_gate.astype(BF16), w_up.astype(BF16),
      w_down.astype(BF16))


def _ca_prep_kernel(x_ref, gn_ref, w_ref, gq_ref, gk_ref, q_out, k_out, v_out):
    h = _rms(x_ref[...], gn_ref[...]).astype(BF16)
    qkv = _dot(h, w_ref[...])
    q_out[...] = (qkv[:, :D_MODEL] * gq_ref[...]).astype(BF16)
    k_out[...] = (qkv[:, D_MODEL:2 * D_MODEL] * gk_ref[...]).astype(BF16)
    v_out[...] = qkv[:, 2 * D_MODEL:].astype(BF16)


def _ca_prep(x2, attn_norm, w_qkv, q_norm, k_norm):
    T = x2.shape[0]
    tm = ROW_TILE
    full = lambda shape: pl.BlockSpec(shape, lambda i: (0,) * len(shape))
    out = jax.ShapeDtypeStruct((T, D_MODEL), BF16)
    row = pl.BlockSpec((tm, D_MODEL), lambda i: (i, 0))
    return pl.pallas_call(
        _ca_prep_kernel,
        grid=(T // tm,),
        in_specs=[row, full((1, D_MODEL)), full((D_MODEL, 3 * D_MODEL)), full((1, D_MODEL)),
                  full((1, D_MODEL))],
        out_specs=[row, row, row],
        out_shape=[out, out, out],
        compiler_params=_params("parallel"),
        name="ca_prep",
    )(x2, attn_norm[None, :], w_qkv.astype(BF16), jnp.tile(q_norm, CA_HEADS)[None, :],
      jnp.tile(k_norm, CA_HEADS)[None, :])


def _band_attn_kernel(q_ref, k_ref, v_ref, bias_ref, o_ref, *, tq):
    qi = pl.program_id(2)
    variant = jnp.minimum(qi, 2)
    kstart = pl.multiple_of(jnp.maximum(qi - 2, 0) * tq, tq)
    q = q_ref[...]
    k = k_ref[pl.ds(kstart, BAND_KEYS), :]
    v = v_ref[pl.ds(kstart, BAND_KEYS), :]

    lane = lax.broadcasted_iota(jnp.int32, (1, LANES), 1)
    head_lane = [lane < CA_HEAD_DIM, lane >= CA_HEAD_DIM]
    q32 = q.astype(F32)
    k32 = k.astype(F32)
    ksq = (k32 * k32).astype(BF16)
    sub = lax.broadcasted_iota(jnp.int32, (16, LANES), 0)
    lane16 = lax.broadcasted_iota(jnp.int32, (16, LANES), 1)
    head_rows = jnp.where((lane16 // CA_HEAD_DIM) == sub, 1.0, 0.0).astype(BF16)
    k_ss = _dot_nt(head_rows, ksq)
    k_r = lax.rsqrt(k_ss * (1.0 / CA_HEAD_DIM) + RMS_EPS)
    scale = CA_HEAD_DIM ** -0.5

    outs = []
    for hh in range(2):
        q_ss = jnp.sum(jnp.where(head_lane[hh], q32 * q32, 0.0), axis=-1, keepdims=True)
        q_r = lax.rsqrt(q_ss * (1.0 / CA_HEAD_DIM) + RMS_EPS) * scale
        qh = jnp.where(head_lane[hh], q, jnp.zeros_like(q))
        s = _dot_nt(qh, k) * q_r * k_r[hh:hh + 1, :] + bias_ref[hh, variant]
        m = jnp.max(s, axis=-1, keepdims=True)
        p = jnp.exp(s - m)
        l = jnp.sum(p, axis=-1, keepdims=True)
        outs.append(_dot(p.astype(BF16), v) / l)
    o_ref[...] = jnp.where(head_lane[0], outs[0], outs[1]).astype(BF16)


def _band_bias_tables(rel_bias):
    tq = ATTN_Q_TILE
    i = np.arange(tq)[:, None]
    j = np.arange(BAND_KEYS)[None, :]
    idx, valid = [], []
    for var in range(3):
        q_pos = var * tq + i
        rel = q_pos - j
        chunk_gap = q_pos // CHUNK - j // CHUNK
        valid.append((chunk_gap >= 0) & (chunk_gap <= LEFT_CHUNKS))
        idx.append(np.clip(rel, -(CHUNK - 1), MAX_REL) + (CHUNK - 1))
    idx = np.stack(idx)
    valid = np.stack(valid)
    tab = rel_bias.astype(F32)[:, idx]
    return jnp.where(valid[None], tab, NEG_INF)


def _band_attn(q, k, v, rel_bias, batch, seq):
    T = batch * seq
    tq = ATTN_Q_TILE
    nq = seq // tq
    bias = _band_bias_tables(rel_bias)
    kern = functools.partial(_band_attn_kernel, tq=tq)
    return pl.pallas_call(
        kern,
        grid=(CA_HEADS // 2, batch, nq),
        in_specs=[
            pl.BlockSpec((tq, LANES), lambda hp, b, qi: (b * nq + qi, hp)),
            pl.BlockSpec((seq, LANES), lambda hp, b, qi: (b, hp)),
            pl.BlockSpec((seq, LANES), lambda hp, b, qi: (b, hp)),
            pl.BlockSpec((2, 3, tq, BAND_KEYS), lambda hp, b, qi: (hp, 0, 0, 0)),
        ],
        out_specs=pl.BlockSpec((tq, LANES), lambda hp, b, qi: (b * nq + qi, hp)),
        out_shape=jax.ShapeDtypeStruct((T, D_MODEL), BF16),
        compiler_params=_params("parallel", "parallel", "arbitrary"),
        name="band_attn",
    )(q, k, v, bias)


def _proj_router_kernel(x_ref, o_ref, wo_ref, gn_ref, rhi_ref, rlo_ref, x1_out, h_out, idx_out, w_out):
    x1 = x_ref[...] + _dot(o_ref[...], wo_ref[...])
    x1_out[...] = x1
    h = _rms(x1, gn_ref[...])
    h_out[...] = h
    h_hi = h.astype(BF16)
    h_lo = (h - h_hi.astype(F32)).astype(BF16)
    logits = _dot(h_hi, rhi_ref[...]) + (_dot(h_lo, rhi_ref[...]) + _dot(h_hi, rlo_ref[...]))

    lane = lax.broadcasted_iota(jnp.int32, logits.shape, 1)
    lane_f = lane.astype(F32)
    lg = jnp.where(lane < N_EXPERTS, logits, NEG_INF)
    m1 = jnp.max(lg, axis=-1, keepdims=True)
    i1 = jnp.min(jnp.where(lg == m1, lane_f, float(LANES)), axis=-1, keepdims=True)
    lg2 = jnp.where(lane_f == i1, NEG_INF, lg)
    m2 = jnp.max(lg2, axis=-1, keepdims=True)
    i2 = jnp.min(jnp.where(lg2 == m2, lane_f, float(LANES)), axis=-1, keepdims=True)
    e = jnp.exp(m2 - m1)
    w1 = 1.0 / (1.0 + e)
    w2 = e / (1.0 + e)
    idx_out[...] = jnp.where(lane == 0, i1, jnp.where(lane == 1, i2, 0.0)).astype(jnp.int32)
    w_out[...] = jnp.where(lane == 0, w1, jnp.where(lane == 1, w2, 0.0))


def _proj_router(x2, o2, w_o, ffn_norm, router):
    T = x2.shape[0]
    tm = ROW_TILE
    r_pad = jnp.zeros((D_MODEL, LANES), F32).at[:, :N_EXPERTS].set(router)
    r_hi = r_pad.astype(BF16)
    r_lo = (r_pad - r_hi.astype(F32)).astype(BF16)
    full = lambda shape: pl.BlockSpec(shape, lambda i: (0,) * len(shape))
    row = pl.BlockSpec((tm, D_MODEL), lambda i: (i, 0))
    small = pl.BlockSpec((tm, LANES), lambda i: (i, 0))
    return pl.pallas_call(
        _proj_router_kernel,
        grid=(T // tm,),
        in_specs=[row, row, full((D_MODEL, D_MODEL)), full((1, D_MODEL)), full((D_MODEL, LANES)),
                  full((D_MODEL, LANES))],
        out_specs=[row, row, small, small],
        out_shape=[jax.ShapeDtypeStruct((T, D_MODEL), F32), jax.ShapeDtypeStruct((T, D_MODEL), F32),
                   jax.ShapeDtypeStruct((T, LANES), jnp.int32), jax.ShapeDtypeStruct((T, LANES), F32)],
        compiler_params=_params("parallel"),
        name="proj_router",
    )(x2, o2, w_o.astype(BF16), ffn_norm[None, :], r_hi, r_lo)


def _moe_plan(top_idx, top_w, tm):
    T = top_idx.shape[0]
    n_assign = T * TOP_K
    n_tiles = n_assign // tm + N_EXPERTS
    e_flat = top_idx.reshape(n_assign)
    onehot = (e_flat[:, None] == jnp.arange(N_EXPERTS, dtype=jnp.int32)[None, :]).astype(jnp.int32)
    csum = jnp.cumsum(onehot, axis=0)
    counts = csum[-1]
    rank = jnp.sum(onehot * csum, axis=1) - 1
    tiles_per = (counts + tm - 1) // tm
    tile_end = jnp.cumsum(tiles_per)
    tile_start = tile_end - tiles_per
    dest = tile_start[e_flat] * tm + rank
    sorted_start = jnp.cumsum(counts) - counts
    order = jnp.argsort(e_flat * n_assign + jnp.arange(n_assign, dtype=jnp.int32))

    tile_id = jnp.arange(n_tiles, dtype=jnp.int32)
    tile_valid = (tile_id < tile_end[-1]).astype(jnp.int32)
    tile_expert = jnp.minimum(jnp.sum((tile_id[:, None] >= tile_end[None, :]).astype(jnp.int32), axis=1),
                              N_EXPERTS - 1)
    row = jnp.arange(n_tiles * tm, dtype=jnp.int32)
    row_e = tile_expert[row // tm]
    row_j = row - tile_start[row_e] * tm
    row_valid = (row_j < counts[row_e]) & (tile_valid[row // tm] > 0)
    src = jnp.clip(sorted_start[row_e] + row_j, 0, n_assign - 1)
    assign = order[src]
    row_token = jnp.where(row_valid, assign // TOP_K, 0).astype(jnp.int32)
    row_gate = jnp.where(row_valid, top_w.reshape(n_assign)[assign], 0.0).astype(F32)
    return row_token, row_gate[:, None], tile_expert, tile_valid, dest.astype(jnp.int32)


def _gather_rows_kernel(tok_ref, valid_ref, h_hbm, out_ref, sem, *, tm):
    i = pl.program_id(0)

    @pl.when(valid_ref[i] > 0)
    def _():
        def issue(r, carry):
            t = tok_ref[i * tm + r]
            pltpu.make_async_copy(h_hbm.at[pl.ds(t, 1), :], out_ref.at[pl.ds(r, 1), :], sem).start()
            return carry
        lax.fori_loop(0, tm, issue, 0, unroll=8)
        pltpu.make_async_copy(h_hbm.at[pl.ds(0, tm), :], out_ref, sem).wait()

    @pl.when(valid_ref[i] == 0)
    def _():
        out_ref[...] = jnp.zeros_like(out_ref)


def _gather_rows(h2, row_token, tile_valid, tm):
    n_rows = row_token.shape[0]
    kern = functools.partial(_gather_rows_kernel, tm=tm)
    return pl.pallas_call(
        kern,
        grid_spec=pltpu.PrefetchScalarGridSpec(
            num_scalar_prefetch=2,
            grid=(n_rows // tm,),
            in_specs=[pl.BlockSpec(memory_space=pl.ANY)],
            out_specs=pl.BlockSpec((tm, D_MODEL), lambda i, tok, val: (i, 0)),
            scratch_shapes=[pltpu.SemaphoreType.DMA],
        ),
        out_shape=jax.ShapeDtypeStruct((n_rows, D_MODEL), F32),
        compiler_params=_params("arbitrary"),
        name="moe_gather",
    )(row_token, tile_valid, h2)


def _moe_ffn_kernel(te_ref, tv_ref, xs_ref, gate_ref, wg_ref, wu_ref, wd_ref, ys_ref, hs_ref, *, nf):
    i = pl.program_id(0)
    f = pl.program_id(1)

    @pl.when(f == 0)
    def _():
        ys_ref[...] = jnp.zeros_like(ys_ref)
        hs_ref[...] = xs_ref[...].astype(BF16)

    @pl.when(tv_ref[i] > 0)
    def _():
        h = hs_ref[...]
        a = (_silu(_dot(h, wg_ref[0])) * _dot(h, wu_ref[0])).astype(BF16)
        ys_ref[...] += _dot(a, wd_ref[0])

    @pl.when(f == nf - 1)
    def _():
        ys_ref[...] = ys_ref[...] * gate_ref[...]


def _moe_ffn(xs, row_gate, tile_expert, tile_valid, w_gate, w_up, w_down, tm):
    n_rows = xs.shape[0]
    ff = w_gate.shape[2]
    tf = MOE_FF_TILE
    nf = ff // tf
    ff_blk = lambda i, f, te, tv: f * tv[i] + (nf - 1) * (1 - tv[i])
    kern = functools.partial(_moe_ffn_kernel, nf=nf)
    return pl.pallas_call(
        kern,
        grid_spec=pltpu.PrefetchScalarGridSpec(
            num_scalar_prefetch=2,
            grid=(n_rows // tm, nf),
            in_specs=[
                pl.BlockSpec((tm, D_MODEL), lambda i, f, te, tv: (i, 0)),
                pl.BlockSpec((tm, 1), lambda i, f, te, tv: (i, 0)),
                pl.BlockSpec((1, D_MODEL, tf), lambda i, f, te, tv: (te[i], 0, ff_blk(i, f, te, tv))),
                pl.BlockSpec((1, D_MODEL, tf), lambda i, f, te, tv: (te[i], 0, ff_blk(i, f, te, tv))),
                pl.BlockSpec((1, tf, D_MODEL), lambda i, f, te, tv: (te[i], ff_blk(i, f, te, tv), 0)),
            ],
            out_specs=pl.BlockSpec((tm, D_MODEL), lambda i, f, te, tv: (i, 0)),
            scratch_shapes=[pltpu.VMEM((tm, D_MODEL), BF16)],
        ),
        out_shape=jax.ShapeDtypeStruct((n_rows, D_MODEL), F32),
        compiler_params=_params("arbitrary", "arbitrary"),
        name="moe_ffn",
    )(tile_expert, tile_valid, xs, row_gate, w_gate.astype(BF16), w_up.astype(BF16), w_down.astype(BF16))


def _combine_kernel(pos_ref, x1_ref, ys_hbm, out_ref, buf0, buf1, sem0, sem1, *, tm):
    i = pl.program_id(0)

    def issue(r, carry):
        a = (i * tm + r) * TOP_K
        pltpu.make_async_copy(ys_hbm.at[pl.ds(pos_ref[a], 1), :], buf0.at[pl.ds(r, 1), :], sem0).start()
        pltpu.make_async_copy(ys_hbm.at[pl.ds(pos_ref[a + 1], 1), :], buf1.at[pl.ds(r, 1), :], sem1).start()
        return carry
    lax.fori_loop(0, tm, issue, 0, unroll=8)
    pltpu.make_async_copy(ys_hbm.at[pl.ds(0, tm), :], buf0, sem0).wait()
    pltpu.make_async_copy(ys_hbm.at[pl.ds(0, tm), :], buf1, sem1).wait()
    out_ref[...] = x1_ref[...] + (buf0[...] + buf1[...])


def _combine(x1, ys, dest, tm):
    T = x1.shape[0]
    kern = functools.partial(_combine_kernel, tm=tm)
    return pl.pallas_call(
        kern,
        grid_spec=pltpu.PrefetchScalarGridSpec(
            num_scalar_prefetch=1,
            grid=(T // tm,),
            in_specs=[pl.BlockSpec((tm, D_MODEL), lambda i, pos: (i, 0)),
                      pl.BlockSpec(memory_space=pl.ANY)],
            out_specs=pl.BlockSpec((tm, D_MODEL), lambda i, pos: (i, 0)),
            scratch_shapes=[pltpu.VMEM((tm, D_MODEL), F32), pltpu.VMEM((tm, D_MODEL), F32),
                            pltpu.SemaphoreType.DMA, pltpu.SemaphoreType.DMA],
        ),
        out_shape=jax.ShapeDtypeStruct((T, D_MODEL), F32),
        compiler_params=_params("arbitrary"),
        name="moe_combine",
    )(dest, x1, ys)


def _mla_dense_layer(x2, pos2, batch, seq, attn_norm, w_in, q_a_norm, kv_a_norm, w_uq, w_ukv, q_norm,
                     k_norm, w_o, ffn_norm, w_gate, w_up, w_down):
    q, k, v = _mla_prep(x2, pos2, attn_norm, w_in, q_a_norm, kv_a_norm, w_uq, w_ukv, q_norm, k_norm)
    o = _mla_attn(q, k, v, batch, seq)
    return _proj_dense_ffn(x2, o, w_o, ffn_norm, w_gate, w_up, w_down)


def _band_moe_layer(x2, batch, seq, attn_norm, w_qkv, q_norm, k_norm, rel_bias, w_o, ffn_norm, router,
                    w_gate, w_up, w_down):
    q, k, v = _ca_prep(x2, attn_norm, w_qkv, q_norm, k_norm)
    o = _band_attn(q, k, v, rel_bias, batch, seq)
    x1, h, idx, wts = _proj_router(x2, o, w_o, ffn_norm, router)
    tm = ROW_TILE
    row_token, row_gate, tile_expert, tile_valid, dest = _moe_plan(idx[:, :TOP_K], wts[:, :TOP_K], tm)
    xs = _gather_rows(h, row_token, tile_valid, tm)
    ys = _moe_ffn(xs, row_gate, tile_expert, tile_valid, w_gate, w_up, w_down, tm)
    return _combine(x1, ys, dest, tm)


def kernel(x, positions, l0_attn_norm, l0_mla_w_in, l0_mla_q_a_norm, l0_mla_kv_a_norm, l0_mla_w_uq, l0_mla_w_ukv, l0_mla_q_norm, l0_mla_k_norm, l0_mla_w_o, l0_ffn_norm, l0_ffn_w_gate, l0_ffn_w_up, l0_ffn_w_down, l1_attn_norm, l1_ca_w_qkv, l1_ca_q_norm, l1_ca_k_norm, l1_ca_rel_bias, l1_ca_w_o, l1_ffn_norm, l1_moe_router, l1_moe_w_gate, l1_moe_w_up, l1_moe_w_down, l2_attn_norm, l2_mla_w_in, l2_mla_q_a_norm, l2_mla_kv_a_norm, l2_mla_w_uq, l2_mla_w_ukv, l2_mla_q_norm, l2_mla_k_norm, l2_mla_w_o, l2_ffn_norm, l2_ffn_w_gate, l2_ffn_w_up, l2_ffn_w_down, l3_attn_norm, l3_ca_w_qkv, l3_ca_q_norm, l3_ca_k_norm, l3_ca_rel_bias, l3_ca_w_o, l3_ffn_norm, l3_moe_router, l3_moe_w_gate, l3_moe_w_up, l3_moe_w_down):
    batch, seq, d = x.shape
    assert d == D_MODEL and seq % ATTN_Q_TILE == 0 and seq >= BAND_KEYS
    assert (batch * seq) % ROW_TILE == 0
    x2 = x.reshape(batch * seq, d)
    pos2 = positions.reshape(batch * seq, 1)
    x2 = _mla_dense_layer(x2, pos2, batch, seq, l0_attn_norm, l0_mla_w_in, l0_mla_q_a_norm, l0_mla_kv_a_norm,
                          l0_mla_w_uq, l0_mla_w_ukv, l0_mla_q_norm, l0_mla_k_norm, l0_mla_w_o, l0_ffn_norm,
                          l0_ffn_w_gate, l0_ffn_w_up, l0_ffn_w_down)
    x2 = _band_moe_layer(x2, batch, seq, l1_attn_norm, l1_ca_w_qkv, l1_ca_q_norm, l1_ca_k_norm, l1_ca_rel_bias,
                         l1_ca_w_o, l1_ffn_norm, l1_moe_router, l1_moe_w_gate, l1_moe_w_up, l1_moe_w_down)
    x2 = _mla_dense_layer(x2, pos2, batch, seq, l2_attn_norm, l2_mla_w_in, l2_mla_q_a_norm, l2_mla_kv_a_norm,
                          l2_mla_w_uq, l2_mla_w_ukv, l2_mla_q_norm, l2_mla_k_norm, l2_mla_w_o, l2_ffn_norm,
                          l2_ffn_w_gate, l2_ffn_w_up, l2_ffn_w_down)
    x2 = _band_moe_layer(x2, batch, seq, l3_attn_norm, l3_ca_w_qkv, l3_ca_q_norm, l3_ca_k_norm, l3_ca_rel_bias,
                         l3_ca_w_o, l3_ffn_norm, l3_moe_router, l3_moe_w_gate, l3_moe_w_up, l3_moe_w_down)
    return x2.reshape(batch, seq, d)
```

```python
import functools

import numpy as np
import jax
import jax.numpy as jnp
from jax import lax
from jax.experimental import pallas as pl
from jax.experimental.pallas import tpu as pltpu

F32 = jnp.float32
BF16 = jnp.bfloat16

D_MODEL = 1024
CHUNK = 64
RMS_EPS = 1e-6

MLA_HEADS = 8
MLA_NOPE = 128
MLA_ROPE = 64
MLA_V = 128
MLA_QK = MLA_NOPE + MLA_ROPE
Q_LORA = 256
KV_LORA = 128
ROPE_THETA = 10000.0

CA_HEADS = 16
CA_HEAD_DIM = 64
LEFT_CHUNKS = 8
MAX_REL = 256

N_EXPERTS = 8
TOP_K = 2

LANES = 128
SUBLANES = 8
VMEM_LIMIT = 56 * 1024 * 1024

ROW_TILE = 512
ATTN_Q_TILE = 256
BAND_KEYS = 3 * ATTN_Q_TILE
BAND_ROLL = ATTN_Q_TILE + BAND_KEYS
DENSE_FF_TILE = 1408
MOE_FF_TILE = 1792
ZERO_ROWS = 256
NEG_INF = float("-inf")


def _dot(a, b):
    return jnp.dot(a, b, preferred_element_type=F32)


def _dot_nt(a, b):
    return lax.dot_general(a, b, (((1,), (1,)), ((), ())), preferred_element_type=F32)


def _rms(x, g):
    return x * lax.rsqrt(jnp.mean(x * x, axis=-1, keepdims=True) + RMS_EPS) * g


def _silu(g):
    return g / (1.0 + jnp.exp(-g))


def _params(*sem):
    return pltpu.CompilerParams(dimension_semantics=sem, vmem_limit_bytes=VMEM_LIMIT)


def _mla_prep_kernel(x_ref, pos_ref, gn_ref, win_ref, qan_ref, kvan_ref, wuq_ref, wukv_ref,
                     invf_ref, qtab_ref, ktab_ref, q_out, k_out, v_out):
    h = _rms(x_ref[...], gn_ref[...]).astype(BF16)
    a = _dot(h, win_ref[...])
    qn = _rms(a[:, :Q_LORA], qan_ref[...]).astype(BF16)
    kvn = _rms(a[:, Q_LORA:Q_LORA + KV_LORA], kvan_ref[...]).astype(BF16)
    kr2 = a[:, Q_LORA + KV_LORA:]
    q_all = _dot(qn, wuq_ref[...])
    kv_all = _dot(kvn, wukv_ref[...])

    ang = pos_ref[...].astype(F32) * invf_ref[...]
    cos = jnp.cos(ang)
    sin = jnp.sin(ang)
    lane = lax.broadcasted_iota(jnp.int32, (1, LANES), 1)
    rope_lane = lane < MLA_ROPE

    q_gn, q_ga, q_gb = qtab_ref[0:1, :], qtab_ref[1:2, :], qtab_ref[2:3, :]
    k_gn, k_ga, k_gb = ktab_ref[0:1, :], ktab_ref[1:2, :], ktab_ref[2:3, :]
    q_cos, q_sin = cos * q_ga, sin * q_gb
    k_rot = kr2 * (cos * k_ga) + pltpu.roll(kr2, MLA_ROPE, 1) * (sin * k_gb)
    kr_ss = jnp.sum(jnp.where(rope_lane, kr2 * kr2, 0.0), axis=-1, keepdims=True)
    scale = MLA_QK ** -0.5

    for hd in range(MLA_HEADS):
        lo = hd * 2 * LANES
        qa = q_all[:, lo:lo + LANES]
        qb = q_all[:, lo + LANES:lo + 2 * LANES]
        ss = jnp.sum(qa * qa + jnp.where(rope_lane, qb * qb, 0.0), axis=-1, keepdims=True)
        r = lax.rsqrt(ss * (1.0 / MLA_QK) + RMS_EPS) * scale
        q_rot = (qb * q_cos + pltpu.roll(qb, MLA_ROPE, 1) * q_sin) * r
        q_out[hd, :, :MLA_NOPE] = (qa * q_gn * r).astype(BF16)
        q_out[hd, :, MLA_NOPE:] = q_rot[:, :MLA_ROPE].astype(BF16)

        kn = kv_all[:, lo:lo + LANES]
        ssk = jnp.sum(kn * kn, axis=-1, keepdims=True) + kr_ss
        rk = lax.rsqrt(ssk * (1.0 / MLA_QK) + RMS_EPS)
        k_out[hd, :, :MLA_NOPE] = (kn * k_gn * rk).astype(BF16)
        k_out[hd, :, MLA_NOPE:] = (k_rot * rk)[:, :MLA_ROPE].astype(BF16)
        v_out[hd] = kv_all[:, lo + LANES:lo + 2 * LANES].astype(BF16)


def _rope_partner():
    d = np.arange(MLA_ROPE)
    half = MLA_ROPE // 2
    partner = np.where(d < half, d + half, d - half)
    sign = np.where(d < half, -1.0, 1.0).astype(np.float32)
    return partner, sign


def _mla_tables(g):
    partner, sign = _rope_partner()
    zeros = jnp.zeros((LANES - MLA_ROPE,), F32)
    g_rope = g[MLA_NOPE:]
    rows = [g[:MLA_NOPE],
            jnp.concatenate([g_rope, zeros]),
            jnp.concatenate([g_rope[partner] * sign, zeros])]
    rows += [jnp.zeros((LANES,), F32)] * 5
    return jnp.stack(rows)


def _mla_prep(x2, pos2, attn_norm, w_in, q_a_norm, kv_a_norm, w_uq, w_ukv, q_norm, k_norm):
    T = x2.shape[0]
    tm = ROW_TILE
    partner, _ = _rope_partner()
    rope0 = Q_LORA + KV_LORA
    w_in_ext = jnp.concatenate([w_in, w_in[:, rope0 + partner]], axis=1).astype(BF16)
    wq = w_uq.reshape(Q_LORA, MLA_HEADS, MLA_QK)
    w_uq_ext = jnp.concatenate([wq, wq[:, :, MLA_NOPE + partner]], axis=2)
    w_uq_ext = w_uq_ext.reshape(Q_LORA, MLA_HEADS * 2 * LANES).astype(BF16)
    w_ukv_b = w_ukv.astype(BF16)
    inv = 1.0 / (ROPE_THETA ** (jnp.arange(0, MLA_ROPE, 2, dtype=F32) / MLA_ROPE))
    invf = jnp.concatenate([inv, inv, jnp.zeros((LANES - MLA_ROPE,), F32)])[None, :]

    full = lambda shape: pl.BlockSpec(shape, lambda i: (0,) * len(shape))
    out_qk = jax.ShapeDtypeStruct((MLA_HEADS, T, MLA_QK), BF16)
    out_v = jax.ShapeDtypeStruct((MLA_HEADS, T, MLA_V), BF16)
    return pl.pallas_call(
        _mla_prep_kernel,
        grid=(T // tm,),
        in_specs=[
            pl.BlockSpec((tm, D_MODEL), lambda i: (i, 0)),
            pl.BlockSpec((tm, 1), lambda i: (i, 0)),
            full((1, D_MODEL)),
            full((D_MODEL, 4 * LANES)),
            full((1, Q_LORA)),
            full((1, KV_LORA)),
            full((Q_LORA, MLA_HEADS * 2 * LANES)),
            full((KV_LORA, MLA_HEADS * 2 * LANES)),
            full((1, LANES)),
            full((8, LANES)),
            full((8, LANES)),
        ],
        out_specs=[
            pl.BlockSpec((MLA_HEADS, tm, MLA_QK), lambda i: (0, i, 0)),
            pl.BlockSpec((MLA_HEADS, tm, MLA_QK), lambda i: (0, i, 0)),
            pl.BlockSpec((MLA_HEADS, tm, MLA_V), lambda i: (0, i, 0)),
        ],
        out_shape=[out_qk, out_qk, out_v],
        compiler_params=_params("parallel"),
        name="mla_prep",
    )(x2, pos2, attn_norm[None, :], w_in_ext, q_a_norm[None, :], kv_a_norm[None, :],
      w_uq_ext, w_ukv_b, invf, _mla_tables(q_norm), _mla_tables(k_norm))


def _mla_attn_kernel(q_ref, k_ref, v_ref, o_ref, *, seq, tq):
    row_c = lax.broadcasted_iota(jnp.int32, (tq, tq), 0) // CHUNK
    col_c = lax.broadcasted_iota(jnp.int32, (tq, tq), 1) // CHUNK
    diag_mask = col_c <= row_c
    for qi in range(seq // tq):
        lo, hi = qi * tq, (qi + 1) * tq
        q = q_ref[0, lo:hi, :]
        s_d = jnp.where(diag_mask, _dot_nt(q, k_ref[0, lo:hi, :]), NEG_INF)
        m = jnp.max(s_d, axis=-1, keepdims=True)
        if qi > 0:
            s_o = _dot_nt(q, k_ref[0, :lo, :])
            m = jnp.maximum(m, jnp.max(s_o, axis=-1, keepdims=True))
        p_d = jnp.exp(s_d - m)
        l = jnp.sum(p_d, axis=-1, keepdims=True)
        o = _dot(p_d.astype(BF16), v_ref[0, lo:hi, :])
        if qi > 0:
            p_o = jnp.exp(s_o - m)
            l = l + jnp.sum(p_o, axis=-1, keepdims=True)
            o = o + _dot(p_o.astype(BF16), v_ref[0, :lo, :])
        o_ref[lo:hi, :] = (o / l).astype(BF16)


def _mla_attn(q, k, v, batch, seq):
    T = batch * seq
    kern = functools.partial(_mla_attn_kernel, seq=seq, tq=ATTN_Q_TILE)
    return pl.pallas_call(
        kern,
        grid=(MLA_HEADS, batch),
        in_specs=[
            pl.BlockSpec((1, seq, MLA_QK), lambda h, b: (h, b, 0)),
            pl.BlockSpec((1, seq, MLA_QK), lambda h, b: (h, b, 0)),
            pl.BlockSpec((1, seq, MLA_V), lambda h, b: (h, b, 0)),
        ],
        out_specs=pl.BlockSpec((seq, MLA_V), lambda h, b: (b, h)),
        out_shape=jax.ShapeDtypeStruct((T, MLA_HEADS * MLA_V), BF16),
        compiler_params=_params("parallel", "parallel"),
        name="mla_attn",
    )(q, k, v)


def _proj_dense_ffn_kernel(x_ref, o_ref, wo_ref, gn_ref, wg_ref, wu_ref, wd_ref, out_ref, hs_ref):
    @pl.when(pl.program_id(1) == 0)
    def _():
        x1 = x_ref[...] + _dot(o_ref[...], wo_ref[...])
        out_ref[...] = x1
        hs_ref[...] = _rms(x1, gn_ref[...]).astype(BF16)

    h = hs_ref[...]
    a = (_silu(_dot(h, wg_ref[...])) * _dot(h, wu_ref[...])).astype(BF16)
    out_ref[...] += _dot(a, wd_ref[...])


def _proj_dense_ffn(x2, o2, w_o, ffn_norm, w_gate, w_up, w_down):
    T = x2.shape[0]
    ff = w_gate.shape[1]
    tm, tf = ROW_TILE, DENSE_FF_TILE
    return pl.pallas_call(
        _proj_dense_ffn_kernel,
        grid=(T // tm, ff // tf),
        in_specs=[
            pl.BlockSpec((tm, D_MODEL), lambda i, f: (i, 0)),
            pl.BlockSpec((tm, D_MODEL), lambda i, f: (i, 0)),
            pl.BlockSpec((D_MODEL, D_MODEL), lambda i, f: (0, 0)),
            pl.BlockSpec((1, D_MODEL), lambda i, f: (0, 0)),
            pl.BlockSpec((D_MODEL, tf), lambda i, f: (0, f)),
            pl.BlockSpec((D_MODEL, tf), lambda i, f: (0, f)),
            pl.BlockSpec((tf, D_MODEL), lambda i, f: (f, 0)),
        ],
        out_specs=pl.BlockSpec((tm, D_MODEL), lambda i, f: (i, 0)),
        out_shape=jax.ShapeDtypeStruct((T, D_MODEL), F32),
        scratch_shapes=[pltpu.VMEM((tm, D_MODEL), BF16)],
        compiler_params=_params("parallel", "arbitrary"),
        name="proj_dense_ffn",
    )(x2, o2, w_o.astype(BF16), ffn_norm[None, :], w_gate.astype(BF16), w_up.astype(BF16),
      w_down.astype(BF16))


def _ca_prep_kernel(x_ref, gn_ref, w_ref, gq_ref, gk_ref, q_out, k_out, v_out):
    h = _rms(x_ref[...], gn_ref[...]).astype(BF16)
    qkv = _dot(h, w_ref[...])
    q_out[...] = (qkv[:, :D_MODEL] * gq_ref[...]).astype(BF16)
    k_out[...] = (qkv[:, D_MODEL:2 * D_MODEL] * gk_ref[...]).astype(BF16)
    v_out[...] = qkv[:, 2 * D_MODEL:].astype(BF16)


def _ca_prep(x2, attn_norm, w_qkv, q_norm, k_norm):
    T = x2.shape[0]
    tm = ROW_TILE
    full = lambda shape: pl.BlockSpec(shape, lambda i: (0,) * len(shape))
    out = jax.ShapeDtypeStruct((T, D_MODEL), BF16)
    row = pl.BlockSpec((tm, D_MODEL), lambda i: (i, 0))
    return pl.pallas_call(
        _ca_prep_kernel,
        grid=(T // tm,),
        in_specs=[row, full((1, D_MODEL)), full((D_MODEL, 3 * D_MODEL)), full((1, D_MODEL)),
                  full((1, D_MODEL))],
        out_specs=[row, row, row],
        out_shape=[out, out, out],
        compiler_params=_params("parallel"),
        name="ca_prep",
    )(x2, attn_norm[None, :], w_qkv.astype(BF16), jnp.tile(q_norm, CA_HEADS)[None, :],
      jnp.tile(k_norm, CA_HEADS)[None, :])


def _band_attn_kernel(q_ref, k_ref, v_ref, w_ref, o_ref, tab_ref, qn_ref, kn_ref, *, seq, tq):
    @pl.when(pl.program_id(1) == 0)
    def _():
        row = lax.broadcasted_iota(jnp.int32, (tq, BAND_KEYS), 0)
        col = lax.broadcasted_iota(jnp.int32, (tq, BAND_KEYS), 1)
        shift = CHUNK.bit_length() - 1
        for hh in range(2):
            for var in range(3):
                w = w_ref[0, 3 * hh + var:3 * hh + var + 1, :]
                t = pltpu.roll(jnp.broadcast_to(w, (tq, BAND_ROLL)), 0, 1, stride=1, stride_axis=0)
                gap = lax.shift_right_logical(row + var * tq, shift) - lax.shift_right_logical(col, shift)
                tab_ref[hh, var] = jnp.where((gap >= 0) & (gap <= LEFT_CHUNKS), t[:, :BAND_KEYS], NEG_INF)

    lane = lax.broadcasted_iota(jnp.int32, (1, LANES), 1)
    head0 = lane < CA_HEAD_DIM

    def head_normed(x_ref, scale):
        x = x_ref[...].astype(F32)
        sq = x * x
        ss0 = jnp.sum(jnp.where(head0, sq, 0.0), axis=-1, keepdims=True)
        ss1 = jnp.sum(jnp.where(head0, 0.0, sq), axis=-1, keepdims=True)
        r = jnp.where(head0, lax.rsqrt(ss0 * (1.0 / CA_HEAD_DIM) + RMS_EPS),
                      lax.rsqrt(ss1 * (1.0 / CA_HEAD_DIM) + RMS_EPS))
        return (x * (r * scale)).astype(BF16)

    qn_ref[...] = head_normed(q_ref, CA_HEAD_DIM ** -0.5)
    kn_ref[...] = head_normed(k_ref, 1.0)

    for qi in range(seq // tq):
        var = min(qi, 2)
        ks = max(qi - 2, 0) * tq
        q = qn_ref[qi * tq:(qi + 1) * tq, :]
        k = kn_ref[ks:ks + BAND_KEYS, :]
        v = v_ref[ks:ks + BAND_KEYS, :]
        zero = jnp.zeros_like(q)
        qq = jnp.concatenate([jnp.where(head0, q, zero), jnp.where(head0, zero, q)], axis=0)
        s = _dot_nt(qq, k)
        ps, ls = [], []
        for hh in range(2):
            sh = s[hh * tq:(hh + 1) * tq] + tab_ref[hh, var]
            p = jnp.exp(sh - jnp.max(sh, axis=-1, keepdims=True))
            ls.append(jnp.sum(p, axis=-1, keepdims=True))
            ps.append(p.astype(BF16))
        oo = _dot(jnp.concatenate(ps, axis=0), v)
        o = jnp.where(head0, oo[:tq] / ls[0], oo[tq:] / ls[1])
        o_ref[qi * tq:(qi + 1) * tq, :] = o.astype(BF16)


def _band_bias_rows(rel_bias):
    tq = ATTN_Q_TILE
    rb = rel_bias.astype(F32)
    r_max = 3 * tq - 1
    n_far = r_max - MAX_REL
    n_m = r_max + BAND_KEYS
    n_neg = n_m - n_far - rb.shape[1]
    heads = rb.shape[0]
    m = jnp.concatenate([jnp.broadcast_to(rb[:, -1:], (heads, n_far)), rb[:, ::-1],
                         jnp.broadcast_to(rb[:, :1], (heads, n_neg))], axis=1)
    rows = []
    for var in range(3):
        base = r_max - var * tq
        rows.append(jnp.concatenate([m[:, base:base + BAND_KEYS], jnp.zeros((heads, 1), F32),
                                     m[:, base - (tq - 1):base]], axis=1))
    w = jnp.stack(rows, axis=1).reshape(heads // 2, 6, BAND_ROLL)
    return jnp.concatenate([w, jnp.zeros((heads // 2, 2, BAND_ROLL), F32)], axis=1)


def _band_attn(q, k, v, rel_bias, batch, seq):
    T = batch * seq
    tq = ATTN_Q_TILE
    kern = functools.partial(_band_attn_kernel, seq=seq, tq=tq)
    slab = pl.BlockSpec((seq, LANES), lambda hp, b: (b, hp))
    return pl.pallas_call(
        kern,
        grid=(CA_HEADS // 2, batch),
        in_specs=[slab, slab, slab, pl.BlockSpec((1, 8, BAND_ROLL), lambda hp, b: (hp, 0, 0))],
        out_specs=slab,
        out_shape=jax.ShapeDtypeStruct((T, D_MODEL), BF16),
        scratch_shapes=[pltpu.VMEM((2, 3, tq, BAND_KEYS), F32), pltpu.VMEM((seq, LANES), BF16),
                        pltpu.VMEM((seq, LANES), BF16)],
        compiler_params=_params("arbitrary", "arbitrary"),
        name="band_attn",
    )(q, k, v, _band_bias_rows(rel_bias))


def _proj_router_kernel(x_ref, o_ref, wo_ref, gn_ref, rhi_ref, rlo_ref, x1_out, h_out, idx_out, w_out):
    x1 = x_ref[...] + _dot(o_ref[...], wo_ref[...])
    x1_out[...] = x1
    h = _rms(x1, gn_ref[...])
    h_out[...] = h
    h_hi = h.astype(BF16)
    h_lo = (h - h_hi.astype(F32)).astype(BF16)
    logits = _dot(h_hi, rhi_ref[...]) + (_dot(h_lo, rhi_ref[...]) + _dot(h_hi, rlo_ref[...]))

    lane = lax.broadcasted_iota(jnp.int32, logits.shape, 1)
    lane_f = lane.astype(F32)
    lg = jnp.where(lane < N_EXPERTS, logits, NEG_INF)
    m1 = jnp.max(lg, axis=-1, keepdims=True)
    i1 = jnp.min(jnp.where(lg == m1, lane_f, float(LANES)), axis=-1, keepdims=True)
    lg2 = jnp.where(lane_f == i1, NEG_INF, lg)
    m2 = jnp.max(lg2, axis=-1, keepdims=True)
    i2 = jnp.min(jnp.where(lg2 == m2, lane_f, float(LANES)), axis=-1, keepdims=True)
    e = jnp.exp(m2 - m1)
    w1 = 1.0 / (1.0 + e)
    w2 = e / (1.0 + e)
    idx_out[...] = jnp.where(lane == 0, i1, jnp.where(lane == 1, i2, 0.0)).astype(jnp.int32)
    w_out[...] = jnp.where(lane == 0, w1, jnp.where(lane == 1, w2, 0.0))


def _proj_router(x2, o2, w_o, ffn_norm, router):
    T = x2.shape[0]
    tm = ROW_TILE
    r_pad = jnp.zeros((D_MODEL, LANES), F32).at[:, :N_EXPERTS].set(router)
    r_hi = r_pad.astype(BF16)
    r_lo = (r_pad - r_hi.astype(F32)).astype(BF16)
    full = lambda shape: pl.BlockSpec(shape, lambda i: (0,) * len(shape))
    row = pl.BlockSpec((tm, D_MODEL), lambda i: (i, 0))
    small = pl.BlockSpec((tm, LANES), lambda i: (i, 0))
    return pl.pallas_call(
        _proj_router_kernel,
        grid=(T // tm,),
        in_specs=[row, row, full((D_MODEL, D_MODEL)), full((1, D_MODEL)), full((D_MODEL, LANES)),
                  full((D_MODEL, LANES))],
        out_specs=[row, row, small, small],
        out_shape=[jax.ShapeDtypeStruct((T, D_MODEL), F32), jax.ShapeDtypeStruct((T, D_MODEL), F32),
                   jax.ShapeDtypeStruct((T, LANES), jnp.int32), jax.ShapeDtypeStruct((T, LANES), F32)],
        compiler_params=_params("parallel"),
        name="proj_router",
    )(x2, o2, w_o.astype(BF16), ffn_norm[None, :], r_hi, r_lo)


def _moe_plan(top_idx, tm):
    T = top_idx.shape[0]
    n_assign = T * TOP_K
    n_tiles = n_assign // tm + N_EXPERTS
    e_flat = top_idx.reshape(1, n_assign)
    onehot = (e_flat == jnp.arange(N_EXPERTS, dtype=jnp.int32)[:, None]).astype(jnp.int32)
    csum = jnp.cumsum(onehot, axis=1)
    counts = csum[:, -1]
    tiles_per = (counts + tm - 1) // tm
    tile_end = jnp.cumsum(tiles_per)
    tile_start = tile_end - tiles_per
    dest = jnp.sum(onehot * (csum - 1 + tile_start[:, None] * tm), axis=0)

    tile_id = jnp.arange(n_tiles, dtype=jnp.int32)
    tile_valid = (tile_id < tile_end[-1]).astype(jnp.int32)
    tile_expert = jnp.minimum(jnp.sum((tile_id[:, None] >= tile_end[None, :]).astype(jnp.int32), axis=1),
                              N_EXPERTS - 1)
    pad = jnp.stack([tile_start * tm + counts, tiles_per * tm - counts], axis=1).reshape(-1)
    pad = jnp.concatenate([pad, tile_end[-1:]])
    return dest.astype(jnp.int32), pad.astype(jnp.int32), tile_expert, tile_valid


def _dispatch_kernel(dest_ref, pad_ref, h_ref, xs_hbm, zbuf, sem, zsem, *, tm, n_tiles):
    i = pl.program_id(0)

    @pl.when(i == 0)
    def _():
        zbuf[...] = jnp.zeros_like(zbuf)
        for t in range(n_tiles - N_EXPERTS, n_tiles):
            @pl.when(t >= pad_ref[2 * N_EXPERTS])
            def _(t=t):
                for part in range(tm // ZERO_ROWS):
                    cp = pltpu.make_async_copy(
                        zbuf, xs_hbm.at[pl.ds(t * tm + part * ZERO_ROWS, ZERO_ROWS), :], zsem)
                    cp.start()
                    cp.wait()
        for e in range(N_EXPERTS):
            start = pad_ref[2 * e]
            length = pad_ref[2 * e + 1]
            end = start + length
            for r in range(SUBLANES - 1):
                @pl.when(r < (length & (SUBLANES - 1)))
                def _(r=r):
                    cp = pltpu.make_async_copy(zbuf.at[pl.ds(0, 1), :], xs_hbm.at[pl.ds(start + r, 1), :], zsem)
                    cp.start()
                    cp.wait()
            n = ZERO_ROWS
            while n >= SUBLANES:
                offset = pl.multiple_of(end - (length & ~(2 * n - 1)) - n, SUBLANES)

                @pl.when((length & n) != 0)
                def _(n=n, offset=offset):
                    cp = pltpu.make_async_copy(zbuf.at[pl.ds(0, n), :], xs_hbm.at[pl.ds(offset, n), :], zsem)
                    cp.start()
                    cp.wait()
                n //= 2

    def issue(r, carry):
        a = (i * tm + r) * TOP_K
        src = h_ref.at[pl.ds(r, 1), :]
        pltpu.make_async_copy(src, xs_hbm.at[pl.ds(dest_ref[a], 1), :], sem).start()
        pltpu.make_async_copy(src, xs_hbm.at[pl.ds(dest_ref[a + 1], 1), :], sem).start()
        return carry
    lax.fori_loop(0, tm, issue, 0, unroll=8)
    for _ in range(TOP_K):
        pltpu.make_async_copy(h_ref, xs_hbm.at[pl.ds(0, tm), :], sem).wait()


def _dispatch(h2, dest, pad, n_rows, tm):
    T = h2.shape[0]
    assert tm == 2 * ZERO_ROWS
    kern = functools.partial(_dispatch_kernel, tm=tm, n_tiles=n_rows // tm)
    return pl.pallas_call(
        kern,
        grid_spec=pltpu.PrefetchScalarGridSpec(
            num_scalar_prefetch=2,
            grid=(T // tm,),
            in_specs=[pl.BlockSpec((tm, D_MODEL), lambda i, dest, pad: (i, 0))],
            out_specs=pl.BlockSpec(memory_space=pl.ANY),
            scratch_shapes=[pltpu.VMEM((ZERO_ROWS, D_MODEL), F32), pltpu.SemaphoreType.DMA,
                            pltpu.SemaphoreType.DMA],
        ),
        out_shape=jax.ShapeDtypeStruct((n_rows, D_MODEL), F32),
        compiler_params=_params("arbitrary"),
        name="moe_dispatch",
    )(dest, pad, h2)


def _moe_ffn_kernel(te_ref, tv_ref, xs_ref, wg_ref, wu_ref, wd_ref, ys_ref, hs_ref):
    i = pl.program_id(0)

    @pl.when(pl.program_id(1) == 0)
    def _():
        ys_ref[...] = jnp.zeros_like(ys_ref)
        hs_ref[...] = xs_ref[...].astype(BF16)

    @pl.when(tv_ref[i] > 0)
    def _():
        h = hs_ref[...]
        a = (_silu(_dot(h, wg_ref[0])) * _dot(h, wu_ref[0])).astype(BF16)
        ys_ref[...] += _dot(a, wd_ref[0])


def _moe_ffn(xs, tile_expert, tile_valid, w_gate, w_up, w_down, tm):
    n_rows = xs.shape[0]
    ff = w_gate.shape[2]
    tf = MOE_FF_TILE
    nf = ff // tf
    ff_blk = lambda i, f, tv: f * tv[i] + (nf - 1) * (1 - tv[i])
    return pl.pallas_call(
        _moe_ffn_kernel,
        grid_spec=pltpu.PrefetchScalarGridSpec(
            num_scalar_prefetch=2,
            grid=(n_rows // tm, nf),
            in_specs=[
                pl.BlockSpec((tm, D_MODEL), lambda i, f, te, tv: (i * tv[i], 0)),
                pl.BlockSpec((1, D_MODEL, tf), lambda i, f, te, tv: (te[i], 0, ff_blk(i, f, tv))),
                pl.BlockSpec((1, D_MODEL, tf), lambda i, f, te, tv: (te[i], 0, ff_blk(i, f, tv))),
                pl.BlockSpec((1, tf, D_MODEL), lambda i, f, te, tv: (te[i], ff_blk(i, f, tv), 0)),
            ],
            out_specs=pl.BlockSpec((tm, D_MODEL), lambda i, f, te, tv: (i, 0)),
            scratch_shapes=[pltpu.VMEM((tm, D_MODEL), BF16)],
        ),
        out_shape=jax.ShapeDtypeStruct((n_rows, D_MODEL), F32),
        compiler_params=_params("arbitrary", "arbitrary"),
        name="moe_ffn",
    )(tile_expert, tile_valid, xs, w_gate.astype(BF16), w_up.astype(BF16), w_down.astype(BF16))


def _combine_kernel(pos_ref, x1_ref, wts_ref, ys_hbm, out_ref, buf0, buf1, sem0, sem1, *, tm):
    i = pl.program_id(0)

    def issue(r, carry):
        a = (i * tm + r) * TOP_K
        pltpu.make_async_copy(ys_hbm.at[pl.ds(pos_ref[a], 1), :], buf0.at[pl.ds(r, 1), :], sem0).start()
        pltpu.make_async_copy(ys_hbm.at[pl.ds(pos_ref[a + 1], 1), :], buf1.at[pl.ds(r, 1), :], sem1).start()
        return carry
    lax.fori_loop(0, tm, issue, 0, unroll=8)
    pltpu.make_async_copy(ys_hbm.at[pl.ds(0, tm), :], buf0, sem0).wait()
    pltpu.make_async_copy(ys_hbm.at[pl.ds(0, tm), :], buf1, sem1).wait()
    wts = wts_ref[...]
    out_ref[...] = x1_ref[...] + (wts[:, 0:1] * buf0[...] + wts[:, 1:2] * buf1[...])


def _combine(x1, wts, ys, dest, tm):
    T = x1.shape[0]
    kern = functools.partial(_combine_kernel, tm=tm)
    return pl.pallas_call(
        kern,
        grid_spec=pltpu.PrefetchScalarGridSpec(
            num_scalar_prefetch=1,
            grid=(T // tm,),
            in_specs=[pl.BlockSpec((tm, D_MODEL), lambda i, pos: (i, 0)),
                      pl.BlockSpec((tm, LANES), lambda i, pos: (i, 0)),
                      pl.BlockSpec(memory_space=pl.ANY)],
            out_specs=pl.BlockSpec((tm, D_MODEL), lambda i, pos: (i, 0)),
            scratch_shapes=[pltpu.VMEM((tm, D_MODEL), F32), pltpu.VMEM((tm, D_MODEL), F32),
                            pltpu.SemaphoreType.DMA, pltpu.SemaphoreType.DMA],
        ),
        out_shape=jax.ShapeDtypeStruct((T, D_MODEL), F32),
        compiler_params=_params("arbitrary"),
        name="moe_combine",
    )(dest, x1, wts, ys)


def _mla_dense_layer(x2, pos2, batch, seq, attn_norm, w_in, q_a_norm, kv_a_norm, w_uq, w_ukv, q_norm,
                     k_norm, w_o, ffn_norm, w_gate, w_up, w_down):
    q, k, v = _mla_prep(x2, pos2, attn_norm, w_in, q_a_norm, kv_a_norm, w_uq, w_ukv, q_norm, k_norm)
    o = _mla_attn(q, k, v, batch, seq)
    return _proj_dense_ffn(x2, o, w_o, ffn_norm, w_gate, w_up, w_down)


def _band_moe_layer(x2, batch, seq, attn_norm, w_qkv, q_norm, k_norm, rel_bias, w_o, ffn_norm, router,
                    w_gate, w_up, w_down):
    q, k, v = _ca_prep(x2, attn_norm, w_qkv, q_norm, k_norm)
    o = _band_attn(q, k, v, rel_bias, batch, seq)
    x1, h, idx, wts = _proj_router(x2, o, w_o, ffn_norm, router)
    tm = ROW_TILE
    dest, pad, tile_expert, tile_valid = _moe_plan(idx[:, :TOP_K], tm)
    xs = _dispatch(h, dest, pad, tile_expert.shape[0] * tm, tm)
    ys = _moe_ffn(xs, tile_expert, tile_valid, w_gate, w_up, w_down, tm)
    return _combine(x1, wts, ys, dest, tm)


def kernel(x, positions, l0_attn_norm, l0_mla_w_in, l0_mla_q_a_norm, l0_mla_kv_a_norm, l0_mla_w_uq, l0_mla_w_ukv, l0_mla_q_norm, l0_mla_k_norm, l0_mla_w_o, l0_ffn_norm, l0_ffn_w_gate, l0_ffn_w_up, l0_ffn_w_down, l1_attn_norm, l1_ca_w_qkv, l1_ca_q_norm, l1_ca_k_norm, l1_ca_rel_bias, l1_ca_w_o, l1_ffn_norm, l1_moe_router, l1_moe_w_gate, l1_moe_w_up, l1_moe_w_down, l2_attn_norm, l2_mla_w_in, l2_mla_q_a_norm, l2_mla_kv_a_norm, l2_mla_w_uq, l2_mla_w_ukv, l2_mla_q_norm, l2_mla_k_norm, l2_mla_w_o, l2_ffn_norm, l2_ffn_w_gate, l2_ffn_w_up, l2_ffn_w_down, l3_attn_norm, l3_ca_w_qkv, l3_ca_q_norm, l3_ca_k_norm, l3_ca_rel_bias, l3_ca_w_o, l3_ffn_norm, l3_moe_router, l3_moe_w_gate, l3_moe_w_up, l3_moe_w_down):
    batch, seq, d = x.shape
    assert d == D_MODEL and seq % ATTN_Q_TILE == 0 and seq >= BAND_KEYS
    assert (batch * seq) % ROW_TILE == 0
    x2 = x.reshape(batch * seq, d)
    pos2 = positions.reshape(batch * seq, 1)
    x2 = _mla_dense_layer(x2, pos2, batch, seq, l0_attn_norm, l0_mla_w_in, l0_mla_q_a_norm, l0_mla_kv_a_norm,
                          l0_mla_w_uq, l0_mla_w_ukv, l0_mla_q_norm, l0_mla_k_norm, l0_mla_w_o, l0_ffn_norm,
                          l0_ffn_w_gate, l0_ffn_w_up, l0_ffn_w_down)
    x2 = _band_moe_layer(x2, batch, seq, l1_attn_norm, l1_ca_w_qkv, l1_ca_q_norm, l1_ca_k_norm, l1_ca_rel_bias,
                         l1_ca_w_o, l1_ffn_norm, l1_moe_router, l1_moe_w_gate, l1_moe_w_up, l1_moe_w_down)
    x2 = _mla_dense_layer(x2, pos2, batch, seq, l2_attn_norm, l2_mla_w_in, l2_mla_q_a_norm, l2_mla_kv_a_norm,
                          l2_mla_w_uq, l2_mla_w_ukv, l2_mla_q_norm, l2_mla_k_norm, l2_mla_w_o, l2_ffn_norm,
                          l2_ffn_w_gate, l2_ffn_w_up, l2_ffn_w_down)
    x2 = _band_moe_layer(x2, batch, seq, l3_attn_norm, l3_ca_w_qkv, l3_ca_q_norm, l3_ca_k_norm, l3_ca_rel_bias,
                         l3_ca_w_o, l3_ffn_norm, l3_moe_router, l3_moe_w_gate, l3_moe_w_up, l3_moe_w_down)
    return x2.reshape(batch, seq, d)
```

```python
import functools

import numpy as np
import jax
import jax.numpy as jnp
from jax import lax
from jax.experimental import pallas as pl
from jax.experimental.pallas import tpu as pltpu

F32 = jnp.float32
BF16 = jnp.bfloat16

D_MODEL = 1024
CHUNK = 64
RMS_EPS = 1e-6

MLA_HEADS = 8
MLA_NOPE = 128
MLA_ROPE = 64
MLA_V = 128
MLA_QK = MLA_NOPE + MLA_ROPE
Q_LORA = 256
KV_LORA = 128
ROPE_THETA = 10000.0

CA_HEADS = 16
CA_HEAD_DIM = 64
LEFT_CHUNKS = 8
MAX_REL = 256

N_EXPERTS = 8
TOP_K = 2

LANES = 128
SUBLANES = 8
VMEM_LIMIT = 56 * 1024 * 1024

ROW_TILE = 512
MLA_PREP_TILE = 1024
MLA_Q_TILE = 512
MLA_HEADS_PER_STEP = 2
ATTN_Q_TILE = 256
BAND_KEYS = 3 * ATTN_Q_TILE
BAND_ROLL = ATTN_Q_TILE + BAND_KEYS
MOE_FF_TILE = 1792
DMA_GROUP = 8
ZERO_ROWS = 256
NEG_INF = float("-inf")


def _dot(a, b):
    return jnp.dot(a, b, preferred_element_type=F32)


def _dot_nt(a, b):
    return lax.dot_general(a, b, (((1,), (1,)), ((), ())), preferred_element_type=F32)


def _rms(x, g):
    return x * lax.rsqrt(jnp.mean(x * x, axis=-1, keepdims=True) + RMS_EPS) * g


def _silu(g):
    return g / (1.0 + jnp.exp(-g))


def _params(*sem):
    return pltpu.CompilerParams(dimension_semantics=sem, vmem_limit_bytes=VMEM_LIMIT)


def _mla_prep_kernel(x_ref, pos_ref, gn_ref, win_ref, qan_ref, kvan_ref, wuq_ref, wukv_ref,
                     invf_ref, qtab_ref, ktab_ref, q_out, k_out, v_out):
    h = _rms(x_ref[...], gn_ref[...]).astype(BF16)
    a = _dot(h, win_ref[...])
    qn = _rms(a[:, :Q_LORA], qan_ref[...]).astype(BF16)
    kvn = _rms(a[:, Q_LORA:Q_LORA + KV_LORA], kvan_ref[...]).astype(BF16)
    kr2 = a[:, Q_LORA + KV_LORA:]
    q_all = _dot(qn, wuq_ref[...])
    kv_all = _dot(kvn, wukv_ref[...])

    ang = pos_ref[...].astype(F32) * invf_ref[...]
    cos = jnp.cos(ang)
    sin = jnp.sin(ang)
    lane = lax.broadcasted_iota(jnp.int32, (1, LANES), 1)
    rope_lane = lane < MLA_ROPE

    q_gn, q_ga, q_gb = qtab_ref[0:1, :], qtab_ref[1:2, :], qtab_ref[2:3, :]
    k_gn, k_ga, k_gb = ktab_ref[0:1, :], ktab_ref[1:2, :], ktab_ref[2:3, :]
    q_cos, q_sin = cos * q_ga, sin * q_gb
    k_rot = kr2 * (cos * k_ga) + pltpu.roll(kr2, MLA_ROPE, 1) * (sin * k_gb)
    kr_ss = jnp.sum(jnp.where(rope_lane, kr2 * kr2, 0.0), axis=-1, keepdims=True)
    scale = MLA_QK ** -0.5

    for hd in range(MLA_HEADS):
        lo = hd * 2 * LANES
        qa = q_all[:, lo:lo + LANES]
        qb = q_all[:, lo + LANES:lo + 2 * LANES]
        ss = jnp.sum(qa * qa + jnp.where(rope_lane, qb * qb, 0.0), axis=-1, keepdims=True)
        r = lax.rsqrt(ss * (1.0 / MLA_QK) + RMS_EPS) * scale
        q_rot = (qb * q_cos + pltpu.roll(qb, MLA_ROPE, 1) * q_sin) * r
        q_out[hd, :, :MLA_NOPE] = (qa * q_gn * r).astype(BF16)
        q_out[hd, :, MLA_NOPE:] = q_rot[:, :MLA_ROPE].astype(BF16)

        kn = kv_all[:, lo:lo + LANES]
        ssk = jnp.sum(kn * kn, axis=-1, keepdims=True) + kr_ss
        rk = lax.rsqrt(ssk * (1.0 / MLA_QK) + RMS_EPS)
        k_out[hd, :, :MLA_NOPE] = (kn * k_gn * rk).astype(BF16)
        k_out[hd, :, MLA_NOPE:] = (k_rot * rk)[:, :MLA_ROPE].astype(BF16)
        v_out[hd] = kv_all[:, lo + LANES:lo + 2 * LANES].astype(BF16)


def _rope_partner():
    d = np.arange(MLA_ROPE)
    half = MLA_ROPE // 2
    partner = np.where(d < half, d + half, d - half)
    sign = np.where(d < half, -1.0, 1.0).astype(np.float32)
    return partner, sign


def _mla_tables(g):
    partner, sign = _rope_partner()
    zeros = jnp.zeros((LANES - MLA_ROPE,), F32)
    g_rope = g[MLA_NOPE:]
    rows = [g[:MLA_NOPE],
            jnp.concatenate([g_rope, zeros]),
            jnp.concatenate([g_rope[partner] * sign, zeros])]
    rows += [jnp.zeros((LANES,), F32)] * 5
    return jnp.stack(rows)


def _mla_prep(x2, pos2, attn_norm, w_in, q_a_norm, kv_a_norm, w_uq, w_ukv, q_norm, k_norm):
    T = x2.shape[0]
    tm = MLA_PREP_TILE
    partner, _ = _rope_partner()
    rope0 = Q_LORA + KV_LORA
    w_in_ext = jnp.concatenate([w_in, w_in[:, rope0 + partner]], axis=1).astype(BF16)
    wq = w_uq.reshape(Q_LORA, MLA_HEADS, MLA_QK)
    w_uq_ext = jnp.concatenate([wq, wq[:, :, MLA_NOPE + partner]], axis=2)
    w_uq_ext = w_uq_ext.reshape(Q_LORA, MLA_HEADS * 2 * LANES).astype(BF16)
    w_ukv_b = w_ukv.astype(BF16)
    inv = 1.0 / (ROPE_THETA ** (jnp.arange(0, MLA_ROPE, 2, dtype=F32) / MLA_ROPE))
    invf = jnp.concatenate([inv, inv, jnp.zeros((LANES - MLA_ROPE,), F32)])[None, :]

    full = lambda shape: pl.BlockSpec(shape, lambda i: (0,) * len(shape))
    out_qk = jax.ShapeDtypeStruct((MLA_HEADS, T, MLA_QK), BF16)
    out_v = jax.ShapeDtypeStruct((MLA_HEADS, T, MLA_V), BF16)
    return pl.pallas_call(
        _mla_prep_kernel,
        grid=(T // tm,),
        in_specs=[
            pl.BlockSpec((tm, D_MODEL), lambda i: (i, 0)),
            pl.BlockSpec((tm, 1), lambda i: (i, 0)),
            full((1, D_MODEL)),
            full((D_MODEL, 4 * LANES)),
            full((1, Q_LORA)),
            full((1, KV_LORA)),
            full((Q_LORA, MLA_HEADS * 2 * LANES)),
            full((KV_LORA, MLA_HEADS * 2 * LANES)),
            full((1, LANES)),
            full((8, LANES)),
            full((8, LANES)),
        ],
        out_specs=[
            pl.BlockSpec((MLA_HEADS, tm, MLA_QK), lambda i: (0, i, 0)),
            pl.BlockSpec((MLA_HEADS, tm, MLA_QK), lambda i: (0, i, 0)),
            pl.BlockSpec((MLA_HEADS, tm, MLA_V), lambda i: (0, i, 0)),
        ],
        out_shape=[out_qk, out_qk, out_v],
        compiler_params=_params("parallel"),
        name="mla_prep",
    )(x2, pos2, attn_norm[None, :], w_in_ext, q_a_norm[None, :], kv_a_norm[None, :],
      w_uq_ext, w_ukv_b, invf, _mla_tables(q_norm), _mla_tables(k_norm))


def _mla_attn_kernel(q_ref, k_ref, v_ref, o_ref, *, seq, tq):
    row_c = lax.broadcasted_iota(jnp.int32, (tq, tq), 0) // CHUNK
    col_c = lax.broadcasted_iota(jnp.int32, (tq, tq), 1) // CHUNK
    diag_mask = col_c <= row_c
    for qi in range(seq // tq):
        lo, hi = qi * tq, (qi + 1) * tq
        for hh in range(MLA_HEADS_PER_STEP):
            q = q_ref[hh, lo:hi, :]
            s_d = jnp.where(diag_mask, _dot_nt(q, k_ref[hh, lo:hi, :]), NEG_INF)
            m = jnp.max(s_d, axis=-1, keepdims=True)
            if qi > 0:
                s_o = _dot_nt(q, k_ref[hh, :lo, :])
                m = jnp.maximum(m, jnp.max(s_o, axis=-1, keepdims=True))
            p_d = jnp.exp(s_d - m)
            l = jnp.sum(p_d, axis=-1, keepdims=True)
            o = _dot(p_d.astype(BF16), v_ref[hh, lo:hi, :])
            if qi > 0:
                p_o = jnp.exp(s_o - m)
                l = l + jnp.sum(p_o, axis=-1, keepdims=True)
                o = o + _dot(p_o.astype(BF16), v_ref[hh, :lo, :])
            o_ref[lo:hi, hh * MLA_V:(hh + 1) * MLA_V] = (o / l).astype(BF16)


def _mla_attn(q, k, v, batch, seq):
    T = batch * seq
    hps = MLA_HEADS_PER_STEP
    kern = functools.partial(_mla_attn_kernel, seq=seq, tq=MLA_Q_TILE)
    return pl.pallas_call(
        kern,
        grid=(MLA_HEADS // hps, batch),
        in_specs=[
            pl.BlockSpec((hps, seq, MLA_QK), lambda h, b: (h, b, 0)),
            pl.BlockSpec((hps, seq, MLA_QK), lambda h, b: (h, b, 0)),
            pl.BlockSpec((hps, seq, MLA_V), lambda h, b: (h, b, 0)),
        ],
        out_specs=pl.BlockSpec((seq, hps * MLA_V), lambda h, b: (b, h)),
        out_shape=jax.ShapeDtypeStruct((T, MLA_HEADS * MLA_V), BF16),
        compiler_params=_params("parallel", "parallel"),
        name="mla_attn",
    )(q, k, v)


def _proj_dense_ffn_kernel(x_ref, o_ref, wo_ref, gn_ref, wg_ref, wu_ref, wd_ref, out_ref):
    x1 = x_ref[...] + _dot(o_ref[...], wo_ref[...])
    h = _rms(x1, gn_ref[...]).astype(BF16)
    a = (_silu(_dot(h, wg_ref[...])) * _dot(h, wu_ref[...])).astype(BF16)
    out_ref[...] = x1 + _dot(a, wd_ref[...])


def _proj_dense_ffn(x2, o2, w_o, ffn_norm, w_gate, w_up, w_down):
    T = x2.shape[0]
    ff = w_gate.shape[1]
    tm = ROW_TILE
    const = lambda shape: pl.BlockSpec(shape, lambda i: (0, 0), pipeline_mode=pl.Buffered(1))
    row = pl.BlockSpec((tm, D_MODEL), lambda i: (i, 0))
    return pl.pallas_call(
        _proj_dense_ffn_kernel,
        grid=(T // tm,),
        in_specs=[row, row, const((D_MODEL, D_MODEL)), const((1, D_MODEL)), const((D_MODEL, ff)),
                  const((D_MODEL, ff)), const((ff, D_MODEL))],
        out_specs=row,
        out_shape=jax.ShapeDtypeStruct((T, D_MODEL), F32),
        compiler_params=_params("parallel"),
        name="proj_dense_ffn",
    )(x2, o2, w_o.astype(BF16), ffn_norm[None, :], w_gate.astype(BF16), w_up.astype(BF16),
      w_down.astype(BF16))


def _ca_prep_kernel(x_ref, gn_ref, w_ref, gq_ref, gk_ref, q_out, k_out, v_out):
    h = _rms(x_ref[...], gn_ref[...]).astype(BF16)
    qkv = _dot(h, w_ref[...])
    q_out[...] = (qkv[:, :D_MODEL] * gq_ref[...]).astype(BF16)
    k_out[...] = (qkv[:, D_MODEL:2 * D_MODEL] * gk_ref[...]).astype(BF16)
    v_out[...] = qkv[:, 2 * D_MODEL:].astype(BF16)


def _ca_prep(x2, attn_norm, w_qkv, q_norm, k_norm):
    T = x2.shape[0]
    tm = ROW_TILE
    full = lambda shape: pl.BlockSpec(shape, lambda i: (0,) * len(shape))
    out = jax.ShapeDtypeStruct((T, D_MODEL), BF16)
    row = pl.BlockSpec((tm, D_MODEL), lambda i: (i, 0))
    return pl.pallas_call(
        _ca_prep_kernel,
        grid=(T // tm,),
        in_specs=[row, full((1, D_MODEL)), full((D_MODEL, 3 * D_MODEL)), full((1, D_MODEL)),
                  full((1, D_MODEL))],
        out_specs=[row, row, row],
        out_shape=[out, out, out],
        compiler_params=_params("parallel"),
        name="ca_prep",
    )(x2, attn_norm[None, :], w_qkv.astype(BF16), jnp.tile(q_norm, CA_HEADS)[None, :],
      jnp.tile(k_norm, CA_HEADS)[None, :])


def _band_attn_kernel(q_ref, k_ref, v_ref, w_ref, o_ref, tab_ref, qn_ref, kn_ref, *, seq, tq):
    @pl.when(pl.program_id(1) == 0)
    def _():
        row = lax.broadcasted_iota(jnp.int32, (tq, BAND_KEYS), 0)
        col = lax.broadcasted_iota(jnp.int32, (tq, BAND_KEYS), 1)
        shift = CHUNK.bit_length() - 1
        for hh in range(2):
            for var in range(3):
                w = w_ref[0, 3 * hh + var:3 * hh + var + 1, :]
                t = pltpu.roll(jnp.broadcast_to(w, (tq, BAND_ROLL)), 0, 1, stride=1, stride_axis=0)
                gap = lax.shift_right_logical(row + var * tq, shift) - lax.shift_right_logical(col, shift)
                tab_ref[hh, var] = jnp.where((gap >= 0) & (gap <= LEFT_CHUNKS), t[:, :BAND_KEYS], NEG_INF)

    lane = lax.broadcasted_iota(jnp.int32, (1, LANES), 1)
    head0 = lane < CA_HEAD_DIM

    def head_normed(x_ref, scale):
        x = x_ref[...].astype(F32)
        sq = x * x
        ss0 = jnp.sum(jnp.where(head0, sq, 0.0), axis=-1, keepdims=True)
        ss1 = jnp.sum(jnp.where(head0, 0.0, sq), axis=-1, keepdims=True)
        r = jnp.where(head0, lax.rsqrt(ss0 * (1.0 / CA_HEAD_DIM) + RMS_EPS),
                      lax.rsqrt(ss1 * (1.0 / CA_HEAD_DIM) + RMS_EPS))
        return (x * (r * scale)).astype(BF16)

    qn_ref[...] = head_normed(q_ref, CA_HEAD_DIM ** -0.5)
    kn_ref[...] = head_normed(k_ref, 1.0)

    for qi in range(seq // tq):
        var = min(qi, 2)
        ks = max(qi - 2, 0) * tq
        q = qn_ref[qi * tq:(qi + 1) * tq, :]
        k = kn_ref[ks:ks + BAND_KEYS, :]
        v = v_ref[ks:ks + BAND_KEYS, :]
        zero = jnp.zeros_like(q)
        qq = jnp.concatenate([jnp.where(head0, q, zero), jnp.where(head0, zero, q)], axis=0)
        s = _dot_nt(qq, k)
        ps, ls = [], []
        for hh in range(2):
            sh = s[hh * tq:(hh + 1) * tq] + tab_ref[hh, var]
            p = jnp.exp(sh - jnp.max(sh, axis=-1, keepdims=True))
            ls.append(jnp.sum(p, axis=-1, keepdims=True))
            ps.append(p.astype(BF16))
        oo = _dot(jnp.concatenate(ps, axis=0), v)
        o = jnp.where(head0, oo[:tq] / ls[0], oo[tq:] / ls[1])
        o_ref[qi * tq:(qi + 1) * tq, :] = o.astype(BF16)


def _band_bias_rows(rel_bias):
    tq = ATTN_Q_TILE
    rb = rel_bias.astype(F32)
    r_max = 3 * tq - 1
    n_far = r_max - MAX_REL
    n_m = r_max + BAND_KEYS
    n_neg = n_m - n_far - rb.shape[1]
    heads = rb.shape[0]
    m = jnp.concatenate([jnp.broadcast_to(rb[:, -1:], (heads, n_far)), rb[:, ::-1],
                         jnp.broadcast_to(rb[:, :1], (heads, n_neg))], axis=1)
    rows = []
    for var in range(3):
        base = r_max - var * tq
        rows.append(jnp.concatenate([m[:, base:base + BAND_KEYS], jnp.zeros((heads, 1), F32),
                                     m[:, base - (tq - 1):base]], axis=1))
    w = jnp.stack(rows, axis=1).reshape(heads // 2, 6, BAND_ROLL)
    return jnp.concatenate([w, jnp.zeros((heads // 2, 2, BAND_ROLL), F32)], axis=1)


def _band_attn(q, k, v, rel_bias, batch, seq):
    T = batch * seq
    tq = ATTN_Q_TILE
    kern = functools.partial(_band_attn_kernel, seq=seq, tq=tq)
    slab = pl.BlockSpec((seq, LANES), lambda hp, b: (b, hp))
    return pl.pallas_call(
        kern,
        grid=(CA_HEADS // 2, batch),
        in_specs=[slab, slab, slab, pl.BlockSpec((1, 8, BAND_ROLL), lambda hp, b: (hp, 0, 0))],
        out_specs=slab,
        out_shape=jax.ShapeDtypeStruct((T, D_MODEL), BF16),
        scratch_shapes=[pltpu.VMEM((2, 3, tq, BAND_KEYS), F32), pltpu.VMEM((seq, LANES), BF16),
                        pltpu.VMEM((seq, LANES), BF16)],
        compiler_params=_params("arbitrary", "arbitrary"),
        name="band_attn",
    )(q, k, v, _band_bias_rows(rel_bias))


def _proj_router_kernel(x_ref, o_ref, wo_ref, gn_ref, rhi_ref, rlo_ref, x1_out, h_out, idx_out, w_out):
    x1 = x_ref[...] + _dot(o_ref[...], wo_ref[...])
    x1_out[...] = x1
    h = _rms(x1, gn_ref[...])
    h_out[...] = h
    h_hi = h.astype(BF16)
    h_lo = (h - h_hi.astype(F32)).astype(BF16)
    logits = _dot(h_hi, rhi_ref[...]) + (_dot(h_lo, rhi_ref[...]) + _dot(h_hi, rlo_ref[...]))

    lane = lax.broadcasted_iota(jnp.int32, logits.shape, 1)
    lane_f = lane.astype(F32)
    lg = jnp.where(lane < N_EXPERTS, logits, NEG_INF)
    m1 = jnp.max(lg, axis=-1, keepdims=True)
    i1 = jnp.min(jnp.where(lg == m1, lane_f, float(LANES)), axis=-1, keepdims=True)
    lg2 = jnp.where(lane_f == i1, NEG_INF, lg)
    m2 = jnp.max(lg2, axis=-1, keepdims=True)
    i2 = jnp.min(jnp.where(lg2 == m2, lane_f, float(LANES)), axis=-1, keepdims=True)
    e = jnp.exp(m2 - m1)
    w1 = 1.0 / (1.0 + e)
    w2 = e / (1.0 + e)
    idx_out[...] = jnp.where(lane == 0, i1, jnp.where(lane == 1, i2, 0.0)).astype(jnp.int32)
    w_out[...] = jnp.where(lane == 0, w1, jnp.where(lane == 1, w2, 0.0))


def _proj_router(x2, o2, w_o, ffn_norm, router):
    T = x2.shape[0]
    tm = ROW_TILE
    r_pad = jnp.zeros((D_MODEL, LANES), F32).at[:, :N_EXPERTS].set(router)
    r_hi = r_pad.astype(BF16)
    r_lo = (r_pad - r_hi.astype(F32)).astype(BF16)
    full = lambda shape: pl.BlockSpec(shape, lambda i: (0,) * len(shape))
    row = pl.BlockSpec((tm, D_MODEL), lambda i: (i, 0))
    small = pl.BlockSpec((tm, LANES), lambda i: (i, 0))
    return pl.pallas_call(
        _proj_router_kernel,
        grid=(T // tm,),
        in_specs=[row, row, full((D_MODEL, D_MODEL)), full((1, D_MODEL)), full((D_MODEL, LANES)),
                  full((D_MODEL, LANES))],
        out_specs=[row, row, small, small],
        out_shape=[jax.ShapeDtypeStruct((T, D_MODEL), F32), jax.ShapeDtypeStruct((T, D_MODEL), F32),
                   jax.ShapeDtypeStruct((T, LANES), jnp.int32), jax.ShapeDtypeStruct((T, LANES), F32)],
        compiler_params=_params("parallel"),
        name="proj_router",
    )(x2, o2, w_o.astype(BF16), ffn_norm[None, :], r_hi, r_lo)


def _moe_plan(top_idx, tm):
    T = top_idx.shape[0]
    n_assign = T * TOP_K
    n_tiles = n_assign // tm + N_EXPERTS
    e_flat = top_idx.reshape(1, n_assign)
    onehot = (e_flat == jnp.arange(N_EXPERTS, dtype=jnp.int32)[:, None]).astype(jnp.int32)
    csum = jnp.cumsum(onehot, axis=1)
    counts = csum[:, -1]
    tiles_per = (counts + tm - 1) // tm
    tile_end = jnp.cumsum(tiles_per)
    tile_start = tile_end - tiles_per
    dest = jnp.sum(onehot * (csum - 1 + tile_start[:, None] * tm), axis=0)

    tile_id = jnp.arange(n_tiles, dtype=jnp.int32)
    tile_valid = (tile_id < tile_end[-1]).astype(jnp.int32)
    tile_expert = jnp.minimum(jnp.sum((tile_id[:, None] >= tile_end[None, :]).astype(jnp.int32), axis=1),
                              N_EXPERTS - 1)
    pad = jnp.stack([tile_start * tm + counts, tiles_per * tm - counts], axis=1).reshape(-1)
    pad = jnp.concatenate([pad, tile_end[-1:]])
    return dest.astype(jnp.int32), pad.astype(jnp.int32), tile_expert, tile_valid


def _dispatch_kernel(dest_ref, pad_ref, h_ref, xs_hbm, zbuf, sem, zsem, *, tm, n_tiles):
    i = pl.program_id(0)

    @pl.when(i == 0)
    def _():
        zbuf[...] = jnp.zeros_like(zbuf)
        for t in range(n_tiles - N_EXPERTS, n_tiles):
            @pl.when(t >= pad_ref[2 * N_EXPERTS])
            def _(t=t):
                for part in range(tm // ZERO_ROWS):
                    cp = pltpu.make_async_copy(
                        zbuf, xs_hbm.at[pl.ds(t * tm + part * ZERO_ROWS, ZERO_ROWS), :], zsem)
                    cp.start()
                    cp.wait()
        for e in range(N_EXPERTS):
            start = pad_ref[2 * e]
            length = pad_ref[2 * e + 1]
            end = start + length
            for r in range(SUBLANES - 1):
                @pl.when(r < (length & (SUBLANES - 1)))
                def _(r=r):
                    cp = pltpu.make_async_copy(zbuf.at[pl.ds(0, 1), :], xs_hbm.at[pl.ds(start + r, 1), :], zsem)
                    cp.start()
                    cp.wait()
            n = ZERO_ROWS
            while n >= SUBLANES:
                offset = pl.multiple_of(end - (length & ~(2 * n - 1)) - n, SUBLANES)

                @pl.when((length & n) != 0)
                def _(n=n, offset=offset):
                    cp = pltpu.make_async_copy(zbuf.at[pl.ds(0, n), :], xs_hbm.at[pl.ds(offset, n), :], zsem)
                    cp.start()
                    cp.wait()
                n //= 2

    def issue(j, carry):
        row0 = pl.multiple_of(j * DMA_GROUP, DMA_GROUP)
        a0 = (i * tm + row0) * TOP_K
        for u in range(DMA_GROUP):
            src = h_ref.at[pl.ds(row0 + u, 1), :]
            for kk in range(TOP_K):
                pltpu.make_async_copy(src, xs_hbm.at[pl.ds(dest_ref[a0 + TOP_K * u + kk], 1), :], sem).start()
        return carry
    lax.fori_loop(0, tm // DMA_GROUP, issue, 0)
    for _ in range(TOP_K):
        pltpu.make_async_copy(h_ref, xs_hbm.at[pl.ds(0, tm), :], sem).wait()


def _dispatch(h2, dest, pad, n_rows, tm):
    T = h2.shape[0]
    assert tm == 2 * ZERO_ROWS
    kern = functools.partial(_dispatch_kernel, tm=tm, n_tiles=n_rows // tm)
    return pl.pallas_call(
        kern,
        grid_spec=pltpu.PrefetchScalarGridSpec(
            num_scalar_prefetch=2,
            grid=(T // tm,),
            in_specs=[pl.BlockSpec((tm, D_MODEL), lambda i, dest, pad: (i, 0))],
            out_specs=pl.BlockSpec(memory_space=pl.ANY),
            scratch_shapes=[pltpu.VMEM((ZERO_ROWS, D_MODEL), F32), pltpu.SemaphoreType.DMA,
                            pltpu.SemaphoreType.DMA],
        ),
        out_shape=jax.ShapeDtypeStruct((n_rows, D_MODEL), F32),
        compiler_params=_params("arbitrary"),
        name="moe_dispatch",
    )(dest, pad, h2)


def _moe_ffn_kernel(te_ref, tv_ref, xs_ref, wg_ref, wu_ref, wd_ref, ys_ref, hs_ref):
    i = pl.program_id(0)

    @pl.when(pl.program_id(1) == 0)
    def _():
        ys_ref[...] = jnp.zeros_like(ys_ref)
        hs_ref[...] = xs_ref[...].astype(BF16)

    @pl.when(tv_ref[i] > 0)
    def _():
        h = hs_ref[...]
        a = (_silu(_dot(h, wg_ref[0])) * _dot(h, wu_ref[0])).astype(BF16)
        ys_ref[...] += _dot(a, wd_ref[0])


def _moe_ffn(xs, tile_expert, tile_valid, w_gate, w_up, w_down, tm):
    n_rows = xs.shape[0]
    ff = w_gate.shape[2]
    tf = MOE_FF_TILE
    nf = ff // tf
    ff_blk = lambda i, f, tv: f * tv[i] + (nf - 1) * (1 - tv[i])
    return pl.pallas_call(
        _moe_ffn_kernel,
        grid_spec=pltpu.PrefetchScalarGridSpec(
            num_scalar_prefetch=2,
            grid=(n_rows // tm, nf),
            in_specs=[
                pl.BlockSpec((tm, D_MODEL), lambda i, f, te, tv: (i * tv[i], 0)),
                pl.BlockSpec((1, D_MODEL, tf), lambda i, f, te, tv: (te[i], 0, ff_blk(i, f, tv))),
                pl.BlockSpec((1, D_MODEL, tf), lambda i, f, te, tv: (te[i], 0, ff_blk(i, f, tv))),
                pl.BlockSpec((1, tf, D_MODEL), lambda i, f, te, tv: (te[i], ff_blk(i, f, tv), 0)),
            ],
            out_specs=pl.BlockSpec((tm, D_MODEL), lambda i, f, te, tv: (i, 0)),
            scratch_shapes=[pltpu.VMEM((tm, D_MODEL), BF16)],
        ),
        out_shape=jax.ShapeDtypeStruct((n_rows, D_MODEL), F32),
        compiler_params=_params("arbitrary", "arbitrary"),
        name="moe_ffn",
    )(tile_expert, tile_valid, xs, w_gate.astype(BF16), w_up.astype(BF16), w_down.astype(BF16))


def _combine_kernel(pos_ref, x1_ref, wts_ref, ys_hbm, out_ref, buf0, buf1, sem0, sem1, *, tm):
    i = pl.program_id(0)

    def issue(j, carry):
        row0 = pl.multiple_of(j * DMA_GROUP, DMA_GROUP)
        a0 = (i * tm + row0) * TOP_K
        for u in range(DMA_GROUP):
            for buf, sem, kk in ((buf0, sem0, 0), (buf1, sem1, 1)):
                pltpu.make_async_copy(ys_hbm.at[pl.ds(pos_ref[a0 + TOP_K * u + kk], 1), :],
                                      buf.at[pl.ds(row0 + u, 1), :], sem).start()
        return carry
    lax.fori_loop(0, tm // DMA_GROUP, issue, 0)
    pltpu.make_async_copy(ys_hbm.at[pl.ds(0, tm), :], buf0, sem0).wait()
    pltpu.make_async_copy(ys_hbm.at[pl.ds(0, tm), :], buf1, sem1).wait()
    wts = wts_ref[...]
    out_ref[...] = x1_ref[...] + (wts[:, 0:1] * buf0[...] + wts[:, 1:2] * buf1[...])


def _combine(x1, wts, ys, dest, tm):
    T = x1.shape[0]
    kern = functools.partial(_combine_kernel, tm=tm)
    return pl.pallas_call(
        kern,
        grid_spec=pltpu.PrefetchScalarGridSpec(
            num_scalar_prefetch=1,
            grid=(T // tm,),
            in_specs=[pl.BlockSpec((tm, D_MODEL), lambda i, pos: (i, 0)),
                      pl.BlockSpec((tm, LANES), lambda i, pos: (i, 0)),
                      pl.BlockSpec(memory_space=pl.ANY)],
            out_specs=pl.BlockSpec((tm, D_MODEL), lambda i, pos: (i, 0)),
            scratch_shapes=[pltpu.VMEM((tm, D_MODEL), F32), pltpu.VMEM((tm, D_MODEL), F32),
                            pltpu.SemaphoreType.DMA, pltpu.SemaphoreType.DMA],
        ),
        out_shape=jax.ShapeDtypeStruct((T, D_MODEL), F32),
        compiler_params=_params("arbitrary"),
        name="moe_combine",
    )(dest, x1, wts, ys)


def _mla_dense_layer(x2, pos2, batch, seq, attn_norm, w_in, q_a_norm, kv_a_norm, w_uq, w_ukv, q_norm,
                     k_norm, w_o, ffn_norm, w_gate, w_up, w_down):
    q, k, v = _mla_prep(x2, pos2, attn_norm, w_in, q_a_norm, kv_a_norm, w_uq, w_ukv, q_norm, k_norm)
    o = _mla_attn(q, k, v, batch, seq)
    return _proj_dense_ffn(x2, o, w_o, ffn_norm, w_gate, w_up, w_down)


def _band_moe_layer(x2, batch, seq, attn_norm, w_qkv, q_norm, k_norm, rel_bias, w_o, ffn_norm, router,
                    w_gate, w_up, w_down):
    q, k, v = _ca_prep(x2, attn_norm, w_qkv, q_norm, k_norm)
    o = _band_attn(q, k, v, rel_bias, batch, seq)
    x1, h, idx, wts = _proj_router(x2, o, w_o, ffn_norm, router)
    tm = ROW_TILE
    dest, pad, tile_expert, tile_valid = _moe_plan(idx[:, :TOP_K], tm)
    xs = _dispatch(h, dest, pad, tile_expert.shape[0] * tm, tm)
    ys = _moe_ffn(xs, tile_expert, tile_valid, w_gate, w_up, w_down, tm)
    return _combine(x1, wts, ys, dest, tm)


def kernel(x, positions, l0_attn_norm, l0_mla_w_in, l0_mla_q_a_norm, l0_mla_kv_a_norm, l0_mla_w_uq, l0_mla_w_ukv, l0_mla_q_norm, l0_mla_k_norm, l0_mla_w_o, l0_ffn_norm, l0_ffn_w_gate, l0_ffn_w_up, l0_ffn_w_down, l1_attn_norm, l1_ca_w_qkv, l1_ca_q_norm, l1_ca_k_norm, l1_ca_rel_bias, l1_ca_w_o, l1_ffn_norm, l1_moe_router, l1_moe_w_gate, l1_moe_w_up, l1_moe_w_down, l2_attn_norm, l2_mla_w_in, l2_mla_q_a_norm, l2_mla_kv_a_norm, l2_mla_w_uq, l2_mla_w_ukv, l2_mla_q_norm, l2_mla_k_norm, l2_mla_w_o, l2_ffn_norm, l2_ffn_w_gate, l2_ffn_w_up, l2_ffn_w_down, l3_attn_norm, l3_ca_w_qkv, l3_ca_q_norm, l3_ca_k_norm, l3_ca_rel_bias, l3_ca_w_o, l3_ffn_norm, l3_moe_router, l3_moe_w_gate, l3_moe_w_up, l3_moe_w_down):
    batch, seq, d = x.shape
    assert d == D_MODEL and seq % ATTN_Q_TILE == 0 and seq >= BAND_KEYS
    assert (batch * seq) % ROW_TILE == 0
    x2 = x.reshape(batch * seq, d)
    pos2 = positions.reshape(batch * seq, 1)
    x2 = _mla_dense_layer(x2, pos2, batch, seq, l0_attn_norm, l0_mla_w_in, l0_mla_q_a_norm, l0_mla_kv_a_norm,
                          l0_mla_w_uq, l0_mla_w_ukv, l0_mla_q_norm, l0_mla_k_norm, l0_mla_w_o, l0_ffn_norm,
                          l0_ffn_w_gate, l0_ffn_w_up, l0_ffn_w_down)
    x2 = _band_moe_layer(x2, batch, seq, l1_attn_norm, l1_ca_w_qkv, l1_ca_q_norm, l1_ca_k_norm, l1_ca_rel_bias,
                         l1_ca_w_o, l1_ffn_norm, l1_moe_router, l1_moe_w_gate, l1_moe_w_up, l1_moe_w_down)
    x2 = _mla_dense_layer(x2, pos2, batch, seq, l2_attn_norm, l2_mla_w_in, l2_mla_q_a_norm, l2_mla_kv_a_norm,
                          l2_mla_w_uq, l2_mla_w_ukv, l2_mla_q_norm, l2_mla_k_norm, l2_mla_w_o, l2_ffn_norm,
                          l2_ffn_w_gate, l2_ffn_w_up, l2_ffn_w_down)
    x2 = _band_moe_layer(x2, batch, seq, l3_attn_norm, l3_ca_w_qkv, l3_ca_q_norm, l3_ca_k_norm, l3_ca_rel_bias,
                         l3_ca_w_o, l3_ffn_norm, l3_moe_router, l3_moe_w_gate, l3_moe_w_up, l3_moe_w_down)
    return x2.reshape(batch, seq, d)
```

```python
import functools

import numpy as np
import jax
import jax.numpy as jnp
from jax import lax
from jax.experimental import pallas as pl
from jax.experimental.pallas import tpu as pltpu

F32 = jnp.float32
BF16 = jnp.bfloat16

D_MODEL = 1024
CHUNK = 64
RMS_EPS = 1e-6

MLA_HEADS = 8
MLA_NOPE = 128
MLA_ROPE = 64
MLA_V = 128
MLA_QK = MLA_NOPE + MLA_ROPE
Q_LORA = 256
KV_LORA = 128
ROPE_THETA = 10000.0

CA_HEADS = 16
CA_HEAD_DIM = 64
LEFT_CHUNKS = 8
MAX_REL = 256

N_EXPERTS = 8
TOP_K = 2

LANES = 128
SUBLANES = 8
VMEM_LIMIT = 56 * 1024 * 1024

ROW_TILE = 512
MLA_PREP_TILE = 1024
MLA_Q_TILE = 512
MLA_HEADS_PER_STEP = 2
ATTN_Q_TILE = 256
BAND_KEYS = 3 * ATTN_Q_TILE
BAND_ROLL = ATTN_Q_TILE + BAND_KEYS
MOE_FF_TILE = 1792
DMA_GROUP = 8
ZERO_ROWS = 256
NEG_INF = float("-inf")


def _dot(a, b):
    return jnp.dot(a, b, preferred_element_type=F32)


def _dot_nt(a, b):
    return lax.dot_general(a, b, (((1,), (1,)), ((), ())), preferred_element_type=F32)


def _rms(x, g):
    return x * lax.rsqrt(jnp.mean(x * x, axis=-1, keepdims=True) + RMS_EPS) * g


def _silu(g):
    return g / (1.0 + jnp.exp(-g))


def _params(*sem):
    return pltpu.CompilerParams(dimension_semantics=sem, vmem_limit_bytes=VMEM_LIMIT)


def _mla_prep_kernel(x_ref, pos_ref, gn_ref, win_ref, qan_ref, kvan_ref, wuq_ref, wukv_ref,
                     invf_ref, qtab_ref, ktab_ref, q_out, k_out, v_out):
    h = _rms(x_ref[...], gn_ref[...]).astype(BF16)
    a = _dot(h, win_ref[...])
    qn = _rms(a[:, :Q_LORA], qan_ref[...]).astype(BF16)
    kvn = _rms(a[:, Q_LORA:Q_LORA + KV_LORA], kvan_ref[...]).astype(BF16)
    kr2 = a[:, Q_LORA + KV_LORA:]
    q_all = _dot(qn, wuq_ref[...])
    kv_all = _dot(kvn, wukv_ref[...])

    ang = pos_ref[...].astype(F32) * invf_ref[...]
    cos = jnp.cos(ang)
    sin = jnp.sin(ang)
    lane = lax.broadcasted_iota(jnp.int32, (1, LANES), 1)
    rope_lane = lane < MLA_ROPE

    q_gn, q_ga, q_gb = qtab_ref[0:1, :], qtab_ref[1:2, :], qtab_ref[2:3, :]
    k_gn, k_ga, k_gb = ktab_ref[0:1, :], ktab_ref[1:2, :], ktab_ref[2:3, :]
    q_cos, q_sin = cos * q_ga, sin * q_gb
    k_rot = kr2 * (cos * k_ga) + pltpu.roll(kr2, MLA_ROPE, 1) * (sin * k_gb)
    kr_ss = jnp.sum(jnp.where(rope_lane, kr2 * kr2, 0.0), axis=-1, keepdims=True)
    scale = MLA_QK ** -0.5

    for hd in range(MLA_HEADS):
        lo = hd * 2 * LANES
        qa = q_all[:, lo:lo + LANES]
        qb = q_all[:, lo + LANES:lo + 2 * LANES]
        ss = jnp.sum(qa * qa + jnp.where(rope_lane, qb * qb, 0.0), axis=-1, keepdims=True)
        r = lax.rsqrt(ss * (1.0 / MLA_QK) + RMS_EPS) * scale
        q_rot = (qb * q_cos + pltpu.roll(qb, MLA_ROPE, 1) * q_sin) * r
        q_out[hd, :, :MLA_NOPE] = (qa * q_gn * r).astype(BF16)
        q_out[hd, :, MLA_NOPE:] = q_rot[:, :MLA_ROPE].astype(BF16)

        kn = kv_all[:, lo:lo + LANES]
        ssk = jnp.sum(kn * kn, axis=-1, keepdims=True) + kr_ss
        rk = lax.rsqrt(ssk * (1.0 / MLA_QK) + RMS_EPS)
        k_out[hd, :, :MLA_NOPE] = (kn * k_gn * rk).astype(BF16)
        k_out[hd, :, MLA_NOPE:] = (k_rot * rk)[:, :MLA_ROPE].astype(BF16)
        v_out[hd] = kv_all[:, lo + LANES:lo + 2 * LANES].astype(BF16)


def _rope_partner():
    d = np.arange(MLA_ROPE)
    half = MLA_ROPE // 2
    partner = np.where(d < half, d + half, d - half)
    sign = np.where(d < half, -1.0, 1.0).astype(np.float32)
    return partner, sign


def _mla_tables(g):
    partner, sign = _rope_partner()
    zeros = jnp.zeros((LANES - MLA_ROPE,), F32)
    g_rope = g[MLA_NOPE:]
    rows = [g[:MLA_NOPE],
            jnp.concatenate([g_rope, zeros]),
            jnp.concatenate([g_rope[partner] * sign, zeros])]
    rows += [jnp.zeros((LANES,), F32)] * 5
    return jnp.stack(rows)


def _mla_prep(x2, pos2, attn_norm, w_in, q_a_norm, kv_a_norm, w_uq, w_ukv, q_norm, k_norm):
    T = x2.shape[0]
    tm = MLA_PREP_TILE
    partner, _ = _rope_partner()
    rope0 = Q_LORA + KV_LORA
    w_in_ext = jnp.concatenate([w_in, w_in[:, rope0 + partner]], axis=1).astype(BF16)
    wq = w_uq.reshape(Q_LORA, MLA_HEADS, MLA_QK)
    w_uq_ext = jnp.concatenate([wq, wq[:, :, MLA_NOPE + partner]], axis=2)
    w_uq_ext = w_uq_ext.reshape(Q_LORA, MLA_HEADS * 2 * LANES).astype(BF16)
    w_ukv_b = w_ukv.astype(BF16)
    inv = 1.0 / (ROPE_THETA ** (jnp.arange(0, MLA_ROPE, 2, dtype=F32) / MLA_ROPE))
    invf = jnp.concatenate([inv, inv, jnp.zeros((LANES - MLA_ROPE,), F32)])[None, :]

    full = lambda shape: pl.BlockSpec(shape, lambda i: (0,) * len(shape))
    out_qk = jax.ShapeDtypeStruct((MLA_HEADS, T, MLA_QK), BF16)
    out_v = jax.ShapeDtypeStruct((MLA_HEADS, T, MLA_V), BF16)
    return pl.pallas_call(
        _mla_prep_kernel,
        grid=(T // tm,),
        in_specs=[
            pl.BlockSpec((tm, D_MODEL), lambda i: (i, 0)),
            pl.BlockSpec((tm, 1), lambda i: (i, 0)),
            full((1, D_MODEL)),
            full((D_MODEL, 4 * LANES)),
            full((1, Q_LORA)),
            full((1, KV_LORA)),
            full((Q_LORA, MLA_HEADS * 2 * LANES)),
            full((KV_LORA, MLA_HEADS * 2 * LANES)),
            full((1, LANES)),
            full((8, LANES)),
            full((8, LANES)),
        ],
        out_specs=[
            pl.BlockSpec((MLA_HEADS, tm, MLA_QK), lambda i: (0, i, 0)),
            pl.BlockSpec((MLA_HEADS, tm, MLA_QK), lambda i: (0, i, 0)),
            pl.BlockSpec((MLA_HEADS, tm, MLA_V), lambda i: (0, i, 0)),
        ],
        out_shape=[out_qk, out_qk, out_v],
        compiler_params=_params("parallel"),
        name="mla_prep",
    )(x2, pos2, attn_norm[None, :], w_in_ext, q_a_norm[None, :], kv_a_norm[None, :],
      w_uq_ext, w_ukv_b, invf, _mla_tables(q_norm), _mla_tables(k_norm))


def _mla_attn_kernel(q_ref, k_ref, v_ref, o_ref, *, seq, tq):
    row_c = lax.broadcasted_iota(jnp.int32, (tq, tq), 0) // CHUNK
    col_c = lax.broadcasted_iota(jnp.int32, (tq, tq), 1) // CHUNK
    diag_mask = col_c <= row_c
    for qi in range(seq // tq):
        lo, hi = qi * tq, (qi + 1) * tq
        for hh in range(MLA_HEADS_PER_STEP):
            q = q_ref[hh, lo:hi, :]
            s_d = jnp.where(diag_mask, _dot_nt(q, k_ref[hh, lo:hi, :]), NEG_INF)
            m = jnp.max(s_d, axis=-1, keepdims=True)
            if qi > 0:
                s_o = _dot_nt(q, k_ref[hh, :lo, :])
                m = jnp.maximum(m, jnp.max(s_o, axis=-1, keepdims=True))
            p_d = jnp.exp(s_d - m)
            l = jnp.sum(p_d, axis=-1, keepdims=True)
            o = _dot(p_d.astype(BF16), v_ref[hh, lo:hi, :])
            if qi > 0:
                p_o = jnp.exp(s_o - m)
                l = l + jnp.sum(p_o, axis=-1, keepdims=True)
                o = o + _dot(p_o.astype(BF16), v_ref[hh, :lo, :])
            o_ref[lo:hi, hh * MLA_V:(hh + 1) * MLA_V] = (o / l).astype(BF16)


def _mla_attn(q, k, v, batch, seq):
    T = batch * seq
    hps = MLA_HEADS_PER_STEP
    kern = functools.partial(_mla_attn_kernel, seq=seq, tq=MLA_Q_TILE)
    return pl.pallas_call(
        kern,
        grid=(MLA_HEADS // hps, batch),
        in_specs=[
            pl.BlockSpec((hps, seq, MLA_QK), lambda h, b: (h, b, 0)),
            pl.BlockSpec((hps, seq, MLA_QK), lambda h, b: (h, b, 0)),
            pl.BlockSpec((hps, seq, MLA_V), lambda h, b: (h, b, 0)),
        ],
        out_specs=pl.BlockSpec((seq, hps * MLA_V), lambda h, b: (b, h)),
        out_shape=jax.ShapeDtypeStruct((T, MLA_HEADS * MLA_V), BF16),
        compiler_params=_params("parallel", "parallel"),
        name="mla_attn",
    )(q, k, v)


def _proj_dense_ffn_kernel(x_ref, o_ref, wo_ref, gn_ref, wg_ref, wu_ref, wd_ref, out_ref):
    x1 = x_ref[...] + _dot(o_ref[...], wo_ref[...])
    h = _rms(x1, gn_ref[...]).astype(BF16)
    a = (_silu(_dot(h, wg_ref[...])) * _dot(h, wu_ref[...])).astype(BF16)
    out_ref[...] = x1 + _dot(a, wd_ref[...])


def _proj_dense_ffn(x2, o2, w_o, ffn_norm, w_gate, w_up, w_down):
    T = x2.shape[0]
    ff = w_gate.shape[1]
    tm = ROW_TILE
    const = lambda shape: pl.BlockSpec(shape, lambda i: (0, 0), pipeline_mode=pl.Buffered(1))
    row = pl.BlockSpec((tm, D_MODEL), lambda i: (i, 0))
    return pl.pallas_call(
        _proj_dense_ffn_kernel,
        grid=(T // tm,),
        in_specs=[row, row, const((D_MODEL, D_MODEL)), const((1, D_MODEL)), const((D_MODEL, ff)),
                  const((D_MODEL, ff)), const((ff, D_MODEL))],
        out_specs=row,
        out_shape=jax.ShapeDtypeStruct((T, D_MODEL), F32),
        compiler_params=_params("parallel"),
        name="proj_dense_ffn",
    )(x2, o2, w_o.astype(BF16), ffn_norm[None, :], w_gate.astype(BF16), w_up.astype(BF16),
      w_down.astype(BF16))


def _ca_prep_kernel(x_ref, gn_ref, w_ref, q_out, k_out, v_out):
    h = _rms(x_ref[...], gn_ref[...]).astype(BF16)
    qkv = _dot(h, w_ref[...])
    q_out[...] = qkv[:, :D_MODEL].astype(BF16)
    k_out[...] = qkv[:, D_MODEL:2 * D_MODEL].astype(BF16)
    v_out[...] = qkv[:, 2 * D_MODEL:].astype(BF16)


def _ca_prep(x2, attn_norm, w_qkv):
    T = x2.shape[0]
    tm = ROW_TILE
    full = lambda shape: pl.BlockSpec(shape, lambda i: (0,) * len(shape))
    out = jax.ShapeDtypeStruct((T, D_MODEL), BF16)
    row = pl.BlockSpec((tm, D_MODEL), lambda i: (i, 0))
    return pl.pallas_call(
        _ca_prep_kernel,
        grid=(T // tm,),
        in_specs=[row, full((1, D_MODEL)), full((D_MODEL, 3 * D_MODEL))],
        out_specs=[row, row, row],
        out_shape=[out, out, out],
        compiler_params=_params("parallel"),
        name="ca_prep",
    )(x2, attn_norm[None, :], w_qkv.astype(BF16))


def _band_attn_kernel(q_ref, k_ref, v_ref, w_ref, g_ref, o_ref, tab_ref, qn_ref, kn_ref, *, seq, tq):
    @pl.when(pl.program_id(1) == 0)
    def _():
        row = lax.broadcasted_iota(jnp.int32, (tq, BAND_KEYS), 0)
        col = lax.broadcasted_iota(jnp.int32, (tq, BAND_KEYS), 1)
        shift = CHUNK.bit_length() - 1
        for hh in range(2):
            for var in range(3):
                w = w_ref[0, 3 * hh + var:3 * hh + var + 1, :]
                t = pltpu.roll(jnp.broadcast_to(w, (tq, BAND_ROLL)), 0, 1, stride=1, stride_axis=0)
                gap = lax.shift_right_logical(row + var * tq, shift) - lax.shift_right_logical(col, shift)
                tab_ref[hh, var] = jnp.where((gap >= 0) & (gap <= LEFT_CHUNKS), t[:, :BAND_KEYS], NEG_INF)

    lane = lax.broadcasted_iota(jnp.int32, (1, LANES), 1)
    head0 = lane < CA_HEAD_DIM

    def head_normed(x_ref, gain):
        x = x_ref[...].astype(F32)
        sq = x * x
        ss0 = jnp.sum(jnp.where(head0, sq, 0.0), axis=-1, keepdims=True)
        ss1 = jnp.sum(jnp.where(head0, 0.0, sq), axis=-1, keepdims=True)
        r = jnp.where(head0, lax.rsqrt(ss0 * (1.0 / CA_HEAD_DIM) + RMS_EPS),
                      lax.rsqrt(ss1 * (1.0 / CA_HEAD_DIM) + RMS_EPS))
        return (x * r * gain).astype(BF16)

    qn_ref[...] = head_normed(q_ref, g_ref[0:1, :])
    kn_ref[...] = head_normed(k_ref, g_ref[1:2, :])

    for qi in range(seq // tq):
        var = min(qi, 2)
        ks = max(qi - 2, 0) * tq
        q = qn_ref[qi * tq:(qi + 1) * tq, :]
        k = kn_ref[ks:ks + BAND_KEYS, :]
        v = v_ref[ks:ks + BAND_KEYS, :]
        zero = jnp.zeros_like(q)
        qq = jnp.concatenate([jnp.where(head0, q, zero), jnp.where(head0, zero, q)], axis=0)
        s = _dot_nt(qq, k)
        ps, ls = [], []
        for hh in range(2):
            sh = s[hh * tq:(hh + 1) * tq] + tab_ref[hh, var]
            p = jnp.exp(sh - jnp.max(sh, axis=-1, keepdims=True))
            ls.append(jnp.sum(p, axis=-1, keepdims=True))
            ps.append(p.astype(BF16))
        oo = _dot(jnp.concatenate(ps, axis=0), v)
        o = jnp.where(head0, oo[:tq] / ls[0], oo[tq:] / ls[1])
        o_ref[qi * tq:(qi + 1) * tq, :] = o.astype(BF16)


def _band_bias_rows(rel_bias):
    tq = ATTN_Q_TILE
    rb = rel_bias.astype(F32)
    r_max = 3 * tq - 1
    n_far = r_max - MAX_REL
    n_m = r_max + BAND_KEYS
    n_neg = n_m - n_far - rb.shape[1]
    heads = rb.shape[0]
    m = jnp.concatenate([jnp.broadcast_to(rb[:, -1:], (heads, n_far)), rb[:, ::-1],
                         jnp.broadcast_to(rb[:, :1], (heads, n_neg))], axis=1)
    rows = []
    for var in range(3):
        base = r_max - var * tq
        rows.append(jnp.concatenate([m[:, base:base + BAND_KEYS], jnp.zeros((heads, 1), F32),
                                     m[:, base - (tq - 1):base]], axis=1))
    w = jnp.stack(rows, axis=1).reshape(heads // 2, 6, BAND_ROLL)
    return jnp.concatenate([w, jnp.zeros((heads // 2, 2, BAND_ROLL), F32)], axis=1)


def _band_attn(q, k, v, q_norm, k_norm, rel_bias, batch, seq):
    T = batch * seq
    tq = ATTN_Q_TILE
    gains = jnp.stack([jnp.tile(q_norm, 2) * CA_HEAD_DIM ** -0.5, jnp.tile(k_norm, 2)]
                      + [jnp.zeros((LANES,), F32)] * 6)
    kern = functools.partial(_band_attn_kernel, seq=seq, tq=tq)
    slab = pl.BlockSpec((seq, LANES), lambda hp, b: (b, hp))
    return pl.pallas_call(
        kern,
        grid=(CA_HEADS // 2, batch),
        in_specs=[slab, slab, slab, pl.BlockSpec((1, 8, BAND_ROLL), lambda hp, b: (hp, 0, 0)),
                  pl.BlockSpec((8, LANES), lambda hp, b: (0, 0))],
        out_specs=slab,
        out_shape=jax.ShapeDtypeStruct((T, D_MODEL), BF16),
        scratch_shapes=[pltpu.VMEM((2, 3, tq, BAND_KEYS), F32), pltpu.VMEM((seq, LANES), BF16),
                        pltpu.VMEM((seq, LANES), BF16)],
        compiler_params=_params("arbitrary", "arbitrary"),
        name="band_attn",
    )(q, k, v, _band_bias_rows(rel_bias), gains)


def _proj_router_kernel(x_ref, o_ref, wo_ref, gn_ref, rhi_ref, rlo_ref, x1_out, h_out, idx_out, w_out):
    x1 = x_ref[...] + _dot(o_ref[...], wo_ref[...])
    x1_out[...] = x1
    h = _rms(x1, gn_ref[...])
    h_out[...] = h
    h_hi = h.astype(BF16)
    h_lo = (h - h_hi.astype(F32)).astype(BF16)
    logits = _dot(h_hi, rhi_ref[...]) + (_dot(h_lo, rhi_ref[...]) + _dot(h_hi, rlo_ref[...]))

    lane = lax.broadcasted_iota(jnp.int32, logits.shape, 1)
    lane_f = lane.astype(F32)
    lg = jnp.where(lane < N_EXPERTS, logits, NEG_INF)
    m1 = jnp.max(lg, axis=-1, keepdims=True)
    i1 = jnp.min(jnp.where(lg == m1, lane_f, float(LANES)), axis=-1, keepdims=True)
    lg2 = jnp.where(lane_f == i1, NEG_INF, lg)
    m2 = jnp.max(lg2, axis=-1, keepdims=True)
    i2 = jnp.min(jnp.where(lg2 == m2, lane_f, float(LANES)), axis=-1, keepdims=True)
    e = jnp.exp(m2 - m1)
    w1 = 1.0 / (1.0 + e)
    w2 = e / (1.0 + e)
    idx_out[...] = jnp.where(lane == 0, i1, jnp.where(lane == 1, i2, 0.0)).astype(jnp.int32)
    w_out[...] = jnp.where(lane == 0, w1, jnp.where(lane == 1, w2, 0.0))


def _proj_router(x2, o2, w_o, ffn_norm, router):
    T = x2.shape[0]
    tm = ROW_TILE
    r_pad = jnp.zeros((D_MODEL, LANES), F32).at[:, :N_EXPERTS].set(router)
    r_hi = r_pad.astype(BF16)
    r_lo = (r_pad - r_hi.astype(F32)).astype(BF16)
    full = lambda shape: pl.BlockSpec(shape, lambda i: (0,) * len(shape))
    row = pl.BlockSpec((tm, D_MODEL), lambda i: (i, 0))
    small = pl.BlockSpec((tm, LANES), lambda i: (i, 0))
    return pl.pallas_call(
        _proj_router_kernel,
        grid=(T // tm,),
        in_specs=[row, row, full((D_MODEL, D_MODEL)), full((1, D_MODEL)), full((D_MODEL, LANES)),
                  full((D_MODEL, LANES))],
        out_specs=[row, row, small, small],
        out_shape=[jax.ShapeDtypeStruct((T, D_MODEL), F32), jax.ShapeDtypeStruct((T, D_MODEL), F32),
                   jax.ShapeDtypeStruct((T, LANES), jnp.int32), jax.ShapeDtypeStruct((T, LANES), F32)],
        compiler_params=_params("parallel"),
        name="proj_router",
    )(x2, o2, w_o.astype(BF16), ffn_norm[None, :], r_hi, r_lo)


def _moe_plan(top_idx, tm):
    T = top_idx.shape[0]
    n_assign = T * TOP_K
    n_tiles = n_assign // tm + N_EXPERTS
    e_flat = top_idx.reshape(1, n_assign)
    onehot = (e_flat == jnp.arange(N_EXPERTS, dtype=jnp.int32)[:, None]).astype(jnp.int32)
    csum = jnp.cumsum(onehot, axis=1)
    counts = csum[:, -1]
    tiles_per = (counts + tm - 1) // tm
    tile_end = jnp.cumsum(tiles_per)
    tile_start = tile_end - tiles_per
    dest = jnp.sum(onehot * (csum - 1 + tile_start[:, None] * tm), axis=0)

    tile_id = jnp.arange(n_tiles, dtype=jnp.int32)
    tile_valid = (tile_id < tile_end[-1]).astype(jnp.int32)
    tile_expert = jnp.minimum(jnp.sum((tile_id[:, None] >= tile_end[None, :]).astype(jnp.int32), axis=1),
                              N_EXPERTS - 1)
    pad = jnp.stack([tile_start * tm + counts, tiles_per * tm - counts], axis=1).reshape(-1)
    pad = jnp.concatenate([pad, tile_end[-1:]])
    return dest.astype(jnp.int32), pad.astype(jnp.int32), tile_expert, tile_valid


def _dispatch_kernel(dest_ref, pad_ref, h_ref, xs_hbm, zbuf, sem, zsem, *, tm, n_tiles):
    i = pl.program_id(0)

    @pl.when(i == 0)
    def _():
        zbuf[...] = jnp.zeros_like(zbuf)
        for t in range(n_tiles - N_EXPERTS, n_tiles):
            @pl.when(t >= pad_ref[2 * N_EXPERTS])
            def _(t=t):
                for part in range(tm // ZERO_ROWS):
                    cp = pltpu.make_async_copy(
                        zbuf, xs_hbm.at[pl.ds(t * tm + part * ZERO_ROWS, ZERO_ROWS), :], zsem)
                    cp.start()
                    cp.wait()
        for e in range(N_EXPERTS):
            start = pad_ref[2 * e]
            length = pad_ref[2 * e + 1]
            end = start + length
            for r in range(SUBLANES - 1):
                @pl.when(r < (length & (SUBLANES - 1)))
                def _(r=r):
                    cp = pltpu.make_async_copy(zbuf.at[pl.ds(0, 1), :], xs_hbm.at[pl.ds(start + r, 1), :], zsem)
                    cp.start()
                    cp.wait()
            n = ZERO_ROWS
            while n >= SUBLANES:
                offset = pl.multiple_of(end - (length & ~(2 * n - 1)) - n, SUBLANES)

                @pl.when((length & n) != 0)
                def _(n=n, offset=offset):
                    cp = pltpu.make_async_copy(zbuf.at[pl.ds(0, n), :], xs_hbm.at[pl.ds(offset, n), :], zsem)
                    cp.start()
                    cp.wait()
                n //= 2

    def issue(j, carry):
        row0 = pl.multiple_of(j * DMA_GROUP, DMA_GROUP)
        a0 = (i * tm + row0) * TOP_K
        for u in range(DMA_GROUP):
            src = h_ref.at[pl.ds(row0 + u, 1), :]
            for kk in range(TOP_K):
                pltpu.make_async_copy(src, xs_hbm.at[pl.ds(dest_ref[a0 + TOP_K * u + kk], 1), :], sem).start()
        return carry
    lax.fori_loop(0, tm // DMA_GROUP, issue, 0)
    for _ in range(TOP_K):
        pltpu.make_async_copy(h_ref, xs_hbm.at[pl.ds(0, tm), :], sem).wait()


def _dispatch(h2, dest, pad, n_rows, tm):
    T = h2.shape[0]
    assert tm == 2 * ZERO_ROWS
    kern = functools.partial(_dispatch_kernel, tm=tm, n_tiles=n_rows // tm)
    return pl.pallas_call(
        kern,
        grid_spec=pltpu.PrefetchScalarGridSpec(
            num_scalar_prefetch=2,
            grid=(T // tm,),
            in_specs=[pl.BlockSpec((tm, D_MODEL), lambda i, dest, pad: (i, 0))],
            out_specs=pl.BlockSpec(memory_space=pl.ANY),
            scratch_shapes=[pltpu.VMEM((ZERO_ROWS, D_MODEL), F32), pltpu.SemaphoreType.DMA,
                            pltpu.SemaphoreType.DMA],
        ),
        out_shape=jax.ShapeDtypeStruct((n_rows, D_MODEL), F32),
        compiler_params=_params("arbitrary"),
        name="moe_dispatch",
    )(dest, pad, h2)


def _moe_ffn_kernel(te_ref, tv_ref, xs_ref, wg_ref, wu_ref, wd_ref, ys_ref, hs_ref):
    i = pl.program_id(0)

    @pl.when(pl.program_id(1) == 0)
    def _():
        ys_ref[...] = jnp.zeros_like(ys_ref)
        hs_ref[...] = xs_ref[...].astype(BF16)

    @pl.when(tv_ref[i] > 0)
    def _():
        h = hs_ref[...]
        a = (_silu(_dot(h, wg_ref[0])) * _dot(h, wu_ref[0])).astype(BF16)
        ys_ref[...] += _dot(a, wd_ref[0])


def _moe_ffn(xs, tile_expert, tile_valid, w_gate, w_up, w_down, tm):
    n_rows = xs.shape[0]
    ff = w_gate.shape[2]
    tf = MOE_FF_TILE
    nf = ff // tf
    ff_blk = lambda i, f, tv: f * tv[i] + (nf - 1) * (1 - tv[i])
    return pl.pallas_call(
        _moe_ffn_kernel,
        grid_spec=pltpu.PrefetchScalarGridSpec(
            num_scalar_prefetch=2,
            grid=(n_rows // tm, nf),
            in_specs=[
                pl.BlockSpec((tm, D_MODEL), lambda i, f, te, tv: (i * tv[i], 0)),
                pl.BlockSpec((1, D_MODEL, tf), lambda i, f, te, tv: (te[i], 0, ff_blk(i, f, tv))),
                pl.BlockSpec((1, D_MODEL, tf), lambda i, f, te, tv: (te[i], 0, ff_blk(i, f, tv))),
                pl.BlockSpec((1, tf, D_MODEL), lambda i, f, te, tv: (te[i], ff_blk(i, f, tv), 0)),
            ],
            out_specs=pl.BlockSpec((tm, D_MODEL), lambda i, f, te, tv: (i, 0)),
            scratch_shapes=[pltpu.VMEM((tm, D_MODEL), BF16)],
        ),
        out_shape=jax.ShapeDtypeStruct((n_rows, D_MODEL), F32),
        compiler_params=_params("arbitrary", "arbitrary"),
        name="moe_ffn",
    )(tile_expert, tile_valid, xs, w_gate.astype(BF16), w_up.astype(BF16), w_down.astype(BF16))


def _combine_kernel(pos_ref, x1_ref, wts_ref, ys_hbm, out_ref, buf0, buf1, sem0, sem1, *, tm):
    i = pl.program_id(0)

    def issue(j, carry):
        row0 = pl.multiple_of(j * DMA_GROUP, DMA_GROUP)
        a0 = (i * tm + row0) * TOP_K
        for u in range(DMA_GROUP):
            for buf, sem, kk in ((buf0, sem0, 0), (buf1, sem1, 1)):
                pltpu.make_async_copy(ys_hbm.at[pl.ds(pos_ref[a0 + TOP_K * u + kk], 1), :],
                                      buf.at[pl.ds(row0 + u, 1), :], sem).start()
        return carry
    lax.fori_loop(0, tm // DMA_GROUP, issue, 0)
    pltpu.make_async_copy(ys_hbm.at[pl.ds(0, tm), :], buf0, sem0).wait()
    pltpu.make_async_copy(ys_hbm.at[pl.ds(0, tm), :], buf1, sem1).wait()
    wts = wts_ref[...]
    out_ref[...] = x1_ref[...] + (wts[:, 0:1] * buf0[...] + wts[:, 1:2] * buf1[...])


def _combine(x1, wts, ys, dest, tm):
    T = x1.shape[0]
    kern = functools.partial(_combine_kernel, tm=tm)
    return pl.pallas_call(
        kern,
        grid_spec=pltpu.PrefetchScalarGridSpec(
            num_scalar_prefetch=1,
            grid=(T // tm,),
            in_specs=[pl.BlockSpec((tm, D_MODEL), lambda i, pos: (i, 0)),
                      pl.BlockSpec((tm, LANES), lambda i, pos: (i, 0)),
                      pl.BlockSpec(memory_space=pl.ANY)],
            out_specs=pl.BlockSpec((tm, D_MODEL), lambda i, pos: (i, 0)),
            scratch_shapes=[pltpu.VMEM((tm, D_MODEL), F32), pltpu.VMEM((tm, D_MODEL), F32),
                            pltpu.SemaphoreType.DMA, pltpu.SemaphoreType.DMA],
        ),
        out_shape=jax.ShapeDtypeStruct((T, D_MODEL), F32),
        compiler_params=_params("arbitrary"),
        name="moe_combine",
    )(dest, x1, wts, ys)


def _mla_dense_layer(x2, pos2, batch, seq, attn_norm, w_in, q_a_norm, kv_a_norm, w_uq, w_ukv, q_norm,
                     k_norm, w_o, ffn_norm, w_gate, w_up, w_down):
    q, k, v = _mla_prep(x2, pos2, attn_norm, w_in, q_a_norm, kv_a_norm, w_uq, w_ukv, q_norm, k_norm)
    o = _mla_attn(q, k, v, batch, seq)
    return _proj_dense_ffn(x2, o, w_o, ffn_norm, w_gate, w_up, w_down)


def _band_moe_layer(x2, batch, seq, attn_norm, w_qkv, q_norm, k_norm, rel_bias, w_o, ffn_norm, router,
                    w_gate, w_up, w_down):
    q, k, v = _ca_prep(x2, attn_norm, w_qkv)
    o = _band_attn(q, k, v, q_norm, k_norm, rel_bias, batch, seq)
    x1, h, idx, wts = _proj_router(x2, o, w_o, ffn_norm, router)
    tm = ROW_TILE
    dest, pad, tile_expert, tile_valid = _moe_plan(idx[:, :TOP_K], tm)
    xs = _dispatch(h, dest, pad, tile_expert.shape[0] * tm, tm)
    ys = _moe_ffn(xs, tile_expert, tile_valid, w_gate, w_up, w_down, tm)
    return _combine(x1, wts, ys, dest, tm)


def kernel(x, positions, l0_attn_norm, l0_mla_w_in, l0_mla_q_a_norm, l0_mla_kv_a_norm, l0_mla_w_uq, l0_mla_w_ukv, l0_mla_q_norm, l0_mla_k_norm, l0_mla_w_o, l0_ffn_norm, l0_ffn_w_gate, l0_ffn_w_up, l0_ffn_w_down, l1_attn_norm, l1_ca_w_qkv, l1_ca_q_norm, l1_ca_k_norm, l1_ca_rel_bias, l1_ca_w_o, l1_ffn_norm, l1_moe_router, l1_moe_w_gate, l1_moe_w_up, l1_moe_w_down, l2_attn_norm, l2_mla_w_in, l2_mla_q_a_norm, l2_mla_kv_a_norm, l2_mla_w_uq, l2_mla_w_ukv, l2_mla_q_norm, l2_mla_k_norm, l2_mla_w_o, l2_ffn_norm, l2_ffn_w_gate, l2_ffn_w_up, l2_ffn_w_down, l3_attn_norm, l3_ca_w_qkv, l3_ca_q_norm, l3_ca_k_norm, l3_ca_rel_bias, l3_ca_w_o, l3_ffn_norm, l3_moe_router, l3_moe_w_gate, l3_moe_w_up, l3_moe_w_down):
    batch, seq, d = x.shape
    assert d == D_MODEL and seq % ATTN_Q_TILE == 0 and seq >= BAND_KEYS
    assert (batch * seq) % ROW_TILE == 0
    x2 = x.reshape(batch * seq, d)
    pos2 = positions.reshape(batch * seq, 1)
    x2 = _mla_dense_layer(x2, pos2, batch, seq, l0_attn_norm, l0_mla_w_in, l0_mla_q_a_norm, l0_mla_kv_a_norm,
                          l0_mla_w_uq, l0_mla_w_ukv, l0_mla_q_norm, l0_mla_k_norm, l0_mla_w_o, l0_ffn_norm,
                          l0_ffn_w_gate, l0_ffn_w_up, l0_ffn_w_down)
    x2 = _band_moe_layer(x2, batch, seq, l1_attn_norm, l1_ca_w_qkv, l1_ca_q_norm, l1_ca_k_norm, l1_ca_rel_bias,
                         l1_ca_w_o, l1_ffn_norm, l1_moe_router, l1_moe_w_gate, l1_moe_w_up, l1_moe_w_down)
    x2 = _mla_dense_layer(x2, pos2, batch, seq, l2_attn_norm, l2_mla_w_in, l2_mla_q_a_norm, l2_mla_kv_a_norm,
                          l2_mla_w_uq, l2_mla_w_ukv, l2_mla_q_norm, l2_mla_k_norm, l2_mla_w_o, l2_ffn_norm,
                          l2_ffn_w_gate, l2_ffn_w_up, l2_ffn_w_down)
    x2 = _band_moe_layer(x2, batch, seq, l3_attn_norm, l3_ca_w_qkv, l3_ca_q_norm, l3_ca_k_norm, l3_ca_rel_bias,
                         l3_ca_w_o, l3_ffn_norm, l3_moe_router, l3_moe_w_gate, l3_moe_w_up, l3_moe_w_down)
    return x2.reshape(batch, seq, d)
```

```python
import functools

import numpy as np
import jax
import jax.numpy as jnp
from jax import lax
from jax.experimental import pallas as pl
from jax.experimental.pallas import tpu as pltpu

F32 = jnp.float32
BF16 = jnp.bfloat16

D_MODEL = 1024
CHUNK = 64
RMS_EPS = 1e-6

MLA_HEADS = 8
MLA_NOPE = 128
MLA_ROPE = 64
MLA_V = 128
MLA_QK = MLA_NOPE + MLA_ROPE
Q_LORA = 256
KV_LORA = 128
ROPE_THETA = 10000.0

CA_HEADS = 16
CA_HEAD_DIM = 64
LEFT_CHUNKS = 8
MAX_REL = 256

N_EXPERTS = 8
TOP_K = 2

LANES = 128
SUBLANES = 8
VMEM_LIMIT = 56 * 1024 * 1024

ROW_TILE = 512
MLA_PREP_TILE = 1024
MLA_Q_TILE = 512
MLA_HEADS_PER_STEP = 2
ATTN_Q_TILE = 256
BAND_KEYS = 3 * ATTN_Q_TILE
BAND_ROLL = ATTN_Q_TILE + BAND_KEYS
MOE_FF_TILE = 1792
NEG_INF = float("-inf")


def _dot(a, b):
    return jnp.dot(a, b, preferred_element_type=F32)


def _dot_nt(a, b):
    return lax.dot_general(a, b, (((1,), (1,)), ((), ())), preferred_element_type=F32)


def _rms(x, g):
    return x * lax.rsqrt(jnp.mean(x * x, axis=-1, keepdims=True) + RMS_EPS) * g


def _silu(g):
    return g / (1.0 + jnp.exp(-g))


def _params(*sem):
    return pltpu.CompilerParams(dimension_semantics=sem, vmem_limit_bytes=VMEM_LIMIT)


def _mla_prep_kernel(x_ref, pos_ref, gn_ref, win_ref, qan_ref, kvan_ref, wuq_ref, wukv_ref,
                     invf_ref, qtab_ref, ktab_ref, q_out, k_out, v_out):
    h = _rms(x_ref[...], gn_ref[...]).astype(BF16)
    a = _dot(h, win_ref[...])
    qn = _rms(a[:, :Q_LORA], qan_ref[...]).astype(BF16)
    kvn = _rms(a[:, Q_LORA:Q_LORA + KV_LORA], kvan_ref[...]).astype(BF16)
    kr2 = a[:, Q_LORA + KV_LORA:]
    q_all = _dot(qn, wuq_ref[...])
    kv_all = _dot(kvn, wukv_ref[...])

    ang = pos_ref[...].astype(F32) * invf_ref[...]
    cos = jnp.cos(ang)
    sin = jnp.sin(ang)
    lane = lax.broadcasted_iota(jnp.int32, (1, LANES), 1)
    rope_lane = lane < MLA_ROPE

    q_gn, q_ga, q_gb = qtab_ref[0:1, :], qtab_ref[1:2, :], qtab_ref[2:3, :]
    k_gn, k_ga, k_gb = ktab_ref[0:1, :], ktab_ref[1:2, :], ktab_ref[2:3, :]
    q_cos, q_sin = cos * q_ga, sin * q_gb
    k_rot = kr2 * (cos * k_ga) + pltpu.roll(kr2, MLA_ROPE, 1) * (sin * k_gb)
    kr_ss = jnp.sum(jnp.where(rope_lane, kr2 * kr2, 0.0), axis=-1, keepdims=True)
    scale = MLA_QK ** -0.5

    for hd in range(MLA_HEADS):
        lo = hd * 2 * LANES
        qa = q_all[:, lo:lo + LANES]
        qb = q_all[:, lo + LANES:lo + 2 * LANES]
        ss = jnp.sum(qa * qa + jnp.where(rope_lane, qb * qb, 0.0), axis=-1, keepdims=True)
        r = lax.rsqrt(ss * (1.0 / MLA_QK) + RMS_EPS) * scale
        q_rot = (qb * q_cos + pltpu.roll(qb, MLA_ROPE, 1) * q_sin) * r
        q_out[hd, :, :MLA_NOPE] = (qa * q_gn * r).astype(BF16)
        q_out[hd, :, MLA_NOPE:] = q_rot[:, :MLA_ROPE].astype(BF16)

        kn = kv_all[:, lo:lo + LANES]
        ssk = jnp.sum(kn * kn, axis=-1, keepdims=True) + kr_ss
        rk = lax.rsqrt(ssk * (1.0 / MLA_QK) + RMS_EPS)
        k_out[hd, :, :MLA_NOPE] = (kn * k_gn * rk).astype(BF16)
        k_out[hd, :, MLA_NOPE:] = (k_rot * rk)[:, :MLA_ROPE].astype(BF16)
        v_out[hd] = kv_all[:, lo + LANES:lo + 2 * LANES].astype(BF16)


def _rope_partner():
    d = np.arange(MLA_ROPE)
    half = MLA_ROPE // 2
    partner = np.where(d < half, d + half, d - half)
    sign = np.where(d < half, -1.0, 1.0).astype(np.float32)
    return partner, sign


def _mla_tables(g):
    partner, sign = _rope_partner()
    zeros = jnp.zeros((LANES - MLA_ROPE,), F32)
    g_rope = g[MLA_NOPE:]
    rows = [g[:MLA_NOPE],
            jnp.concatenate([g_rope, zeros]),
            jnp.concatenate([g_rope[partner] * sign, zeros])]
    rows += [jnp.zeros((LANES,), F32)] * 5
    return jnp.stack(rows)


def _mla_prep(x2, pos2, attn_norm, w_in, q_a_norm, kv_a_norm, w_uq, w_ukv, q_norm, k_norm):
    T = x2.shape[0]
    tm = MLA_PREP_TILE
    partner, _ = _rope_partner()
    rope0 = Q_LORA + KV_LORA
    w_in_ext = jnp.concatenate([w_in, w_in[:, rope0 + partner]], axis=1).astype(BF16)
    wq = w_uq.reshape(Q_LORA, MLA_HEADS, MLA_QK)
    w_uq_ext = jnp.concatenate([wq, wq[:, :, MLA_NOPE + partner]], axis=2)
    w_uq_ext = w_uq_ext.reshape(Q_LORA, MLA_HEADS * 2 * LANES).astype(BF16)
    w_ukv_b = w_ukv.astype(BF16)
    inv = 1.0 / (ROPE_THETA ** (jnp.arange(0, MLA_ROPE, 2, dtype=F32) / MLA_ROPE))
    invf = jnp.concatenate([inv, inv, jnp.zeros((LANES - MLA_ROPE,), F32)])[None, :]

    full = lambda shape: pl.BlockSpec(shape, lambda i: (0,) * len(shape))
    out_qk = jax.ShapeDtypeStruct((MLA_HEADS, T, MLA_QK), BF16)
    out_v = jax.ShapeDtypeStruct((MLA_HEADS, T, MLA_V), BF16)
    return pl.pallas_call(
        _mla_prep_kernel,
        grid=(T // tm,),
        in_specs=[
            pl.BlockSpec((tm, D_MODEL), lambda i: (i, 0)),
            pl.BlockSpec((tm, 1), lambda i: (i, 0)),
            full((1, D_MODEL)),
            full((D_MODEL, 4 * LANES)),
            full((1, Q_LORA)),
            full((1, KV_LORA)),
            full((Q_LORA, MLA_HEADS * 2 * LANES)),
            full((KV_LORA, MLA_HEADS * 2 * LANES)),
            full((1, LANES)),
            full((8, LANES)),
            full((8, LANES)),
        ],
        out_specs=[
            pl.BlockSpec((MLA_HEADS, tm, MLA_QK), lambda i: (0, i, 0)),
            pl.BlockSpec((MLA_HEADS, tm, MLA_QK), lambda i: (0, i, 0)),
            pl.BlockSpec((MLA_HEADS, tm, MLA_V), lambda i: (0, i, 0)),
        ],
        out_shape=[out_qk, out_qk, out_v],
        compiler_params=_params("parallel"),
        name="mla_prep",
    )(x2, pos2, attn_norm[None, :], w_in_ext, q_a_norm[None, :], kv_a_norm[None, :],
      w_uq_ext, w_ukv_b, invf, _mla_tables(q_norm), _mla_tables(k_norm))


def _mla_attn_kernel(q_ref, k_ref, v_ref, o_ref, *, seq, tq):
    row_c = lax.broadcasted_iota(jnp.int32, (tq, tq), 0) // CHUNK
    col_c = lax.broadcasted_iota(jnp.int32, (tq, tq), 1) // CHUNK
    diag_mask = col_c <= row_c
    for qi in range(seq // tq):
        lo, hi = qi * tq, (qi + 1) * tq
        for hh in range(MLA_HEADS_PER_STEP):
            q = q_ref[hh, lo:hi, :]
            s_d = jnp.where(diag_mask, _dot_nt(q, k_ref[hh, lo:hi, :]), NEG_INF)
            m = jnp.max(s_d, axis=-1, keepdims=True)
            if qi > 0:
                s_o = _dot_nt(q, k_ref[hh, :lo, :])
                m = jnp.maximum(m, jnp.max(s_o, axis=-1, keepdims=True))
            p_d = jnp.exp(s_d - m)
            l = jnp.sum(p_d, axis=-1, keepdims=True)
            o = _dot(p_d.astype(BF16), v_ref[hh, lo:hi, :])
            if qi > 0:
                p_o = jnp.exp(s_o - m)
                l = l + jnp.sum(p_o, axis=-1, keepdims=True)
                o = o + _dot(p_o.astype(BF16), v_ref[hh, :lo, :])
            o_ref[lo:hi, hh * MLA_V:(hh + 1) * MLA_V] = (o / l).astype(BF16)


def _mla_attn(q, k, v, batch, seq):
    T = batch * seq
    hps = MLA_HEADS_PER_STEP
    kern = functools.partial(_mla_attn_kernel, seq=seq, tq=MLA_Q_TILE)
    return pl.pallas_call(
        kern,
        grid=(MLA_HEADS // hps, batch),
        in_specs=[
            pl.BlockSpec((hps, seq, MLA_QK), lambda h, b: (h, b, 0)),
            pl.BlockSpec((hps, seq, MLA_QK), lambda h, b: (h, b, 0)),
            pl.BlockSpec((hps, seq, MLA_V), lambda h, b: (h, b, 0)),
        ],
        out_specs=pl.BlockSpec((seq, hps * MLA_V), lambda h, b: (b, h)),
        out_shape=jax.ShapeDtypeStruct((T, MLA_HEADS * MLA_V), BF16),
        compiler_params=_params("parallel", "parallel"),
        name="mla_attn",
    )(q, k, v)


def _proj_dense_ffn_kernel(x_ref, o_ref, wo_ref, gn_ref, wg_ref, wu_ref, wd_ref, out_ref):
    x1 = x_ref[...] + _dot(o_ref[...], wo_ref[...])
    h = _rms(x1, gn_ref[...]).astype(BF16)
    a = (_silu(_dot(h, wg_ref[...])) * _dot(h, wu_ref[...])).astype(BF16)
    out_ref[...] = x1 + _dot(a, wd_ref[...])


def _proj_dense_ffn(x2, o2, w_o, ffn_norm, w_gate, w_up, w_down):
    T = x2.shape[0]
    ff = w_gate.shape[1]
    tm = ROW_TILE
    const = lambda shape: pl.BlockSpec(shape, lambda i: (0, 0), pipeline_mode=pl.Buffered(1))
    row = pl.BlockSpec((tm, D_MODEL), lambda i: (i, 0))
    return pl.pallas_call(
        _proj_dense_ffn_kernel,
        grid=(T // tm,),
        in_specs=[row, row, const((D_MODEL, D_MODEL)), const((1, D_MODEL)), const((D_MODEL, ff)),
                  const((D_MODEL, ff)), const((ff, D_MODEL))],
        out_specs=row,
        out_shape=jax.ShapeDtypeStruct((T, D_MODEL), F32),
        compiler_params=_params("parallel"),
        name="proj_dense_ffn",
    )(x2, o2, w_o.astype(BF16), ffn_norm[None, :], w_gate.astype(BF16), w_up.astype(BF16),
      w_down.astype(BF16))


def _ca_prep_kernel(x_ref, gn_ref, w_ref, q_out, k_out, v_out):
    h = _rms(x_ref[...], gn_ref[...]).astype(BF16)
    qkv = _dot(h, w_ref[...])
    q_out[...] = qkv[:, :D_MODEL].astype(BF16)
    k_out[...] = qkv[:, D_MODEL:2 * D_MODEL].astype(BF16)
    v_out[...] = qkv[:, 2 * D_MODEL:].astype(BF16)


def _ca_prep(x2, attn_norm, w_qkv):
    T = x2.shape[0]
    tm = ROW_TILE
    full = lambda shape: pl.BlockSpec(shape, lambda i: (0,) * len(shape))
    out = jax.ShapeDtypeStruct((T, D_MODEL), BF16)
    row = pl.BlockSpec((tm, D_MODEL), lambda i: (i, 0))
    return pl.pallas_call(
        _ca_prep_kernel,
        grid=(T // tm,),
        in_specs=[row, full((1, D_MODEL)), full((D_MODEL, 3 * D_MODEL))],
        out_specs=[row, row, row],
        out_shape=[out, out, out],
        compiler_params=_params("parallel"),
        name="ca_prep",
    )(x2, attn_norm[None, :], w_qkv.astype(BF16))


def _band_attn_kernel(q_ref, k_ref, v_ref, w_ref, g_ref, o_ref, tab_ref, qn_ref, kn_ref, *, seq, tq):
    @pl.when(pl.program_id(1) == 0)
    def _():
        row = lax.broadcasted_iota(jnp.int32, (tq, BAND_KEYS), 0)
        col = lax.broadcasted_iota(jnp.int32, (tq, BAND_KEYS), 1)
        shift = CHUNK.bit_length() - 1
        for hh in range(2):
            for var in range(3):
                w = w_ref[0, 3 * hh + var:3 * hh + var + 1, :]
                t = pltpu.roll(jnp.broadcast_to(w, (tq, BAND_ROLL)), 0, 1, stride=1, stride_axis=0)
                gap = lax.shift_right_logical(row + var * tq, shift) - lax.shift_right_logical(col, shift)
                tab_ref[hh, var] = jnp.where((gap >= 0) & (gap <= LEFT_CHUNKS), t[:, :BAND_KEYS], NEG_INF)

    lane = lax.broadcasted_iota(jnp.int32, (1, LANES), 1)
    head0 = lane < CA_HEAD_DIM

    def head_normed(x_ref, gain):
        x = x_ref[...].astype(F32)
        sq = x * x
        ss0 = jnp.sum(jnp.where(head0, sq, 0.0), axis=-1, keepdims=True)
        ss1 = jnp.sum(jnp.where(head0, 0.0, sq), axis=-1, keepdims=True)
        r = jnp.where(head0, lax.rsqrt(ss0 * (1.0 / CA_HEAD_DIM) + RMS_EPS),
                      lax.rsqrt(ss1 * (1.0 / CA_HEAD_DIM) + RMS_EPS))
        return (x * r * gain).astype(BF16)

    qn_ref[...] = head_normed(q_ref, g_ref[0:1, :])
    kn_ref[...] = head_normed(k_ref, g_ref[1:2, :])

    for qi in range(seq // tq):
        var = min(qi, 2)
        ks = max(qi - 2, 0) * tq
        q = qn_ref[qi * tq:(qi + 1) * tq, :]
        k = kn_ref[ks:ks + BAND_KEYS, :]
        v = v_ref[ks:ks + BAND_KEYS, :]
        zero = jnp.zeros_like(q)
        qq = jnp.concatenate([jnp.where(head0, q, zero), jnp.where(head0, zero, q)], axis=0)
        s = _dot_nt(qq, k)
        ps, ls = [], []
        for hh in range(2):
            sh = s[hh * tq:(hh + 1) * tq] + tab_ref[hh, var]
            p = jnp.exp(sh - jnp.max(sh, axis=-1, keepdims=True))
            ls.append(jnp.sum(p, axis=-1, keepdims=True))
            ps.append(p.astype(BF16))
        oo = _dot(jnp.concatenate(ps, axis=0), v)
        o = jnp.where(head0, oo[:tq] / ls[0], oo[tq:] / ls[1])
        o_ref[qi * tq:(qi + 1) * tq, :] = o.astype(BF16)


def _band_bias_rows(rel_bias):
    tq = ATTN_Q_TILE
    rb = rel_bias.astype(F32)
    r_max = 3 * tq - 1
    n_far = r_max - MAX_REL
    n_m = r_max + BAND_KEYS
    n_neg = n_m - n_far - rb.shape[1]
    heads = rb.shape[0]
    m = jnp.concatenate([jnp.broadcast_to(rb[:, -1:], (heads, n_far)), rb[:, ::-1],
                         jnp.broadcast_to(rb[:, :1], (heads, n_neg))], axis=1)
    rows = []
    for var in range(3):
        base = r_max - var * tq
        rows.append(jnp.concatenate([m[:, base:base + BAND_KEYS], jnp.zeros((heads, 1), F32),
                                     m[:, base - (tq - 1):base]], axis=1))
    w = jnp.stack(rows, axis=1).reshape(heads // 2, 6, BAND_ROLL)
    return jnp.concatenate([w, jnp.zeros((heads // 2, 2, BAND_ROLL), F32)], axis=1)


def _band_attn(q, k, v, q_norm, k_norm, rel_bias, batch, seq):
    T = batch * seq
    tq = ATTN_Q_TILE
    gains = jnp.stack([jnp.tile(q_norm, 2) * CA_HEAD_DIM ** -0.5, jnp.tile(k_norm, 2)]
                      + [jnp.zeros((LANES,), F32)] * 6)
    kern = functools.partial(_band_attn_kernel, seq=seq, tq=tq)
    slab = pl.BlockSpec((seq, LANES), lambda hp, b: (b, hp))
    return pl.pallas_call(
        kern,
        grid=(CA_HEADS // 2, batch),
        in_specs=[slab, slab, slab, pl.BlockSpec((1, 8, BAND_ROLL), lambda hp, b: (hp, 0, 0)),
                  pl.BlockSpec((8, LANES), lambda hp, b: (0, 0))],
        out_specs=slab,
        out_shape=jax.ShapeDtypeStruct((T, D_MODEL), BF16),
        scratch_shapes=[pltpu.VMEM((2, 3, tq, BAND_KEYS), F32), pltpu.VMEM((seq, LANES), BF16),
                        pltpu.VMEM((seq, LANES), BF16)],
        compiler_params=_params("arbitrary", "arbitrary"),
        name="band_attn",
    )(q, k, v, _band_bias_rows(rel_bias), gains)


def _proj_router_kernel(x_ref, o_ref, wo_ref, gn_ref, rhi_ref, rlo_ref, x1_out, h_out, idx_out, w_out):
    x1 = x_ref[...] + _dot(o_ref[...], wo_ref[...])
    x1_out[...] = x1
    h = _rms(x1, gn_ref[...])
    h_out[...] = h
    h_hi = h.astype(BF16)
    h_lo = (h - h_hi.astype(F32)).astype(BF16)
    logits = _dot(h_hi, rhi_ref[...]) + (_dot(h_lo, rhi_ref[...]) + _dot(h_hi, rlo_ref[...]))

    lane = lax.broadcasted_iota(jnp.int32, logits.shape, 1)
    lane_f = lane.astype(F32)
    lg = jnp.where(lane < N_EXPERTS, logits, NEG_INF)
    m1 = jnp.max(lg, axis=-1, keepdims=True)
    i1 = jnp.min(jnp.where(lg == m1, lane_f, float(LANES)), axis=-1, keepdims=True)
    lg2 = jnp.where(lane_f == i1, NEG_INF, lg)
    m2 = jnp.max(lg2, axis=-1, keepdims=True)
    i2 = jnp.min(jnp.where(lg2 == m2, lane_f, float(LANES)), axis=-1, keepdims=True)
    e = jnp.exp(m2 - m1)
    w1 = 1.0 / (1.0 + e)
    w2 = e / (1.0 + e)
    idx_out[...] = jnp.where(lane == 0, i1, jnp.where(lane == 1, i2, 0.0)).astype(jnp.int32)
    w_out[...] = jnp.where(lane == 0, w1, jnp.where(lane == 1, w2, 0.0))


def _proj_router(x2, o2, w_o, ffn_norm, router):
    T = x2.shape[0]
    tm = ROW_TILE
    r_pad = jnp.zeros((D_MODEL, LANES), F32).at[:, :N_EXPERTS].set(router)
    r_hi = r_pad.astype(BF16)
    r_lo = (r_pad - r_hi.astype(F32)).astype(BF16)
    full = lambda shape: pl.BlockSpec(shape, lambda i: (0,) * len(shape))
    row = pl.BlockSpec((tm, D_MODEL), lambda i: (i, 0))
    small = pl.BlockSpec((tm, LANES), lambda i: (i, 0))
    return pl.pallas_call(
        _proj_router_kernel,
        grid=(T // tm,),
        in_specs=[row, row, full((D_MODEL, D_MODEL)), full((1, D_MODEL)), full((D_MODEL, LANES)),
                  full((D_MODEL, LANES))],
        out_specs=[row, row, small, small],
        out_shape=[jax.ShapeDtypeStruct((T, D_MODEL), F32), jax.ShapeDtypeStruct((T, D_MODEL), F32),
                   jax.ShapeDtypeStruct((T, LANES), jnp.int32), jax.ShapeDtypeStruct((T, LANES), F32)],
        compiler_params=_params("parallel"),
        name="proj_router",
    )(x2, o2, w_o.astype(BF16), ffn_norm[None, :], r_hi, r_lo)


def _moe_plan(top_idx, tm):
    T = top_idx.shape[0]
    n_assign = T * TOP_K
    n_tiles = n_assign // tm + N_EXPERTS
    n_rows = n_tiles * tm
    plane = T + tm
    e_flat = top_idx.reshape(1, n_assign)
    experts = jnp.arange(N_EXPERTS, dtype=jnp.int32)[:, None]
    onehot = (e_flat == experts).astype(jnp.int32)
    csum = jnp.cumsum(onehot, axis=1)
    counts = csum[:, -1]
    tiles_per = (counts + tm - 1) // tm
    tile_end = jnp.cumsum(tiles_per)
    tile_start = tile_end - tiles_per
    dest = jnp.sum(onehot * (csum - 1 + tile_start[:, None] * tm), axis=0)
    a = jnp.arange(n_assign, dtype=jnp.int32)
    slot_real = (a % TOP_K) * plane + a // TOP_K
    s = jnp.arange(tm, dtype=jnp.int32)[None, :]
    pad_row = (tile_start * tm + counts)[:, None] + s
    pad_len = (tiles_per * tm - counts)[:, None]
    pad_key = jnp.where(s < pad_len, pad_row, n_rows + experts * tm + s)
    slot_pad = ((pad_row // tm) % 2) * plane + T + pad_row % tm
    keys = jnp.concatenate([dest, pad_key.reshape(-1)])
    slots = jnp.concatenate([slot_real, slot_pad.reshape(-1)])
    _, row_slot = lax.sort((keys, slots), num_keys=1)

    tile_id = jnp.arange(n_tiles, dtype=jnp.int32)
    tile_valid = (tile_id < tile_end[-1]).astype(jnp.int32)
    tile_expert = jnp.minimum(jnp.sum((tile_id[:, None] >= tile_end[None, :]).astype(jnp.int32), axis=1),
                              N_EXPERTS - 1)
    return row_slot.astype(jnp.int32), tile_expert, tile_valid


def _moe_ffn_kernel(te_ref, tv_ref, slot_ref, h_hbm, wg_ref, wu_ref, wd_ref, y_hbm,
                    xbuf, ybuf, hs_ref, gsem, ssem, *, tm, n_tok):
    i = pl.program_id(0)
    f = pl.program_id(1)
    s = lax.rem(i, 2)
    plane = n_tok + tm
    valid = tv_ref[i] > 0
    prev_valid = (i > 0) & (tv_ref[jnp.maximum(i - 1, 0)] > 0)

    def gather_copy(tile, slot, j):
        p = slot_ref[tile * tm + j]
        tok = jnp.minimum(jnp.where(p >= plane, p - plane, p), n_tok - 1)
        return pltpu.make_async_copy(h_hbm.at[pl.ds(tok, 1), :], xbuf.at[slot, pl.ds(j, 1), :], gsem.at[slot])

    def scatter_copy(tile, slot, j):
        p = slot_ref[tile * tm + j]
        return pltpu.make_async_copy(ybuf.at[slot, pl.ds(j, 1), :], y_hbm.at[pl.ds(p, 1), :], ssem.at[slot])

    def wait_gather(slot):
        pltpu.make_async_copy(h_hbm.at[pl.ds(0, tm), :], xbuf.at[slot], gsem.at[slot]).wait()

    def wait_scatter(slot):
        pltpu.make_async_copy(ybuf.at[slot], y_hbm.at[pl.ds(0, tm), :], ssem.at[slot]).wait()

    def partial_ffn():
        h = hs_ref[...]
        a = (_silu(_dot(h, wg_ref[0])) * _dot(h, wu_ref[0])).astype(BF16)
        return _dot(a, wd_ref[0])

    @pl.when((f == 0) & (i == 0))
    def _():
        ybuf[1] = jnp.zeros((tm, D_MODEL), F32)
        for k in range(TOP_K):
            cp = pltpu.make_async_copy(ybuf.at[1], y_hbm.at[pl.ds(k * plane + n_tok, tm), :], ssem.at[1])
            cp.start()
            cp.wait()

        def first(j, carry):
            gather_copy(0, 0, j).start()
            return carry
        lax.fori_loop(0, tm, first, 0)

    @pl.when((f == 0) & ((i == 0) | prev_valid))
    def _():
        wait_gather(s)

    @pl.when((f == 0) & (i >= 2) & prev_valid)
    def _():
        wait_scatter(s)

    @pl.when(valid & (f == 0))
    def _():
        hs_ref[...] = xbuf[s].astype(BF16)
        for j in range(tm):
            gather_copy(i + 1, 1 - s, j).start()
        ybuf[s] = partial_ffn()

    @pl.when(valid & (f == 1) & (i == 0))
    def _():
        ybuf[s] += partial_ffn()

    @pl.when(valid & (f == 1) & (i > 0))
    def _():
        for j in range(tm):
            scatter_copy(i - 1, 1 - s, j).start()
        ybuf[s] += partial_ffn()

    @pl.when(jnp.logical_not(valid) & (f == 1) & prev_valid)
    def _():
        def last(j, carry):
            scatter_copy(i - 1, 1 - s, j).start()
            return carry
        lax.fori_loop(0, tm, last, 0)
        wait_scatter(1 - s)


def _moe_ffn(h2, row_slot, tile_expert, tile_valid, w_gate, w_up, w_down, tm):
    n_tok = h2.shape[0]
    n_tiles = tile_expert.shape[0]
    ff = w_gate.shape[2]
    tf = MOE_FF_TILE
    nf = ff // tf
    assert nf == 2
    ff_blk = lambda i, f, tv: f * tv[i] + (nf - 1) * (1 - tv[i])
    kern = functools.partial(_moe_ffn_kernel, tm=tm, n_tok=n_tok)
    return pl.pallas_call(
        kern,
        grid_spec=pltpu.PrefetchScalarGridSpec(
            num_scalar_prefetch=3,
            grid=(n_tiles, nf),
            in_specs=[
                pl.BlockSpec(memory_space=pl.ANY),
                pl.BlockSpec((1, D_MODEL, tf), lambda i, f, te, tv, sl: (te[i], 0, ff_blk(i, f, tv))),
                pl.BlockSpec((1, D_MODEL, tf), lambda i, f, te, tv, sl: (te[i], 0, ff_blk(i, f, tv))),
                pl.BlockSpec((1, tf, D_MODEL), lambda i, f, te, tv, sl: (te[i], ff_blk(i, f, tv), 0)),
            ],
            out_specs=pl.BlockSpec(memory_space=pl.ANY),
            scratch_shapes=[pltpu.VMEM((2, tm, D_MODEL), F32), pltpu.VMEM((2, tm, D_MODEL), F32),
                            pltpu.VMEM((tm, D_MODEL), BF16), pltpu.SemaphoreType.DMA((2,)),
                            pltpu.SemaphoreType.DMA((2,))],
        ),
        out_shape=jax.ShapeDtypeStruct((TOP_K * (n_tok + tm), D_MODEL), F32),
        compiler_params=_params("arbitrary", "arbitrary"),
        name="moe_ffn",
    )(tile_expert, tile_valid, row_slot, h2, w_gate.astype(BF16), w_up.astype(BF16), w_down.astype(BF16))


def _moe_sum_kernel(x1_ref, wts_ref, y0_ref, y1_ref, out_ref):
    wts = wts_ref[...]
    out_ref[...] = x1_ref[...] + (wts[:, 0:1] * y0_ref[...] + wts[:, 1:2] * y1_ref[...])


def _moe_sum(x1, wts, y, tm):
    T = x1.shape[0]
    plane_blocks = (T + tm) // tm
    row = pl.BlockSpec((tm, D_MODEL), lambda i: (i, 0))
    return pl.pallas_call(
        _moe_sum_kernel,
        grid=(T // tm,),
        in_specs=[row, pl.BlockSpec((tm, LANES), lambda i: (i, 0)), row,
                  pl.BlockSpec((tm, D_MODEL), lambda i: (plane_blocks + i, 0))],
        out_specs=row,
        out_shape=jax.ShapeDtypeStruct((T, D_MODEL), F32),
        compiler_params=_params("parallel"),
        name="moe_sum",
    )(x1, wts, y, y)


def _mla_dense_layer(x2, pos2, batch, seq, attn_norm, w_in, q_a_norm, kv_a_norm, w_uq, w_ukv, q_norm,
                     k_norm, w_o, ffn_norm, w_gate, w_up, w_down):
    q, k, v = _mla_prep(x2, pos2, attn_norm, w_in, q_a_norm, kv_a_norm, w_uq, w_ukv, q_norm, k_norm)
    o = _mla_attn(q, k, v, batch, seq)
    return _proj_dense_ffn(x2, o, w_o, ffn_norm, w_gate, w_up, w_down)


def _band_moe_layer(x2, batch, seq, attn_norm, w_qkv, q_norm, k_norm, rel_bias, w_o, ffn_norm, router,
                    w_gate, w_up, w_down):
    q, k, v = _ca_prep(x2, attn_norm, w_qkv)
    o = _band_attn(q, k, v, q_norm, k_norm, rel_bias, batch, seq)
    x1, h, idx, wts = _proj_router(x2, o, w_o, ffn_norm, router)
    tm = ROW_TILE
    row_slot, tile_expert, tile_valid = _moe_plan(idx[:, :TOP_K], tm)
    y = _moe_ffn(h, row_slot, tile_expert, tile_valid, w_gate, w_up, w_down, tm)
    return _moe_sum(x1, wts, y, tm)


def kernel(x, positions, l0_attn_norm, l0_mla_w_in, l0_mla_q_a_norm, l0_mla_kv_a_norm, l0_mla_w_uq, l0_mla_w_ukv, l0_mla_q_norm, l0_mla_k_norm, l0_mla_w_o, l0_ffn_norm, l0_ffn_w_gate, l0_ffn_w_up, l0_ffn_w_down, l1_attn_norm, l1_ca_w_qkv, l1_ca_q_norm, l1_ca_k_norm, l1_ca_rel_bias, l1_ca_w_o, l1_ffn_norm, l1_moe_router, l1_moe_w_gate, l1_moe_w_up, l1_moe_w_down, l2_attn_norm, l2_mla_w_in, l2_mla_q_a_norm, l2_mla_kv_a_norm, l2_mla_w_uq, l2_mla_w_ukv, l2_mla_q_norm, l2_mla_k_norm, l2_mla_w_o, l2_ffn_norm, l2_ffn_w_gate, l2_ffn_w_up, l2_ffn_w_down, l3_attn_norm, l3_ca_w_qkv, l3_ca_q_norm, l3_ca_k_norm, l3_ca_rel_bias, l3_ca_w_o, l3_ffn_norm, l3_moe_router, l3_moe_w_gate, l3_moe_w_up, l3_moe_w_down):
    batch, seq, d = x.shape
    assert d == D_MODEL and seq % ATTN_Q_TILE == 0 and seq >= BAND_KEYS
    assert (batch * seq) % ROW_TILE == 0
    x2 = x.reshape(batch * seq, d)
    pos2 = positions.reshape(batch * seq, 1)
    x2 = _mla_dense_layer(x2, pos2, batch, seq, l0_attn_norm, l0_mla_w_in, l0_mla_q_a_norm, l0_mla_kv_a_norm,
                          l0_mla_w_uq, l0_mla_w_ukv, l0_mla_q_norm, l0_mla_k_norm, l0_mla_w_o, l0_ffn_norm,
                          l0_ffn_w_gate, l0_ffn_w_up, l0_ffn_w_down)
    x2 = _band_moe_layer(x2, batch, seq, l1_attn_norm, l1_ca_w_qkv, l1_ca_q_norm, l1_ca_k_norm, l1_ca_rel_bias,
                         l1_ca_w_o, l1_ffn_norm, l1_moe_router, l1_moe_w_gate, l1_moe_w_up, l1_moe_w_down)
    x2 = _mla_dense_layer(x2, pos2, batch, seq, l2_attn_norm, l2_mla_w_in, l2_mla_q_a_norm, l2_mla_kv_a_norm,
                          l2_mla_w_uq, l2_mla_w_ukv, l2_mla_q_norm, l2_mla_k_norm, l2_mla_w_o, l2_ffn_norm,
                          l2_ffn_w_gate, l2_ffn_w_up, l2_ffn_w_down)
    x2 = _band_moe_layer(x2, batch, seq, l3_attn_norm, l3_ca_w_qkv, l3_ca_q_norm, l3_ca_k_norm, l3_ca_rel_bias,
                         l3_ca_w_o, l3_ffn_norm, l3_moe_router, l3_moe_w_gate, l3_moe_w_up, l3_moe_w_down)
    return x2.reshape(batch, seq, d)
```

```python
import functools

import numpy as np
import jax
import jax.numpy as jnp
from jax import lax
from jax.experimental import pallas as pl
from jax.experimental.pallas import tpu as pltpu

F32 = jnp.float32
BF16 = jnp.bfloat16

D_MODEL = 1024
CHUNK = 64
RMS_EPS = 1e-6

MLA_HEADS = 8
MLA_NOPE = 128
MLA_ROPE = 64
MLA_V = 128
MLA_QK = MLA_NOPE + MLA_ROPE
Q_LORA = 256
KV_LORA = 128
ROPE_THETA = 10000.0

CA_HEADS = 16
CA_HEAD_DIM = 64
LEFT_CHUNKS = 8
MAX_REL = 256

N_EXPERTS = 8
TOP_K = 2

LANES = 128
SUBLANES = 8
VMEM_LIMIT = 56 * 1024 * 1024

ROW_TILE = 512
MLA_PREP_TILE = 1024
MLA_Q_TILE = 512
MLA_HEADS_PER_STEP = 2
ATTN_Q_TILE = 256
BAND_KEYS = 3 * ATTN_Q_TILE
BAND_ROLL = ATTN_Q_TILE + BAND_KEYS
MOE_FF_TILE = 1792
NEG_INF = float("-inf")


def _dot(a, b):
    return jnp.dot(a, b, preferred_element_type=F32)


def _dot_nt(a, b):
    return lax.dot_general(a, b, (((1,), (1,)), ((), ())), preferred_element_type=F32)


def _rms(x, g):
    return x * lax.rsqrt(jnp.mean(x * x, axis=-1, keepdims=True) + RMS_EPS) * g


def _silu(g):
    return g / (1.0 + jnp.exp(-g))


def _params(*sem):
    return pltpu.CompilerParams(dimension_semantics=sem, vmem_limit_bytes=VMEM_LIMIT)


def _mla_prep_kernel(x_ref, pos_ref, gn_ref, win_ref, qan_ref, kvan_ref, wuq_ref, wukv_ref,
                     invf_ref, qtab_ref, ktab_ref, q_out, k_out, v_out):
    h = _rms(x_ref[...], gn_ref[...]).astype(BF16)
    a = _dot(h, win_ref[...])
    qn = _rms(a[:, :Q_LORA], qan_ref[...]).astype(BF16)
    kvn = _rms(a[:, Q_LORA:Q_LORA + KV_LORA], kvan_ref[...]).astype(BF16)
    kr2 = a[:, Q_LORA + KV_LORA:]
    q_all = _dot(qn, wuq_ref[...])
    kv_all = _dot(kvn, wukv_ref[...])

    ang = pos_ref[...].astype(F32) * invf_ref[...]
    cos = jnp.cos(ang)
    sin = jnp.sin(ang)
    lane = lax.broadcasted_iota(jnp.int32, (1, LANES), 1)
    rope_lane = lane < MLA_ROPE

    q_gn, q_ga, q_gb = qtab_ref[0:1, :], qtab_ref[1:2, :], qtab_ref[2:3, :]
    k_gn, k_ga, k_gb = ktab_ref[0:1, :], ktab_ref[1:2, :], ktab_ref[2:3, :]
    q_cos, q_sin = cos * q_ga, sin * q_gb
    k_rot = kr2 * (cos * k_ga) + pltpu.roll(kr2, MLA_ROPE, 1) * (sin * k_gb)
    kr_ss = jnp.sum(jnp.where(rope_lane, kr2 * kr2, 0.0), axis=-1, keepdims=True)
    scale = MLA_QK ** -0.5

    for hd in range(MLA_HEADS):
        lo = hd * 2 * LANES
        qa = q_all[:, lo:lo + LANES]
        qb = q_all[:, lo + LANES:lo + 2 * LANES]
        ss = jnp.sum(qa * qa + jnp.where(rope_lane, qb * qb, 0.0), axis=-1, keepdims=True)
        r = lax.rsqrt(ss * (1.0 / MLA_QK) + RMS_EPS) * scale
        q_rot = (qb * q_cos + pltpu.roll(qb, MLA_ROPE, 1) * q_sin) * r
        q_out[hd, :, :MLA_NOPE] = (qa * q_gn * r).astype(BF16)
        q_out[hd, :, MLA_NOPE:] = q_rot[:, :MLA_ROPE].astype(BF16)

        kn = kv_all[:, lo:lo + LANES]
        ssk = jnp.sum(kn * kn, axis=-1, keepdims=True) + kr_ss
        rk = lax.rsqrt(ssk * (1.0 / MLA_QK) + RMS_EPS)
        k_out[hd, :, :MLA_NOPE] = (kn * k_gn * rk).astype(BF16)
        k_out[hd, :, MLA_NOPE:] = (k_rot * rk)[:, :MLA_ROPE].astype(BF16)
        v_out[hd] = kv_all[:, lo + LANES:lo + 2 * LANES].astype(BF16)


def _rope_partner():
    d = np.arange(MLA_ROPE)
    half = MLA_ROPE // 2
    partner = np.where(d < half, d + half, d - half)
    sign = np.where(d < half, -1.0, 1.0).astype(np.float32)
    return partner, sign


def _mla_tables(g):
    partner, sign = _rope_partner()
    zeros = jnp.zeros((LANES - MLA_ROPE,), F32)
    g_rope = g[MLA_NOPE:]
    rows = [g[:MLA_NOPE],
            jnp.concatenate([g_rope, zeros]),
            jnp.concatenate([g_rope[partner] * sign, zeros])]
    rows += [jnp.zeros((LANES,), F32)] * 5
    return jnp.stack(rows)


def _mla_prep(x2, pos2, attn_norm, w_in, q_a_norm, kv_a_norm, w_uq, w_ukv, q_norm, k_norm):
    T = x2.shape[0]
    tm = MLA_PREP_TILE
    partner, _ = _rope_partner()
    rope0 = Q_LORA + KV_LORA
    w_in_ext = jnp.concatenate([w_in, w_in[:, rope0 + partner]], axis=1).astype(BF16)
    wq = w_uq.reshape(Q_LORA, MLA_HEADS, MLA_QK)
    w_uq_ext = jnp.concatenate([wq, wq[:, :, MLA_NOPE + partner]], axis=2)
    w_uq_ext = w_uq_ext.reshape(Q_LORA, MLA_HEADS * 2 * LANES).astype(BF16)
    w_ukv_b = w_ukv.astype(BF16)
    inv = 1.0 / (ROPE_THETA ** (jnp.arange(0, MLA_ROPE, 2, dtype=F32) / MLA_ROPE))
    invf = jnp.concatenate([inv, inv, jnp.zeros((LANES - MLA_ROPE,), F32)])[None, :]

    full = lambda shape: pl.BlockSpec(shape, lambda i: (0,) * len(shape))
    out_qk = jax.ShapeDtypeStruct((MLA_HEADS, T, MLA_QK), BF16)
    out_v = jax.ShapeDtypeStruct((MLA_HEADS, T, MLA_V), BF16)
    return pl.pallas_call(
        _mla_prep_kernel,
        grid=(T // tm,),
        in_specs=[
            pl.BlockSpec((tm, D_MODEL), lambda i: (i, 0)),
            pl.BlockSpec((tm, 1), lambda i: (i, 0)),
            full((1, D_MODEL)),
            full((D_MODEL, 4 * LANES)),
            full((1, Q_LORA)),
            full((1, KV_LORA)),
            full((Q_LORA, MLA_HEADS * 2 * LANES)),
            full((KV_LORA, MLA_HEADS * 2 * LANES)),
            full((1, LANES)),
            full((8, LANES)),
            full((8, LANES)),
        ],
        out_specs=[
            pl.BlockSpec((MLA_HEADS, tm, MLA_QK), lambda i: (0, i, 0)),
            pl.BlockSpec((MLA_HEADS, tm, MLA_QK), lambda i: (0, i, 0)),
            pl.BlockSpec((MLA_HEADS, tm, MLA_V), lambda i: (0, i, 0)),
        ],
        out_shape=[out_qk, out_qk, out_v],
        compiler_params=_params("parallel"),
        name="mla_prep",
    )(x2, pos2, attn_norm[None, :], w_in_ext, q_a_norm[None, :], kv_a_norm[None, :],
      w_uq_ext, w_ukv_b, invf, _mla_tables(q_norm), _mla_tables(k_norm))


def _mla_attn_kernel(q_ref, k_ref, v_ref, o_ref, *, seq, tq):
    row_c = lax.broadcasted_iota(jnp.int32, (tq, tq), 0) // CHUNK
    col_c = lax.broadcasted_iota(jnp.int32, (tq, tq), 1) // CHUNK
    diag_mask = col_c <= row_c
    for qi in range(seq // tq):
        lo, hi = qi * tq, (qi + 1) * tq
        for hh in range(MLA_HEADS_PER_STEP):
            q = q_ref[hh, lo:hi, :]
            s_d = jnp.where(diag_mask, _dot_nt(q, k_ref[hh, lo:hi, :]), NEG_INF)
            m = jnp.max(s_d, axis=-1, keepdims=True)
            if qi > 0:
                s_o = _dot_nt(q, k_ref[hh, :lo, :])
                m = jnp.maximum(m, jnp.max(s_o, axis=-1, keepdims=True))
            p_d = jnp.exp(s_d - m)
            l = jnp.sum(p_d, axis=-1, keepdims=True)
            o = _dot(p_d.astype(BF16), v_ref[hh, lo:hi, :])
            if qi > 0:
                p_o = jnp.exp(s_o - m)
                l = l + jnp.sum(p_o, axis=-1, keepdims=True)
                o = o + _dot(p_o.astype(BF16), v_ref[hh, :lo, :])
            o_ref[lo:hi, hh * MLA_V:(hh + 1) * MLA_V] = (o / l).astype(BF16)


def _mla_attn(q, k, v, batch, seq):
    T = batch * seq
    hps = MLA_HEADS_PER_STEP
    kern = functools.partial(_mla_attn_kernel, seq=seq, tq=MLA_Q_TILE)
    return pl.pallas_call(
        kern,
        grid=(MLA_HEADS // hps, batch),
        in_specs=[
            pl.BlockSpec((hps, seq, MLA_QK), lambda h, b: (h, b, 0)),
            pl.BlockSpec((hps, seq, MLA_QK), lambda h, b: (h, b, 0)),
            pl.BlockSpec((hps, seq, MLA_V), lambda h, b: (h, b, 0)),
        ],
        out_specs=pl.BlockSpec((seq, hps * MLA_V), lambda h, b: (b, h)),
        out_shape=jax.ShapeDtypeStruct((T, MLA_HEADS * MLA_V), BF16),
        compiler_params=_params("parallel", "parallel"),
        name="mla_attn",
    )(q, k, v)


def _proj_dense_ffn_kernel(x_ref, o_ref, wo_ref, gn_ref, wg_ref, wu_ref, wd_ref, out_ref):
    x1 = x_ref[...] + _dot(o_ref[...], wo_ref[...])
    h = _rms(x1, gn_ref[...]).astype(BF16)
    a = (_silu(_dot(h, wg_ref[...])) * _dot(h, wu_ref[...])).astype(BF16)
    out_ref[...] = x1 + _dot(a, wd_ref[...])


def _proj_dense_ffn(x2, o2, w_o, ffn_norm, w_gate, w_up, w_down):
    T = x2.shape[0]
    ff = w_gate.shape[1]
    tm = ROW_TILE
    const = lambda shape: pl.BlockSpec(shape, lambda i: (0, 0), pipeline_mode=pl.Buffered(1))
    row = pl.BlockSpec((tm, D_MODEL), lambda i: (i, 0))
    return pl.pallas_call(
        _proj_dense_ffn_kernel,
        grid=(T // tm,),
        in_specs=[row, row, const((D_MODEL, D_MODEL)), const((1, D_MODEL)), const((D_MODEL, ff)),
                  const((D_MODEL, ff)), const((ff, D_MODEL))],
        out_specs=row,
        out_shape=jax.ShapeDtypeStruct((T, D_MODEL), F32),
        compiler_params=_params("parallel"),
        name="proj_dense_ffn",
    )(x2, o2, w_o.astype(BF16), ffn_norm[None, :], w_gate.astype(BF16), w_up.astype(BF16),
      w_down.astype(BF16))


def _ca_prep_kernel(x_ref, gn_ref, w_ref, q_out, k_out, v_out):
    h = _rms(x_ref[...], gn_ref[...]).astype(BF16)
    qkv = _dot(h, w_ref[...])
    q_out[...] = qkv[:, :D_MODEL].astype(BF16)
    k_out[...] = qkv[:, D_MODEL:2 * D_MODEL].astype(BF16)
    v_out[...] = qkv[:, 2 * D_MODEL:].astype(BF16)


def _ca_prep(x2, attn_norm, w_qkv):
    T = x2.shape[0]
    tm = ROW_TILE
    full = lambda shape: pl.BlockSpec(shape, lambda i: (0,) * len(shape))
    out = jax.ShapeDtypeStruct((T, D_MODEL), BF16)
    row = pl.BlockSpec((tm, D_MODEL), lambda i: (i, 0))
    return pl.pallas_call(
        _ca_prep_kernel,
        grid=(T // tm,),
        in_specs=[row, full((1, D_MODEL)), full((D_MODEL, 3 * D_MODEL))],
        out_specs=[row, row, row],
        out_shape=[out, out, out],
        compiler_params=_params("parallel"),
        name="ca_prep",
    )(x2, attn_norm[None, :], w_qkv.astype(BF16))


def _band_attn_kernel(q_ref, k_ref, v_ref, w_ref, g_ref, o_ref, tab_ref, qn_ref, kn_ref, *, seq, tq):
    @pl.when(pl.program_id(1) == 0)
    def _():
        row = lax.broadcasted_iota(jnp.int32, (tq, BAND_KEYS), 0)
        col = lax.broadcasted_iota(jnp.int32, (tq, BAND_KEYS), 1)
        shift = CHUNK.bit_length() - 1
        for hh in range(2):
            for var in range(3):
                w = w_ref[0, 3 * hh + var:3 * hh + var + 1, :]
                t = pltpu.roll(jnp.broadcast_to(w, (tq, BAND_ROLL)), 0, 1, stride=1, stride_axis=0)
                gap = lax.shift_right_logical(row + var * tq, shift) - lax.shift_right_logical(col, shift)
                tab_ref[hh, var] = jnp.where((gap >= 0) & (gap <= LEFT_CHUNKS), t[:, :BAND_KEYS], NEG_INF)

    lane = lax.broadcasted_iota(jnp.int32, (1, LANES), 1)
    head0 = lane < CA_HEAD_DIM

    def head_normed(x_ref, gain):
        x = x_ref[...].astype(F32)
        sq = x * x
        ss0 = jnp.sum(jnp.where(head0, sq, 0.0), axis=-1, keepdims=True)
        ss1 = jnp.sum(jnp.where(head0, 0.0, sq), axis=-1, keepdims=True)
        r = jnp.where(head0, lax.rsqrt(ss0 * (1.0 / CA_HEAD_DIM) + RMS_EPS),
                      lax.rsqrt(ss1 * (1.0 / CA_HEAD_DIM) + RMS_EPS))
        return (x * r * gain).astype(BF16)

    qn_ref[...] = head_normed(q_ref, g_ref[0:1, :])
    kn_ref[...] = head_normed(k_ref, g_ref[1:2, :])

    for qi in range(seq // tq):
        var = min(qi, 2)
        ks = max(qi - 2, 0) * tq
        q = qn_ref[qi * tq:(qi + 1) * tq, :]
        k = kn_ref[ks:ks + BAND_KEYS, :]
        v = v_ref[ks:ks + BAND_KEYS, :]
        zero = jnp.zeros_like(q)
        qq = jnp.concatenate([jnp.where(head0, q, zero), jnp.where(head0, zero, q)], axis=0)
        s = _dot_nt(qq, k)
        ps, ls = [], []
        for hh in range(2):
            sh = s[hh * tq:(hh + 1) * tq] + tab_ref[hh, var]
            p = jnp.exp(sh - jnp.max(sh, axis=-1, keepdims=True))
            ls.append(jnp.sum(p, axis=-1, keepdims=True))
            ps.append(p.astype(BF16))
        oo = _dot(jnp.concatenate(ps, axis=0), v)
        o = jnp.where(head0, oo[:tq] / ls[0], oo[tq:] / ls[1])
        o_ref[qi * tq:(qi + 1) * tq, :] = o.astype(BF16)


def _band_bias_rows(rel_bias):
    tq = ATTN_Q_TILE
    rb = rel_bias.astype(F32)
    r_max = 3 * tq - 1
    n_far = r_max - MAX_REL
    n_m = r_max + BAND_KEYS
    n_neg = n_m - n_far - rb.shape[1]
    heads = rb.shape[0]
    m = jnp.concatenate([jnp.broadcast_to(rb[:, -1:], (heads, n_far)), rb[:, ::-1],
                         jnp.broadcast_to(rb[:, :1], (heads, n_neg))], axis=1)
    rows = []
    for var in range(3):
        base = r_max - var * tq
        rows.append(jnp.concatenate([m[:, base:base + BAND_KEYS], jnp.zeros((heads, 1), F32),
                                     m[:, base - (tq - 1):base]], axis=1))
    w = jnp.stack(rows, axis=1).reshape(heads // 2, 6, BAND_ROLL)
    return jnp.concatenate([w, jnp.zeros((heads // 2, 2, BAND_ROLL), F32)], axis=1)


def _band_attn(q, k, v, q_norm, k_norm, rel_bias, batch, seq):
    T = batch * seq
    tq = ATTN_Q_TILE
    gains = jnp.stack([jnp.tile(q_norm, 2) * CA_HEAD_DIM ** -0.5, jnp.tile(k_norm, 2)]
                      + [jnp.zeros((LANES,), F32)] * 6)
    kern = functools.partial(_band_attn_kernel, seq=seq, tq=tq)
    slab = pl.BlockSpec((seq, LANES), lambda hp, b: (b, hp))
    return pl.pallas_call(
        kern,
        grid=(CA_HEADS // 2, batch),
        in_specs=[slab, slab, slab, pl.BlockSpec((1, 8, BAND_ROLL), lambda hp, b: (hp, 0, 0)),
                  pl.BlockSpec((8, LANES), lambda hp, b: (0, 0))],
        out_specs=slab,
        out_shape=jax.ShapeDtypeStruct((T, D_MODEL), BF16),
        scratch_shapes=[pltpu.VMEM((2, 3, tq, BAND_KEYS), F32), pltpu.VMEM((seq, LANES), BF16),
                        pltpu.VMEM((seq, LANES), BF16)],
        compiler_params=_params("arbitrary", "arbitrary"),
        name="band_attn",
    )(q, k, v, _band_bias_rows(rel_bias), gains)


def _proj_router_kernel(x_ref, o_ref, wo_ref, gn_ref, rhi_ref, rlo_ref, x1_out, h_out, idx_out, w_out):
    x1 = x_ref[...] + _dot(o_ref[...], wo_ref[...])
    x1_out[...] = x1
    h = _rms(x1, gn_ref[...])
    h_out[...] = h
    h_hi = h.astype(BF16)
    h_lo = (h - h_hi.astype(F32)).astype(BF16)
    logits = _dot(h_hi, rhi_ref[...]) + (_dot(h_lo, rhi_ref[...]) + _dot(h_hi, rlo_ref[...]))

    lane = lax.broadcasted_iota(jnp.int32, logits.shape, 1)
    lane_f = lane.astype(F32)
    lg = jnp.where(lane < N_EXPERTS, logits, NEG_INF)
    m1 = jnp.max(lg, axis=-1, keepdims=True)
    i1 = jnp.min(jnp.where(lg == m1, lane_f, float(LANES)), axis=-1, keepdims=True)
    lg2 = jnp.where(lane_f == i1, NEG_INF, lg)
    m2 = jnp.max(lg2, axis=-1, keepdims=True)
    i2 = jnp.min(jnp.where(lg2 == m2, lane_f, float(LANES)), axis=-1, keepdims=True)
    e = jnp.exp(m2 - m1)
    w1 = 1.0 / (1.0 + e)
    w2 = e / (1.0 + e)
    idx_out[...] = jnp.where(lane == 0, i1, jnp.where(lane == 1, i2, 0.0)).astype(jnp.int32)
    w_out[...] = jnp.where(lane == 0, w1, jnp.where(lane == 1, w2, 0.0))


def _proj_router(x2, o2, w_o, ffn_norm, router):
    T = x2.shape[0]
    tm = ROW_TILE
    r_pad = jnp.zeros((D_MODEL, LANES), F32).at[:, :N_EXPERTS].set(router)
    r_hi = r_pad.astype(BF16)
    r_lo = (r_pad - r_hi.astype(F32)).astype(BF16)
    full = lambda shape: pl.BlockSpec(shape, lambda i: (0,) * len(shape))
    row = pl.BlockSpec((tm, D_MODEL), lambda i: (i, 0))
    small = pl.BlockSpec((tm, LANES), lambda i: (i, 0))
    return pl.pallas_call(
        _proj_router_kernel,
        grid=(T // tm,),
        in_specs=[row, row, full((D_MODEL, D_MODEL)), full((1, D_MODEL)), full((D_MODEL, LANES)),
                  full((D_MODEL, LANES))],
        out_specs=[row, row, small, small],
        out_shape=[jax.ShapeDtypeStruct((T, D_MODEL), F32), jax.ShapeDtypeStruct((T, D_MODEL), F32),
                   jax.ShapeDtypeStruct((T, LANES), jnp.int32), jax.ShapeDtypeStruct((T, LANES), F32)],
        compiler_params=_params("parallel"),
        name="proj_router",
    )(x2, o2, w_o.astype(BF16), ffn_norm[None, :], r_hi, r_lo)


def _moe_plan(top_idx, tm):
    T = top_idx.shape[0]
    n_assign = T * TOP_K
    n_tiles = n_assign // tm + N_EXPERTS
    n_rows = n_tiles * tm
    plane = T + tm
    e_flat = top_idx.reshape(1, n_assign)
    experts = jnp.arange(N_EXPERTS, dtype=jnp.int32)[:, None]
    onehot = (e_flat == experts).astype(jnp.int32)
    csum = jnp.cumsum(onehot, axis=1)
    counts = csum[:, -1]
    tiles_per = (counts + tm - 1) // tm
    tile_end = jnp.cumsum(tiles_per)
    tile_start = tile_end - tiles_per
    dest = jnp.sum(onehot * (csum - 1 + tile_start[:, None] * tm), axis=0)
    a = jnp.arange(n_assign, dtype=jnp.int32)
    slot_real = (a % TOP_K) * plane + a // TOP_K
    s = jnp.arange(tm, dtype=jnp.int32)[None, :]
    pad_row = (tile_start * tm + counts)[:, None] + s
    pad_len = (tiles_per * tm - counts)[:, None]
    pad_key = jnp.where(s < pad_len, pad_row, n_rows + experts * tm + s)
    slot_pad = ((pad_row // tm) % 2) * plane + T + pad_row % tm
    keys = jnp.concatenate([dest, pad_key.reshape(-1)])
    slots = jnp.concatenate([slot_real, slot_pad.reshape(-1)])
    _, row_slot = lax.sort((keys, slots), num_keys=1)

    tile_id = jnp.arange(n_tiles, dtype=jnp.int32)
    tile_valid = (tile_id < tile_end[-1]).astype(jnp.int32)
    tile_expert = jnp.minimum(jnp.sum((tile_id[:, None] >= tile_end[None, :]).astype(jnp.int32), axis=1),
                              N_EXPERTS - 1)
    return row_slot.astype(jnp.int32), tile_expert, tile_valid


def _moe_ffn_kernel(te_ref, tv_ref, slot_ref, h_hbm, wg_ref, wu_ref, wd_ref, y_hbm,
                    x0, x1, y0, y1, hs_ref, gsem, ssem, *, tm, n_tok):
    i = pl.program_id(0)
    f = pl.program_id(1)
    s = lax.rem(i, 2)
    plane = n_tok + tm
    valid = tv_ref[i] > 0
    prev_valid = (i > 0) & (tv_ref[jnp.maximum(i - 1, 0)] > 0)
    xs = (x0, x1)
    ys = (y0, y1)

    def gather_copy(tile, par, j):
        p = slot_ref[tile * tm + j]
        tok = jnp.minimum(jnp.where(p >= plane, p - plane, p), n_tok - 1)
        return pltpu.make_async_copy(h_hbm.at[pl.ds(tok, 1), :], xs[par].at[pl.ds(j, 1), :], gsem.at[par])

    def scatter_copy(tile, par, j):
        p = slot_ref[tile * tm + j]
        return pltpu.make_async_copy(ys[par].at[pl.ds(j, 1), :], y_hbm.at[pl.ds(p, 1), :], ssem.at[par])

    def wait_gather(slot):
        pltpu.make_async_copy(h_hbm.at[pl.ds(0, tm), :], x0, gsem.at[slot]).wait()

    def wait_scatter(slot):
        pltpu.make_async_copy(y0, y_hbm.at[pl.ds(0, tm), :], ssem.at[slot]).wait()

    def partial_ffn():
        h = hs_ref[...]
        a = (_silu(_dot(h, wg_ref[0])) * _dot(h, wu_ref[0])).astype(BF16)
        return _dot(a, wd_ref[0])

    @pl.when((f == 0) & (i == 0))
    def _():
        y1[...] = jnp.zeros((tm, D_MODEL), F32)
        for k in range(TOP_K):
            cp = pltpu.make_async_copy(y1, y_hbm.at[pl.ds(k * plane + n_tok, tm), :], ssem.at[1])
            cp.start()
            cp.wait()

        def first(j, carry):
            gather_copy(0, 0, j).start()
            return carry
        lax.fori_loop(0, tm, first, 0)

    @pl.when((f == 0) & ((i == 0) | prev_valid))
    def _():
        wait_gather(s)

    @pl.when((f == 0) & (i >= 2) & prev_valid)
    def _():
        wait_scatter(s)

    for par in range(2):
        @pl.when(valid & (f == 0) & (s == par))
        def _(par=par):
            hs_ref[...] = xs[par][...].astype(BF16)
            for j in range(tm):
                gather_copy(i + 1, 1 - par, j).start()
            ys[par][...] = partial_ffn()

        @pl.when(valid & (f == 1) & (i > 0) & (s == par))
        def _(par=par):
            for j in range(tm):
                scatter_copy(i - 1, 1 - par, j).start()
            ys[par][...] += partial_ffn()

        @pl.when(jnp.logical_not(valid) & (f == 1) & prev_valid & (s == par))
        def _(par=par):
            def last(j, carry):
                scatter_copy(i - 1, 1 - par, j).start()
                return carry
            lax.fori_loop(0, tm, last, 0)
            wait_scatter(1 - par)

    @pl.when(valid & (f == 1) & (i == 0))
    def _():
        y0[...] += partial_ffn()


def _moe_ffn(h2, row_slot, tile_expert, tile_valid, w_gate, w_up, w_down, tm):
    n_tok = h2.shape[0]
    n_tiles = tile_expert.shape[0]
    ff = w_gate.shape[2]
    tf = MOE_FF_TILE
    nf = ff // tf
    assert nf == 2
    ff_blk = lambda i, f, tv: f * tv[i] + (nf - 1) * (1 - tv[i])
    kern = functools.partial(_moe_ffn_kernel, tm=tm, n_tok=n_tok)
    return pl.pallas_call(
        kern,
        grid_spec=pltpu.PrefetchScalarGridSpec(
            num_scalar_prefetch=3,
            grid=(n_tiles, nf),
            in_specs=[
                pl.BlockSpec(memory_space=pl.ANY),
                pl.BlockSpec((1, D_MODEL, tf), lambda i, f, te, tv, sl: (te[i], 0, ff_blk(i, f, tv))),
                pl.BlockSpec((1, D_MODEL, tf), lambda i, f, te, tv, sl: (te[i], 0, ff_blk(i, f, tv))),
                pl.BlockSpec((1, tf, D_MODEL), lambda i, f, te, tv, sl: (te[i], ff_blk(i, f, tv), 0)),
            ],
            out_specs=pl.BlockSpec(memory_space=pl.ANY),
            scratch_shapes=[pltpu.VMEM((tm, D_MODEL), F32)] * 4 + [
                pltpu.VMEM((tm, D_MODEL), BF16), pltpu.SemaphoreType.DMA((2,)), pltpu.SemaphoreType.DMA((2,))],
        ),
        out_shape=jax.ShapeDtypeStruct((TOP_K * (n_tok + tm), D_MODEL), F32),
        compiler_params=_params("arbitrary", "arbitrary"),
        name="moe_ffn",
    )(tile_expert, tile_valid, row_slot, h2, w_gate.astype(BF16), w_up.astype(BF16), w_down.astype(BF16))


def _moe_sum_kernel(x1_ref, wts_ref, y0_ref, y1_ref, out_ref):
    wts = wts_ref[...]
    out_ref[...] = x1_ref[...] + (wts[:, 0:1] * y0_ref[...] + wts[:, 1:2] * y1_ref[...])


def _moe_sum(x1, wts, y, tm):
    T = x1.shape[0]
    plane_blocks = (T + tm) // tm
    row = pl.BlockSpec((tm, D_MODEL), lambda i: (i, 0))
    return pl.pallas_call(
        _moe_sum_kernel,
        grid=(T // tm,),
        in_specs=[row, pl.BlockSpec((tm, LANES), lambda i: (i, 0)), row,
                  pl.BlockSpec((tm, D_MODEL), lambda i: (plane_blocks + i, 0))],
        out_specs=row,
        out_shape=jax.ShapeDtypeStruct((T, D_MODEL), F32),
        compiler_params=_params("parallel"),
        name="moe_sum",
    )(x1, wts, y, y)


def _mla_dense_layer(x2, pos2, batch, seq, attn_norm, w_in, q_a_norm, kv_a_norm, w_uq, w_ukv, q_norm,
                     k_norm, w_o, ffn_norm, w_gate, w_up, w_down):
    q, k, v = _mla_prep(x2, pos2, attn_norm, w_in, q_a_norm, kv_a_norm, w_uq, w_ukv, q_norm, k_norm)
    o = _mla_attn(q, k, v, batch, seq)
    return _proj_dense_ffn(x2, o, w_o, ffn_norm, w_gate, w_up, w_down)


def _band_moe_layer(x2, batch, seq, attn_norm, w_qkv, q_norm, k_norm, rel_bias, w_o, ffn_norm, router,
                    w_gate, w_up, w_down):
    q, k, v = _ca_prep(x2, attn_norm, w_qkv)
    o = _band_attn(q, k, v, q_norm, k_norm, rel_bias, batch, seq)
    x1, h, idx, wts = _proj_router(x2, o, w_o, ffn_norm, router)
    tm = ROW_TILE
    row_slot, tile_expert, tile_valid = _moe_plan(idx[:, :TOP_K], tm)
    y = _moe_ffn(h, row_slot, tile_expert, tile_valid, w_gate, w_up, w_down, tm)
    return _moe_sum(x1, wts, y, tm)


def kernel(x, positions, l0_attn_norm, l0_mla_w_in, l0_mla_q_a_norm, l0_mla_kv_a_norm, l0_mla_w_uq, l0_mla_w_ukv, l0_mla_q_norm, l0_mla_k_norm, l0_mla_w_o, l0_ffn_norm, l0_ffn_w_gate, l0_ffn_w_up, l0_ffn_w_down, l1_attn_norm, l1_ca_w_qkv, l1_ca_q_norm, l1_ca_k_norm, l1_ca_rel_bias, l1_ca_w_o, l1_ffn_norm, l1_moe_router, l1_moe_w_gate, l1_moe_w_up, l1_moe_w_down, l2_attn_norm, l2_mla_w_in, l2_mla_q_a_norm, l2_mla_kv_a_norm, l2_mla_w_uq, l2_mla_w_ukv, l2_mla_q_norm, l2_mla_k_norm, l2_mla_w_o, l2_ffn_norm, l2_ffn_w_gate, l2_ffn_w_up, l2_ffn_w_down, l3_attn_norm, l3_ca_w_qkv, l3_ca_q_norm, l3_ca_k_norm, l3_ca_rel_bias, l3_ca_w_o, l3_ffn_norm, l3_moe_router, l3_moe_w_gate, l3_moe_w_up, l3_moe_w_down):
    batch, seq, d = x.shape
    assert d == D_MODEL and seq % ATTN_Q_TILE == 0 and seq >= BAND_KEYS
    assert (batch * seq) % ROW_TILE == 0
    x2 = x.reshape(batch * seq, d)
    pos2 = positions.reshape(batch * seq, 1)
    x2 = _mla_dense_layer(x2, pos2, batch, seq, l0_attn_norm, l0_mla_w_in, l0_mla_q_a_norm, l0_mla_kv_a_norm,
                          l0_mla_w_uq, l0_mla_w_ukv, l0_mla_q_norm, l0_mla_k_norm, l0_mla_w_o, l0_ffn_norm,
                          l0_ffn_w_gate, l0_ffn_w_up, l0_ffn_w_down)
    x2 = _band_moe_layer(x2, batch, seq, l1_attn_norm, l1_ca_w_qkv, l1_ca_q_norm, l1_ca_k_norm, l1_ca_rel_bias,
                         l1_ca_w_o, l1_ffn_norm, l1_moe_router, l1_moe_w_gate, l1_moe_w_up, l1_moe_w_down)
    x2 = _mla_dense_layer(x2, pos2, batch, seq, l2_attn_norm, l2_mla_w_in, l2_mla_q_a_norm, l2_mla_kv_a_norm,
                          l2_mla_w_uq, l2_mla_w_ukv, l2_mla_q_norm, l2_mla_k_norm, l2_mla_w_o, l2_ffn_norm,
                          l2_ffn_w_gate, l2_ffn_w_up, l2_ffn_w_down)
    x2 = _band_moe_layer(x2, batch, seq, l3_attn_norm, l3_ca_w_qkv, l3_ca_q_norm, l3_ca_k_norm, l3_ca_rel_bias,
                         l3_ca_w_o, l3_ffn_norm, l3_moe_router, l3_moe_w_gate, l3_moe_w_up, l3_moe_w_down)
    return x2.reshape(batch, seq, d)
```

```python
import functools

import numpy as np
import jax
import jax.numpy as jnp
from jax import lax
from jax.experimental import pallas as pl
from jax.experimental.pallas import tpu as pltpu

F32 = jnp.float32
BF16 = jnp.bfloat16

D_MODEL = 1024
CHUNK = 64
RMS_EPS = 1e-6

MLA_HEADS = 8
MLA_NOPE = 128
MLA_ROPE = 64
MLA_V = 128
MLA_QK = MLA_NOPE + MLA_ROPE
Q_LORA = 256
KV_LORA = 128
ROPE_THETA = 10000.0

CA_HEADS = 16
CA_HEAD_DIM = 64
LEFT_CHUNKS = 8
MAX_REL = 256

N_EXPERTS = 8
TOP_K = 2

LANES = 128
SUBLANES = 8
VMEM_LIMIT = 56 * 1024 * 1024

ROW_TILE = 512
MLA_PREP_TILE = 1024
ROUTER_TILE = 1024
ROUTER_SUBBLOCKS = 4
MLA_Q_TILE = 512
MLA_HEADS_PER_STEP = 2
ATTN_Q_TILE = 256
BAND_KEYS = 3 * ATTN_Q_TILE
BAND_ROLL = ATTN_Q_TILE + BAND_KEYS
MOE_FF_TILE = 1792
NEG_INF = float("-inf")


def _dot(a, b):
    return jnp.dot(a, b, preferred_element_type=F32)


def _dot_nt(a, b):
    return lax.dot_general(a, b, (((1,), (1,)), ((), ())), preferred_element_type=F32)


def _rms(x, g):
    return x * lax.rsqrt(jnp.mean(x * x, axis=-1, keepdims=True) + RMS_EPS) * g


def _silu(g):
    return g / (1.0 + jnp.exp(-g))


def _params(*sem):
    return pltpu.CompilerParams(dimension_semantics=sem, vmem_limit_bytes=VMEM_LIMIT)


def _mla_prep_kernel(x_ref, pos_ref, gn_ref, win_ref, qan_ref, kvan_ref, wuq_ref, wukv_ref,
                     invf_ref, qtab_ref, ktab_ref, q_out, k_out, v_out):
    h = _rms(x_ref[...], gn_ref[...]).astype(BF16)
    a = _dot(h, win_ref[...])
    qn = _rms(a[:, :Q_LORA], qan_ref[...]).astype(BF16)
    kvn = _rms(a[:, Q_LORA:Q_LORA + KV_LORA], kvan_ref[...]).astype(BF16)
    kr2 = a[:, Q_LORA + KV_LORA:]
    q_all = _dot(qn, wuq_ref[...])
    kv_all = _dot(kvn, wukv_ref[...])

    ang = pos_ref[...].astype(F32) * invf_ref[...]
    cos = jnp.cos(ang)
    sin = jnp.sin(ang)
    lane = lax.broadcasted_iota(jnp.int32, (1, LANES), 1)
    rope_lane = lane < MLA_ROPE

    q_gn, q_ga, q_gb = qtab_ref[0:1, :], qtab_ref[1:2, :], qtab_ref[2:3, :]
    k_gn, k_ga, k_gb = ktab_ref[0:1, :], ktab_ref[1:2, :], ktab_ref[2:3, :]
    q_cos, q_sin = cos * q_ga, sin * q_gb
    k_rot = kr2 * (cos * k_ga) + pltpu.roll(kr2, MLA_ROPE, 1) * (sin * k_gb)
    kr_ss = jnp.sum(jnp.where(rope_lane, kr2 * kr2, 0.0), axis=-1, keepdims=True)
    scale = MLA_QK ** -0.5

    for hd in range(MLA_HEADS):
        lo = hd * 2 * LANES
        qa = q_all[:, lo:lo + LANES]
        qb = q_all[:, lo + LANES:lo + 2 * LANES]
        ss = jnp.sum(qa * qa + jnp.where(rope_lane, qb * qb, 0.0), axis=-1, keepdims=True)
        r = lax.rsqrt(ss * (1.0 / MLA_QK) + RMS_EPS) * scale
        q_rot = (qb * q_cos + pltpu.roll(qb, MLA_ROPE, 1) * q_sin) * r
        q_out[hd, :, :MLA_NOPE] = (qa * q_gn * r).astype(BF16)
        q_out[hd, :, MLA_NOPE:] = q_rot[:, :MLA_ROPE].astype(BF16)

        kn = kv_all[:, lo:lo + LANES]
        ssk = jnp.sum(kn * kn, axis=-1, keepdims=True) + kr_ss
        rk = lax.rsqrt(ssk * (1.0 / MLA_QK) + RMS_EPS)
        k_out[hd, :, :MLA_NOPE] = (kn * k_gn * rk).astype(BF16)
        k_out[hd, :, MLA_NOPE:] = (k_rot * rk)[:, :MLA_ROPE].astype(BF16)
        v_out[hd] = kv_all[:, lo + LANES:lo + 2 * LANES].astype(BF16)


def _rope_partner():
    d = np.arange(MLA_ROPE)
    half = MLA_ROPE // 2
    partner = np.where(d < half, d + half, d - half)
    sign = np.where(d < half, -1.0, 1.0).astype(np.float32)
    return partner, sign


def _mla_tables(g):
    partner, sign = _rope_partner()
    zeros = jnp.zeros((LANES - MLA_ROPE,), F32)
    g_rope = g[MLA_NOPE:]
    rows = [g[:MLA_NOPE],
            jnp.concatenate([g_rope, zeros]),
            jnp.concatenate([g_rope[partner] * sign, zeros])]
    rows += [jnp.zeros((LANES,), F32)] * 5
    return jnp.stack(rows)


def _mla_prep(x2, pos2, attn_norm, w_in, q_a_norm, kv_a_norm, w_uq, w_ukv, q_norm, k_norm):
    T = x2.shape[0]
    tm = MLA_PREP_TILE
    partner, _ = _rope_partner()
    rope0 = Q_LORA + KV_LORA
    w_in_ext = jnp.concatenate([w_in, w_in[:, rope0 + partner]], axis=1).astype(BF16)
    wq = w_uq.reshape(Q_LORA, MLA_HEADS, MLA_QK)
    w_uq_ext = jnp.concatenate([wq, wq[:, :, MLA_NOPE + partner]], axis=2)
    w_uq_ext = w_uq_ext.reshape(Q_LORA, MLA_HEADS * 2 * LANES).astype(BF16)
    w_ukv_b = w_ukv.astype(BF16)
    inv = 1.0 / (ROPE_THETA ** (jnp.arange(0, MLA_ROPE, 2, dtype=F32) / MLA_ROPE))
    invf = jnp.concatenate([inv, inv, jnp.zeros((LANES - MLA_ROPE,), F32)])[None, :]

    full = lambda shape: pl.BlockSpec(shape, lambda i: (0,) * len(shape))
    out_qk = jax.ShapeDtypeStruct((MLA_HEADS, T, MLA_QK), BF16)
    out_v = jax.ShapeDtypeStruct((MLA_HEADS, T, MLA_V), BF16)
    return pl.pallas_call(
        _mla_prep_kernel,
        grid=(T // tm,),
        in_specs=[
            pl.BlockSpec((tm, D_MODEL), lambda i: (i, 0)),
            pl.BlockSpec((tm, 1), lambda i: (i, 0)),
            full((1, D_MODEL)),
            full((D_MODEL, 4 * LANES)),
            full((1, Q_LORA)),
            full((1, KV_LORA)),
            full((Q_LORA, MLA_HEADS * 2 * LANES)),
            full((KV_LORA, MLA_HEADS * 2 * LANES)),
            full((1, LANES)),
            full((8, LANES)),
            full((8, LANES)),
        ],
        out_specs=[
            pl.BlockSpec((MLA_HEADS, tm, MLA_QK), lambda i: (0, i, 0)),
            pl.BlockSpec((MLA_HEADS, tm, MLA_QK), lambda i: (0, i, 0)),
            pl.BlockSpec((MLA_HEADS, tm, MLA_V), lambda i: (0, i, 0)),
        ],
        out_shape=[out_qk, out_qk, out_v],
        compiler_params=_params("parallel"),
        name="mla_prep",
    )(x2, pos2, attn_norm[None, :], w_in_ext, q_a_norm[None, :], kv_a_norm[None, :],
      w_uq_ext, w_ukv_b, invf, _mla_tables(q_norm), _mla_tables(k_norm))


def _mla_attn_kernel(q_ref, k_ref, v_ref, o_ref, *, seq, tq):
    row_c = lax.broadcasted_iota(jnp.int32, (tq, tq), 0) // CHUNK
    col_c = lax.broadcasted_iota(jnp.int32, (tq, tq), 1) // CHUNK
    diag_mask = col_c <= row_c
    for qi in range(seq // tq):
        lo, hi = qi * tq, (qi + 1) * tq
        for hh in range(MLA_HEADS_PER_STEP):
            q = q_ref[hh, lo:hi, :]
            s_d = jnp.where(diag_mask, _dot_nt(q, k_ref[hh, lo:hi, :]), NEG_INF)
            m = jnp.max(s_d, axis=-1, keepdims=True)
            if qi > 0:
                s_o = _dot_nt(q, k_ref[hh, :lo, :])
                m = jnp.maximum(m, jnp.max(s_o, axis=-1, keepdims=True))
            p_d = jnp.exp(s_d - m)
            l = jnp.sum(p_d, axis=-1, keepdims=True)
            o = _dot(p_d.astype(BF16), v_ref[hh, lo:hi, :])
            if qi > 0:
                p_o = jnp.exp(s_o - m)
                l = l + jnp.sum(p_o, axis=-1, keepdims=True)
                o = o + _dot(p_o.astype(BF16), v_ref[hh, :lo, :])
            o_ref[lo:hi, hh * MLA_V:(hh + 1) * MLA_V] = (o / l).astype(BF16)


def _mla_attn(q, k, v, batch, seq):
    T = batch * seq
    hps = MLA_HEADS_PER_STEP
    kern = functools.partial(_mla_attn_kernel, seq=seq, tq=MLA_Q_TILE)
    return pl.pallas_call(
        kern,
        grid=(MLA_HEADS // hps, batch),
        in_specs=[
            pl.BlockSpec((hps, seq, MLA_QK), lambda h, b: (h, b, 0)),
            pl.BlockSpec((hps, seq, MLA_QK), lambda h, b: (h, b, 0)),
            pl.BlockSpec((hps, seq, MLA_V), lambda h, b: (h, b, 0)),
        ],
        out_specs=pl.BlockSpec((seq, hps * MLA_V), lambda h, b: (b, h)),
        out_shape=jax.ShapeDtypeStruct((T, MLA_HEADS * MLA_V), BF16),
        compiler_params=_params("parallel", "parallel"),
        name="mla_attn",
    )(q, k, v)


def _proj_dense_ffn_kernel(x_ref, o_ref, wo_ref, gn_ref, wg_ref, wu_ref, wd_ref, out_ref):
    x1 = x_ref[...] + _dot(o_ref[...], wo_ref[...])
    h = _rms(x1, gn_ref[...]).astype(BF16)
    a = (_silu(_dot(h, wg_ref[...])) * _dot(h, wu_ref[...])).astype(BF16)
    out_ref[...] = x1 + _dot(a, wd_ref[...])


def _proj_dense_ffn(x2, o2, w_o, ffn_norm, w_gate, w_up, w_down):
    T = x2.shape[0]
    ff = w_gate.shape[1]
    tm = ROW_TILE
    const = lambda shape: pl.BlockSpec(shape, lambda i: (0, 0), pipeline_mode=pl.Buffered(1))
    row = pl.BlockSpec((tm, D_MODEL), lambda i: (i, 0))
    return pl.pallas_call(
        _proj_dense_ffn_kernel,
        grid=(T // tm,),
        in_specs=[row, row, const((D_MODEL, D_MODEL)), const((1, D_MODEL)), const((D_MODEL, ff)),
                  const((D_MODEL, ff)), const((ff, D_MODEL))],
        out_specs=row,
        out_shape=jax.ShapeDtypeStruct((T, D_MODEL), F32),
        compiler_params=_params("parallel"),
        name="proj_dense_ffn",
    )(x2, o2, w_o.astype(BF16), ffn_norm[None, :], w_gate.astype(BF16), w_up.astype(BF16),
      w_down.astype(BF16))


def _ca_prep_kernel(x_ref, gn_ref, w_ref, q_out, k_out, v_out):
    h = _rms(x_ref[...], gn_ref[...]).astype(BF16)
    qkv = _dot(h, w_ref[...])
    q_out[...] = qkv[:, :D_MODEL].astype(BF16)
    k_out[...] = qkv[:, D_MODEL:2 * D_MODEL].astype(BF16)
    v_out[...] = qkv[:, 2 * D_MODEL:].astype(BF16)


def _ca_prep(x2, attn_norm, w_qkv):
    T = x2.shape[0]
    tm = ROW_TILE
    full = lambda shape: pl.BlockSpec(shape, lambda i: (0,) * len(shape))
    out = jax.ShapeDtypeStruct((T, D_MODEL), BF16)
    row = pl.BlockSpec((tm, D_MODEL), lambda i: (i, 0))
    return pl.pallas_call(
        _ca_prep_kernel,
        grid=(T // tm,),
        in_specs=[row, full((1, D_MODEL)), full((D_MODEL, 3 * D_MODEL))],
        out_specs=[row, row, row],
        out_shape=[out, out, out],
        compiler_params=_params("parallel"),
        name="ca_prep",
    )(x2, attn_norm[None, :], w_qkv.astype(BF16))


def _band_attn_kernel(q_ref, k_ref, v_ref, w_ref, g_ref, wg_ref, wu_ref, wd_ref,
                      o_ref, wg_out, wu_out, wd_out, tab_ref, qn_ref, kn_ref, *, seq, tq):
    wg_out[...] = wg_ref[...].astype(BF16)
    wu_out[...] = wu_ref[...].astype(BF16)
    wd_out[...] = wd_ref[...].astype(BF16)

    @pl.when(pl.program_id(1) == 0)
    def _():
        row = lax.broadcasted_iota(jnp.int32, (tq, BAND_KEYS), 0)
        col = lax.broadcasted_iota(jnp.int32, (tq, BAND_KEYS), 1)
        shift = CHUNK.bit_length() - 1
        for hh in range(2):
            for var in range(3):
                w = w_ref[0, 3 * hh + var:3 * hh + var + 1, :]
                t = pltpu.roll(jnp.broadcast_to(w, (tq, BAND_ROLL)), 0, 1, stride=1, stride_axis=0)
                gap = lax.shift_right_logical(row + var * tq, shift) - lax.shift_right_logical(col, shift)
                tab_ref[hh, var] = jnp.where((gap >= 0) & (gap <= LEFT_CHUNKS), t[:, :BAND_KEYS], NEG_INF)

    lane = lax.broadcasted_iota(jnp.int32, (1, LANES), 1)
    head0 = lane < CA_HEAD_DIM

    def head_normed(x_ref, gain):
        x = x_ref[...].astype(F32)
        sq = x * x
        ss0 = jnp.sum(jnp.where(head0, sq, 0.0), axis=-1, keepdims=True)
        ss1 = jnp.sum(jnp.where(head0, 0.0, sq), axis=-1, keepdims=True)
        r = jnp.where(head0, lax.rsqrt(ss0 * (1.0 / CA_HEAD_DIM) + RMS_EPS),
                      lax.rsqrt(ss1 * (1.0 / CA_HEAD_DIM) + RMS_EPS))
        return (x * r * gain).astype(BF16)

    qn_ref[...] = head_normed(q_ref, g_ref[0:1, :])
    kn_ref[...] = head_normed(k_ref, g_ref[1:2, :])

    for qi in range(seq // tq):
        var = min(qi, 2)
        ks = max(qi - 2, 0) * tq
        q = qn_ref[qi * tq:(qi + 1) * tq, :]
        k = kn_ref[ks:ks + BAND_KEYS, :]
        v = v_ref[ks:ks + BAND_KEYS, :]
        zero = jnp.zeros_like(q)
        qq = jnp.concatenate([jnp.where(head0, q, zero), jnp.where(head0, zero, q)], axis=0)
        s = _dot_nt(qq, k)
        ps, ls = [], []
        for hh in range(2):
            sh = s[hh * tq:(hh + 1) * tq] + tab_ref[hh, var]
            p = jnp.exp(sh - jnp.max(sh, axis=-1, keepdims=True))
            ls.append(jnp.sum(p, axis=-1, keepdims=True))
            ps.append(p.astype(BF16))
        oo = _dot(jnp.concatenate(ps, axis=0), v)
        o = jnp.where(head0, oo[:tq] / ls[0], oo[tq:] / ls[1])
        o_ref[qi * tq:(qi + 1) * tq, :] = o.astype(BF16)


def _band_bias_rows(rel_bias):
    tq = ATTN_Q_TILE
    rb = rel_bias.astype(F32)
    r_max = 3 * tq - 1
    n_far = r_max - MAX_REL
    n_m = r_max + BAND_KEYS
    n_neg = n_m - n_far - rb.shape[1]
    heads = rb.shape[0]
    m = jnp.concatenate([jnp.broadcast_to(rb[:, -1:], (heads, n_far)), rb[:, ::-1],
                         jnp.broadcast_to(rb[:, :1], (heads, n_neg))], axis=1)
    rows = []
    for var in range(3):
        base = r_max - var * tq
        rows.append(jnp.concatenate([m[:, base:base + BAND_KEYS], jnp.zeros((heads, 1), F32),
                                     m[:, base - (tq - 1):base]], axis=1))
    w = jnp.stack(rows, axis=1).reshape(heads // 2, 6, BAND_ROLL)
    return jnp.concatenate([w, jnp.zeros((heads // 2, 2, BAND_ROLL), F32)], axis=1)


def _band_attn(q, k, v, q_norm, k_norm, rel_bias, batch, seq, expert_weights):
    T = batch * seq
    tq = ATTN_Q_TILE
    n_steps = (CA_HEADS // 2) * batch
    gains = jnp.stack([jnp.tile(q_norm, 2) * CA_HEAD_DIM ** -0.5, jnp.tile(k_norm, 2)]
                      + [jnp.zeros((LANES,), F32)] * 6)
    kern = functools.partial(_band_attn_kernel, seq=seq, tq=tq)
    slab = pl.BlockSpec((seq, LANES), lambda hp, b: (b, hp))
    sliced = [w.reshape(n_steps, w.shape[0] * w.shape[1] // n_steps, w.shape[2]) for w in expert_weights]
    w_specs = [pl.BlockSpec((1,) + w.shape[1:], lambda hp, b: (hp * batch + b, 0, 0)) for w in sliced]
    outs = pl.pallas_call(
        kern,
        grid=(CA_HEADS // 2, batch),
        in_specs=[slab, slab, slab, pl.BlockSpec((1, 8, BAND_ROLL), lambda hp, b: (hp, 0, 0)),
                  pl.BlockSpec((8, LANES), lambda hp, b: (0, 0))] + w_specs,
        out_specs=[slab] + w_specs,
        out_shape=[jax.ShapeDtypeStruct((T, D_MODEL), BF16)]
                  + [jax.ShapeDtypeStruct(w.shape, BF16) for w in sliced],
        scratch_shapes=[pltpu.VMEM((2, 3, tq, BAND_KEYS), F32), pltpu.VMEM((seq, LANES), BF16),
                        pltpu.VMEM((seq, LANES), BF16)],
        compiler_params=_params("arbitrary", "arbitrary"),
        name="band_attn",
    )(q, k, v, _band_bias_rows(rel_bias), gains, *sliced)
    return outs[0], [wb.reshape(w.shape) for wb, w in zip(outs[1:], expert_weights)]


def _proj_router_kernel(x_ref, o_ref, wo_ref, gn_ref, rhi_ref, rlo_ref, x1_out, h_out, idx_out, w_out):
    rows = x_ref.shape[0] // ROUTER_SUBBLOCKS
    for blk in range(ROUTER_SUBBLOCKS):
        sl = slice(blk * rows, (blk + 1) * rows)
        x1 = x_ref[sl, :] + _dot(o_ref[sl, :], wo_ref[...])
        x1_out[sl, :] = x1
        h = _rms(x1, gn_ref[...])
        h_out[sl, :] = h
        h_hi = h.astype(BF16)
        h_lo = (h - h_hi.astype(F32)).astype(BF16)
        logits = _dot(h_hi, rhi_ref[...]) + (_dot(h_lo, rhi_ref[...]) + _dot(h_hi, rlo_ref[...]))

        lane = lax.broadcasted_iota(jnp.int32, logits.shape, 1)
        lane_f = lane.astype(F32)
        lg = jnp.where(lane < N_EXPERTS, logits, NEG_INF)
        m1 = jnp.max(lg, axis=-1, keepdims=True)
        i1 = jnp.min(jnp.where(lg == m1, lane_f, float(LANES)), axis=-1, keepdims=True)
        lg2 = jnp.where(lane_f == i1, NEG_INF, lg)
        m2 = jnp.max(lg2, axis=-1, keepdims=True)
        i2 = jnp.min(jnp.where(lg2 == m2, lane_f, float(LANES)), axis=-1, keepdims=True)
        e = jnp.exp(m2 - m1)
        w1 = 1.0 / (1.0 + e)
        w2 = e / (1.0 + e)
        idx_out[sl, :] = jnp.where(lane == 0, i1, jnp.where(lane == 1, i2, 0.0)).astype(jnp.int32)
        w_out[sl, :] = jnp.where(lane == 0, w1, jnp.where(lane == 1, w2, 0.0))


def _proj_router(x2, o2, w_o, ffn_norm, router):
    T = x2.shape[0]
    tm = ROUTER_TILE
    r_pad = jnp.zeros((D_MODEL, LANES), F32).at[:, :N_EXPERTS].set(router)
    r_hi = r_pad.astype(BF16)
    r_lo = (r_pad - r_hi.astype(F32)).astype(BF16)
    full = lambda shape: pl.BlockSpec(shape, lambda i: (0,) * len(shape))
    row = pl.BlockSpec((tm, D_MODEL), lambda i: (i, 0))
    small = pl.BlockSpec((tm, LANES), lambda i: (i, 0))
    return pl.pallas_call(
        _proj_router_kernel,
        grid=(T // tm,),
        in_specs=[row, row, full((D_MODEL, D_MODEL)), full((1, D_MODEL)), full((D_MODEL, LANES)),
                  full((D_MODEL, LANES))],
        out_specs=[row, row, small, small],
        out_shape=[jax.ShapeDtypeStruct((T, D_MODEL), F32), jax.ShapeDtypeStruct((T, D_MODEL), F32),
                   jax.ShapeDtypeStruct((T, LANES), jnp.int32), jax.ShapeDtypeStruct((T, LANES), F32)],
        compiler_params=_params("parallel"),
        name="proj_router",
    )(x2, o2, w_o.astype(BF16), ffn_norm[None, :], r_hi, r_lo)


def _moe_plan(top_idx, tm):
    T = top_idx.shape[0]
    n_assign = T * TOP_K
    n_tiles = n_assign // tm + N_EXPERTS
    n_rows = n_tiles * tm
    plane = T + tm
    e_flat = top_idx.reshape(1, n_assign)
    experts = jnp.arange(N_EXPERTS, dtype=jnp.int32)[:, None]
    onehot = (e_flat == experts).astype(jnp.int32)
    csum = jnp.cumsum(onehot, axis=1)
    counts = csum[:, -1]
    tiles_per = (counts + tm - 1) // tm
    tile_end = jnp.cumsum(tiles_per)
    tile_start = tile_end - tiles_per
    dest = jnp.sum(onehot * (csum - 1 + tile_start[:, None] * tm), axis=0)
    a = jnp.arange(n_assign, dtype=jnp.int32)
    slot_real = (a % TOP_K) * plane + a // TOP_K
    s = jnp.arange(tm, dtype=jnp.int32)[None, :]
    pad_row = (tile_start * tm + counts)[:, None] + s
    pad_len = (tiles_per * tm - counts)[:, None]
    pad_key = jnp.where(s < pad_len, pad_row, n_rows + experts * tm + s)
    slot_pad = ((pad_row // tm) % 2) * plane + T + pad_row % tm
    keys = jnp.concatenate([dest, pad_key.reshape(-1)])
    slots = jnp.concatenate([slot_real, slot_pad.reshape(-1)])
    _, row_slot = lax.sort((keys, slots), num_keys=1)

    tile_id = jnp.arange(n_tiles, dtype=jnp.int32)
    tile_valid = (tile_id < tile_end[-1]).astype(jnp.int32)
    tile_expert = jnp.minimum(jnp.sum((tile_id[:, None] >= tile_end[None, :]).astype(jnp.int32), axis=1),
                              N_EXPERTS - 1)
    return row_slot.astype(jnp.int32), tile_expert, tile_valid


def _moe_ffn_kernel(te_ref, tv_ref, slot_ref, h_hbm, wg_ref, wu_ref, wd_ref, y_hbm,
                    x0, x1, y0, y1, hs_ref, gsem, ssem, *, tm, n_tok):
    i = pl.program_id(0)
    f = pl.program_id(1)
    s = lax.rem(i, 2)
    plane = n_tok + tm
    valid = tv_ref[i] > 0
    prev_valid = (i > 0) & (tv_ref[jnp.maximum(i - 1, 0)] > 0)
    xs = (x0, x1)
    ys = (y0, y1)

    def gather_copy(tile, par, j):
        p = slot_ref[tile * tm + j]
        tok = jnp.minimum(jnp.where(p >= plane, p - plane, p), n_tok - 1)
        return pltpu.make_async_copy(h_hbm.at[pl.ds(tok, 1), :], xs[par].at[pl.ds(j, 1), :], gsem.at[par])

    def scatter_copy(tile, par, j):
        p = slot_ref[tile * tm + j]
        return pltpu.make_async_copy(ys[par].at[pl.ds(j, 1), :], y_hbm.at[pl.ds(p, 1), :], ssem.at[par])

    def wait_gather(slot):
        pltpu.make_async_copy(h_hbm.at[pl.ds(0, tm), :], x0, gsem.at[slot]).wait()

    def wait_scatter(slot):
        pltpu.make_async_copy(y0, y_hbm.at[pl.ds(0, tm), :], ssem.at[slot]).wait()

    def partial_ffn():
        h = hs_ref[...]
        a = (_silu(_dot(h, wg_ref[0])) * _dot(h, wu_ref[0])).astype(BF16)
        return _dot(a, wd_ref[0])

    @pl.when((f == 0) & (i == 0))
    def _():
        y1[...] = jnp.zeros((tm, D_MODEL), F32)
        for k in range(TOP_K):
            cp = pltpu.make_async_copy(y1, y_hbm.at[pl.ds(k * plane + n_tok, tm), :], ssem.at[1])
            cp.start()
            cp.wait()

        def first(j, carry):
            gather_copy(0, 0, j).start()
            return carry
        lax.fori_loop(0, tm, first, 0)

    @pl.when((f == 0) & ((i == 0) | prev_valid))
    def _():
        wait_gather(s)

    @pl.when((f == 0) & (i >= 2) & prev_valid)
    def _():
        wait_scatter(s)

    for par in range(2):
        @pl.when(valid & (f == 0) & (s == par))
        def _(par=par):
            hs_ref[...] = xs[par][...].astype(BF16)
            for j in range(tm):
                gather_copy(i + 1, 1 - par, j).start()
            ys[par][...] = partial_ffn()

        @pl.when(valid & (f == 1) & (i > 0) & (s == par))
        def _(par=par):
            for j in range(tm):
                scatter_copy(i - 1, 1 - par, j).start()
            ys[par][...] += partial_ffn()

        @pl.when(jnp.logical_not(valid) & (f == 1) & prev_valid & (s == par))
        def _(par=par):
            def last(j, carry):
                scatter_copy(i - 1, 1 - par, j).start()
                return carry
            lax.fori_loop(0, tm, last, 0)
            wait_scatter(1 - par)

    @pl.when(valid & (f == 1) & (i == 0))
    def _():
        y0[...] += partial_ffn()


def _moe_ffn(h2, row_slot, tile_expert, tile_valid, w_gate, w_up, w_down, tm):
    n_tok = h2.shape[0]
    n_tiles = tile_expert.shape[0]
    ff = w_gate.shape[2]
    tf = MOE_FF_TILE
    nf = ff // tf
    assert nf == 2
    ff_blk = lambda i, f, tv: f * tv[i] + (nf - 1) * (1 - tv[i])
    kern = functools.partial(_moe_ffn_kernel, tm=tm, n_tok=n_tok)
    return pl.pallas_call(
        kern,
        grid_spec=pltpu.PrefetchScalarGridSpec(
            num_scalar_prefetch=3,
            grid=(n_tiles, nf),
            in_specs=[
                pl.BlockSpec(memory_space=pl.ANY),
                pl.BlockSpec((1, D_MODEL, tf), lambda i, f, te, tv, sl: (te[i], 0, ff_blk(i, f, tv))),
                pl.BlockSpec((1, D_MODEL, tf), lambda i, f, te, tv, sl: (te[i], 0, ff_blk(i, f, tv))),
                pl.BlockSpec((1, tf, D_MODEL), lambda i, f, te, tv, sl: (te[i], ff_blk(i, f, tv), 0)),
            ],
            out_specs=pl.BlockSpec(memory_space=pl.ANY),
            scratch_shapes=[pltpu.VMEM((tm, D_MODEL), F32)] * 4 + [
                pltpu.VMEM((tm, D_MODEL), BF16), pltpu.SemaphoreType.DMA((2,)), pltpu.SemaphoreType.DMA((2,))],
        ),
        out_shape=jax.ShapeDtypeStruct((TOP_K * (n_tok + tm), D_MODEL), F32),
        compiler_params=_params("arbitrary", "arbitrary"),
        name="moe_ffn",
    )(tile_expert, tile_valid, row_slot, h2, w_gate.astype(BF16), w_up.astype(BF16), w_down.astype(BF16))


def _moe_sum_kernel(x1_ref, wts_ref, y0_ref, y1_ref, out_ref):
    wts = wts_ref[...]
    out_ref[...] = x1_ref[...] + (wts[:, 0:1] * y0_ref[...] + wts[:, 1:2] * y1_ref[...])


def _moe_sum(x1, wts, y, tm):
    T = x1.shape[0]
    plane_blocks = (T + tm) // tm
    row = pl.BlockSpec((tm, D_MODEL), lambda i: (i, 0))
    return pl.pallas_call(
        _moe_sum_kernel,
        grid=(T // tm,),
        in_specs=[row, pl.BlockSpec((tm, LANES), lambda i: (i, 0)), row,
                  pl.BlockSpec((tm, D_MODEL), lambda i: (plane_blocks + i, 0))],
        out_specs=row,
        out_shape=jax.ShapeDtypeStruct((T, D_MODEL), F32),
        compiler_params=_params("parallel"),
        name="moe_sum",
    )(x1, wts, y, y)


def _mla_dense_layer(x2, pos2, batch, seq, attn_norm, w_in, q_a_norm, kv_a_norm, w_uq, w_ukv, q_norm,
                     k_norm, w_o, ffn_norm, w_gate, w_up, w_down):
    q, k, v = _mla_prep(x2, pos2, attn_norm, w_in, q_a_norm, kv_a_norm, w_uq, w_ukv, q_norm, k_norm)
    o = _mla_attn(q, k, v, batch, seq)
    return _proj_dense_ffn(x2, o, w_o, ffn_norm, w_gate, w_up, w_down)


def _band_moe_layer(x2, batch, seq, attn_norm, w_qkv, q_norm, k_norm, rel_bias, w_o, ffn_norm, router,
                    w_gate, w_up, w_down):
    q, k, v = _ca_prep(x2, attn_norm, w_qkv)
    o, (w_gate, w_up, w_down) = _band_attn(q, k, v, q_norm, k_norm, rel_bias, batch, seq,
                                           (w_gate, w_up, w_down))
    x1, h, idx, wts = _proj_router(x2, o, w_o, ffn_norm, router)
    tm = ROW_TILE
    row_slot, tile_expert, tile_valid = _moe_plan(idx[:, :TOP_K], tm)
    y = _moe_ffn(h, row_slot, tile_expert, tile_valid, w_gate, w_up, w_down, tm)
    return _moe_sum(x1, wts, y, tm)


def kernel(x, positions, l0_attn_norm, l0_mla_w_in, l0_mla_q_a_norm, l0_mla_kv_a_norm, l0_mla_w_uq, l0_mla_w_ukv, l0_mla_q_norm, l0_mla_k_norm, l0_mla_w_o, l0_ffn_norm, l0_ffn_w_gate, l0_ffn_w_up, l0_ffn_w_down, l1_attn_norm, l1_ca_w_qkv, l1_ca_q_norm, l1_ca_k_norm, l1_ca_rel_bias, l1_ca_w_o, l1_ffn_norm, l1_moe_router, l1_moe_w_gate, l1_moe_w_up, l1_moe_w_down, l2_attn_norm, l2_mla_w_in, l2_mla_q_a_norm, l2_mla_kv_a_norm, l2_mla_w_uq, l2_mla_w_ukv, l2_mla_q_norm, l2_mla_k_norm, l2_mla_w_o, l2_ffn_norm, l2_ffn_w_gate, l2_ffn_w_up, l2_ffn_w_down, l3_attn_norm, l3_ca_w_qkv, l3_ca_q_norm, l3_ca_k_norm, l3_ca_rel_bias, l3_ca_w_o, l3_ffn_norm, l3_moe_router, l3_moe_w_gate, l3_moe_w_up, l3_moe_w_down):
    batch, seq, d = x.shape
    assert d == D_MODEL and seq % ATTN_Q_TILE == 0 and seq >= BAND_KEYS
    assert (batch * seq) % ROW_TILE == 0
    x2 = x.reshape(batch * seq, d)
    pos2 = positions.reshape(batch * seq, 1)
    x2 = _mla_dense_layer(x2, pos2, batch, seq, l0_attn_norm, l0_mla_w_in, l0_mla_q_a_norm, l0_mla_kv_a_norm,
                          l0_mla_w_uq, l0_mla_w_ukv, l0_mla_q_norm, l0_mla_k_norm, l0_mla_w_o, l0_ffn_norm,
                          l0_ffn_w_gate, l0_ffn_w_up, l0_ffn_w_down)
    x2 = _band_moe_layer(x2, batch, seq, l1_attn_norm, l1_ca_w_qkv, l1_ca_q_norm, l1_ca_k_norm, l1_ca_rel_bias,
                         l1_ca_w_o, l1_ffn_norm, l1_moe_router, l1_moe_w_gate, l1_moe_w_up, l1_moe_w_down)
    x2 = _mla_dense_layer(x2, pos2, batch, seq, l2_attn_norm, l2_mla_w_in, l2_mla_q_a_norm, l2_mla_kv_a_norm,
                          l2_mla_w_uq, l2_mla_w_ukv, l2_mla_q_norm, l2_mla_k_norm, l2_mla_w_o, l2_ffn_norm,
                          l2_ffn_w_gate, l2_ffn_w_up, l2_ffn_w_down)
    x2 = _band_moe_layer(x2, batch, seq, l3_attn_norm, l3_ca_w_qkv, l3_ca_q_norm, l3_ca_k_norm, l3_ca_rel_bias,
                         l3_ca_w_o, l3_ffn_norm, l3_moe_router, l3_moe_w_gate, l3_moe_w_up, l3_moe_w_down)
    return x2.reshape(batch, seq, d)
```

```python
import functools

import numpy as np
import jax
import jax.numpy as jnp
from jax import lax
from jax.experimental import pallas as pl
from jax.experimental.pallas import tpu as pltpu

F32 = jnp.float32
BF16 = jnp.bfloat16

D_MODEL = 1024
CHUNK = 64
RMS_EPS = 1e-6

MLA_HEADS = 8
MLA_NOPE = 128
MLA_ROPE = 64
MLA_V = 128
MLA_QK = MLA_NOPE + MLA_ROPE
Q_LORA = 256
KV_LORA = 128
ROPE_THETA = 10000.0

CA_HEADS = 16
CA_HEAD_DIM = 64
LEFT_CHUNKS = 8
MAX_REL = 256

N_EXPERTS = 8
TOP_K = 2

LANES = 128
SUBLANES = 8
VMEM_LIMIT = 56 * 1024 * 1024

ROW_TILE = 512
MLA_PREP_TILE = 1024
ROUTER_TILE = 1024
ROUTER_SUBBLOCKS = 4
MLA_Q_TILE = 512
MLA_HEADS_PER_STEP = 2
ATTN_Q_TILE = 256
BAND_KEYS = 3 * ATTN_Q_TILE
BAND_ROLL = ATTN_Q_TILE + BAND_KEYS
MOE_FF_TILE = 1792
NEG_INF = float("-inf")


def _dot(a, b):
    return jnp.dot(a, b, preferred_element_type=F32)


def _dot_nt(a, b):
    return lax.dot_general(a, b, (((1,), (1,)), ((), ())), preferred_element_type=F32)


def _rms(x, g):
    return x * lax.rsqrt(jnp.mean(x * x, axis=-1, keepdims=True) + RMS_EPS) * g


def _silu(g):
    return g / (1.0 + jnp.exp(-g))


def _params(*sem):
    return pltpu.CompilerParams(dimension_semantics=sem, vmem_limit_bytes=VMEM_LIMIT)


def _mla_prep_kernel(x_ref, pos_ref, gn_ref, win_ref, qan_ref, kvan_ref, wuq_ref, wukv_ref,
                     invf_ref, qtab_ref, ktab_ref, q_out, k_out, v_out):
    h = _rms(x_ref[...], gn_ref[...]).astype(BF16)
    a = _dot(h, win_ref[...])
    qn = _rms(a[:, :Q_LORA], qan_ref[...]).astype(BF16)
    kvn = _rms(a[:, Q_LORA:Q_LORA + KV_LORA], kvan_ref[...]).astype(BF16)
    kr2 = a[:, Q_LORA + KV_LORA:]
    q_all = _dot(qn, wuq_ref[...])
    kv_all = _dot(kvn, wukv_ref[...])

    ang = pos_ref[...].astype(F32) * invf_ref[...]
    cos = jnp.cos(ang)
    sin = jnp.sin(ang)
    lane = lax.broadcasted_iota(jnp.int32, (1, LANES), 1)
    rope_lane = lane < MLA_ROPE

    q_gn, q_ga, q_gb = qtab_ref[0:1, :], qtab_ref[1:2, :], qtab_ref[2:3, :]
    k_gn, k_ga, k_gb = ktab_ref[0:1, :], ktab_ref[1:2, :], ktab_ref[2:3, :]
    q_cos, q_sin = cos * q_ga, sin * q_gb
    k_rot = kr2 * (cos * k_ga) + pltpu.roll(kr2, MLA_ROPE, 1) * (sin * k_gb)
    kr_ss = jnp.sum(jnp.where(rope_lane, kr2 * kr2, 0.0), axis=-1, keepdims=True)
    scale = MLA_QK ** -0.5

    slabs = []
    for hd in range(MLA_HEADS):
        lo = hd * 2 * LANES
        qa = q_all[:, lo:lo + LANES]
        qb = q_all[:, lo + LANES:lo + 2 * LANES]
        kn = kv_all[:, lo:lo + LANES]
        ss = jnp.sum(qa * qa + jnp.where(rope_lane, qb * qb, 0.0), axis=-1, keepdims=True)
        ssk = jnp.sum(kn * kn, axis=-1, keepdims=True) + kr_ss
        slabs.append((qa, qb, kn, ss, ssk))
    for hd, (qa, qb, kn, ss, ssk) in enumerate(slabs):
        lo = hd * 2 * LANES
        r = lax.rsqrt(ss * (1.0 / MLA_QK) + RMS_EPS) * scale
        rk = lax.rsqrt(ssk * (1.0 / MLA_QK) + RMS_EPS)
        q_rot = (qb * q_cos + pltpu.roll(qb, MLA_ROPE, 1) * q_sin) * r
        q_out[hd, :, :MLA_NOPE] = (qa * q_gn * r).astype(BF16)
        q_out[hd, :, MLA_NOPE:] = q_rot[:, :MLA_ROPE].astype(BF16)
        k_out[hd, :, :MLA_NOPE] = (kn * k_gn * rk).astype(BF16)
        k_out[hd, :, MLA_NOPE:] = (k_rot * rk)[:, :MLA_ROPE].astype(BF16)
        v_out[hd] = kv_all[:, lo + LANES:lo + 2 * LANES].astype(BF16)


def _rope_partner():
    d = np.arange(MLA_ROPE)
    half = MLA_ROPE // 2
    partner = np.where(d < half, d + half, d - half)
    sign = np.where(d < half, -1.0, 1.0).astype(np.float32)
    return partner, sign


def _mla_tables(g):
    partner, sign = _rope_partner()
    zeros = jnp.zeros((LANES - MLA_ROPE,), F32)
    g_rope = g[MLA_NOPE:]
    rows = [g[:MLA_NOPE],
            jnp.concatenate([g_rope, zeros]),
            jnp.concatenate([g_rope[partner] * sign, zeros])]
    rows += [jnp.zeros((LANES,), F32)] * 5
    return jnp.stack(rows)


def _mla_prep(x2, pos2, attn_norm, w_in, q_a_norm, kv_a_norm, w_uq, w_ukv, q_norm, k_norm):
    T = x2.shape[0]
    tm = MLA_PREP_TILE
    partner, _ = _rope_partner()
    rope0 = Q_LORA + KV_LORA
    w_in_ext = jnp.concatenate([w_in, w_in[:, rope0 + partner]], axis=1).astype(BF16)
    wq = w_uq.reshape(Q_LORA, MLA_HEADS, MLA_QK)
    w_uq_ext = jnp.concatenate([wq, wq[:, :, MLA_NOPE + partner]], axis=2)
    w_uq_ext = w_uq_ext.reshape(Q_LORA, MLA_HEADS * 2 * LANES).astype(BF16)
    w_ukv_b = w_ukv.astype(BF16)
    inv = 1.0 / (ROPE_THETA ** (jnp.arange(0, MLA_ROPE, 2, dtype=F32) / MLA_ROPE))
    invf = jnp.concatenate([inv, inv, jnp.zeros((LANES - MLA_ROPE,), F32)])[None, :]

    full = lambda shape: pl.BlockSpec(shape, lambda i: (0,) * len(shape))
    out_qk = jax.ShapeDtypeStruct((MLA_HEADS, T, MLA_QK), BF16)
    out_v = jax.ShapeDtypeStruct((MLA_HEADS, T, MLA_V), BF16)
    return pl.pallas_call(
        _mla_prep_kernel,
        grid=(T // tm,),
        in_specs=[
            pl.BlockSpec((tm, D_MODEL), lambda i: (i, 0)),
            pl.BlockSpec((tm, 1), lambda i: (i, 0)),
            full((1, D_MODEL)),
            full((D_MODEL, 4 * LANES)),
            full((1, Q_LORA)),
            full((1, KV_LORA)),
            full((Q_LORA, MLA_HEADS * 2 * LANES)),
            full((KV_LORA, MLA_HEADS * 2 * LANES)),
            full((1, LANES)),
            full((8, LANES)),
            full((8, LANES)),
        ],
        out_specs=[
            pl.BlockSpec((MLA_HEADS, tm, MLA_QK), lambda i: (0, i, 0)),
            pl.BlockSpec((MLA_HEADS, tm, MLA_QK), lambda i: (0, i, 0)),
            pl.BlockSpec((MLA_HEADS, tm, MLA_V), lambda i: (0, i, 0)),
        ],
        out_shape=[out_qk, out_qk, out_v],
        compiler_params=_params("parallel"),
        name="mla_prep",
    )(x2, pos2, attn_norm[None, :], w_in_ext, q_a_norm[None, :], kv_a_norm[None, :],
      w_uq_ext, w_ukv_b, invf, _mla_tables(q_norm), _mla_tables(k_norm))


def _mla_attn_kernel(q_ref, k_ref, v_ref, o_ref, *, seq, tq):
    row_c = lax.broadcasted_iota(jnp.int32, (tq, tq), 0) // CHUNK
    col_c = lax.broadcasted_iota(jnp.int32, (tq, tq), 1) // CHUNK
    diag_mask = col_c <= row_c
    for qi in range(seq // tq):
        lo, hi = qi * tq, (qi + 1) * tq
        for hh in range(MLA_HEADS_PER_STEP):
            q = q_ref[hh, lo:hi, :]
            s_d = jnp.where(diag_mask, _dot_nt(q, k_ref[hh, lo:hi, :]), NEG_INF)
            m = jnp.max(s_d, axis=-1, keepdims=True)
            if qi > 0:
                s_o = _dot_nt(q, k_ref[hh, :lo, :])
                m = jnp.maximum(m, jnp.max(s_o, axis=-1, keepdims=True))
            p_d = jnp.exp(s_d - m)
            l = jnp.sum(p_d, axis=-1, keepdims=True)
            o = _dot(p_d.astype(BF16), v_ref[hh, lo:hi, :])
            if qi > 0:
                p_o = jnp.exp(s_o - m)
                l = l + jnp.sum(p_o, axis=-1, keepdims=True)
                o = o + _dot(p_o.astype(BF16), v_ref[hh, :lo, :])
            o_ref[lo:hi, hh * MLA_V:(hh + 1) * MLA_V] = (o / l).astype(BF16)


def _mla_attn(q, k, v, batch, seq):
    T = batch * seq
    hps = MLA_HEADS_PER_STEP
    kern = functools.partial(_mla_attn_kernel, seq=seq, tq=MLA_Q_TILE)
    return pl.pallas_call(
        kern,
        grid=(MLA_HEADS // hps, batch),
        in_specs=[
            pl.BlockSpec((hps, seq, MLA_QK), lambda h, b: (h, b, 0)),
            pl.BlockSpec((hps, seq, MLA_QK), lambda h, b: (h, b, 0)),
            pl.BlockSpec((hps, seq, MLA_V), lambda h, b: (h, b, 0)),
        ],
        out_specs=pl.BlockSpec((seq, hps * MLA_V), lambda h, b: (b, h)),
        out_shape=jax.ShapeDtypeStruct((T, MLA_HEADS * MLA_V), BF16),
        compiler_params=_params("parallel", "parallel"),
        name="mla_attn",
    )(q, k, v)


def _proj_dense_ffn_kernel(x_ref, o_ref, wo_ref, gn_ref, wg_ref, wu_ref, wd_ref, out_ref):
    x1 = x_ref[...] + _dot(o_ref[...], wo_ref[...])
    h = _rms(x1, gn_ref[...]).astype(BF16)
    a = (_silu(_dot(h, wg_ref[...])) * _dot(h, wu_ref[...])).astype(BF16)
    out_ref[...] = x1 + _dot(a, wd_ref[...])


def _proj_dense_ffn(x2, o2, w_o, ffn_norm, w_gate, w_up, w_down):
    T = x2.shape[0]
    ff = w_gate.shape[1]
    tm = ROW_TILE
    const = lambda shape: pl.BlockSpec(shape, lambda i: (0, 0), pipeline_mode=pl.Buffered(1))
    row = pl.BlockSpec((tm, D_MODEL), lambda i: (i, 0))
    return pl.pallas_call(
        _proj_dense_ffn_kernel,
        grid=(T // tm,),
        in_specs=[row, row, const((D_MODEL, D_MODEL)), const((1, D_MODEL)), const((D_MODEL, ff)),
                  const((D_MODEL, ff)), const((ff, D_MODEL))],
        out_specs=row,
        out_shape=jax.ShapeDtypeStruct((T, D_MODEL), F32),
        compiler_params=_params("parallel"),
        name="proj_dense_ffn",
    )(x2, o2, w_o.astype(BF16), ffn_norm[None, :], w_gate.astype(BF16), w_up.astype(BF16),
      w_down.astype(BF16))


def _ca_prep_kernel(x_ref, gn_ref, w_ref, q_out, k_out, v_out):
    h = _rms(x_ref[...], gn_ref[...]).astype(BF16)
    qkv = _dot(h, w_ref[...])
    q_out[...] = qkv[:, :D_MODEL].astype(BF16)
    k_out[...] = qkv[:, D_MODEL:2 * D_MODEL].astype(BF16)
    v_out[...] = qkv[:, 2 * D_MODEL:].astype(BF16)


def _ca_prep(x2, attn_norm, w_qkv):
    T = x2.shape[0]
    tm = ROW_TILE
    full = lambda shape: pl.BlockSpec(shape, lambda i: (0,) * len(shape))
    out = jax.ShapeDtypeStruct((T, D_MODEL), BF16)
    row = pl.BlockSpec((tm, D_MODEL), lambda i: (i, 0))
    return pl.pallas_call(
        _ca_prep_kernel,
        grid=(T // tm,),
        in_specs=[row, full((1, D_MODEL)), full((D_MODEL, 3 * D_MODEL))],
        out_specs=[row, row, row],
        out_shape=[out, out, out],
        compiler_params=_params("parallel"),
        name="ca_prep",
    )(x2, attn_norm[None, :], w_qkv.astype(BF16))


def _band_attn_kernel(q_ref, k_ref, v_ref, w_ref, g_ref, wg_ref, wu_ref, wd_ref,
                      o_ref, wg_out, wu_out, wd_out, tab_ref, qn_ref, kn_ref, *, seq, tq):
    wg_out[...] = wg_ref[...].astype(BF16)
    wu_out[...] = wu_ref[...].astype(BF16)
    wd_out[...] = wd_ref[...].astype(BF16)

    @pl.when(pl.program_id(1) == 0)
    def _():
        row = lax.broadcasted_iota(jnp.int32, (tq, BAND_KEYS), 0)
        col = lax.broadcasted_iota(jnp.int32, (tq, BAND_KEYS), 1)
        shift = CHUNK.bit_length() - 1
        for hh in range(2):
            for var in range(3):
                w = w_ref[0, 3 * hh + var:3 * hh + var + 1, :]
                t = pltpu.roll(jnp.broadcast_to(w, (tq, BAND_ROLL)), 0, 1, stride=1, stride_axis=0)
                gap = lax.shift_right_logical(row + var * tq, shift) - lax.shift_right_logical(col, shift)
                tab_ref[hh, var] = jnp.where((gap >= 0) & (gap <= LEFT_CHUNKS), t[:, :BAND_KEYS], NEG_INF)

    lane = lax.broadcasted_iota(jnp.int32, (1, LANES), 1)
    head0 = lane < CA_HEAD_DIM

    def head_normed(x_ref, gain):
        x = x_ref[...].astype(F32)
        sq = x * x
        ss0 = jnp.sum(jnp.where(head0, sq, 0.0), axis=-1, keepdims=True)
        ss1 = jnp.sum(jnp.where(head0, 0.0, sq), axis=-1, keepdims=True)
        r = jnp.where(head0, lax.rsqrt(ss0 * (1.0 / CA_HEAD_DIM) + RMS_EPS),
                      lax.rsqrt(ss1 * (1.0 / CA_HEAD_DIM) + RMS_EPS))
        return (x * r * gain).astype(BF16)

    qn_ref[...] = head_normed(q_ref, g_ref[0:1, :])
    kn_ref[...] = head_normed(k_ref, g_ref[1:2, :])

    for qi in range(seq // tq):
        var = min(qi, 2)
        ks = max(qi - 2, 0) * tq
        q = qn_ref[qi * tq:(qi + 1) * tq, :]
        k = kn_ref[ks:ks + BAND_KEYS, :]
        v = v_ref[ks:ks + BAND_KEYS, :]
        zero = jnp.zeros_like(q)
        qq = jnp.concatenate([jnp.where(head0, q, zero), jnp.where(head0, zero, q)], axis=0)
        s = _dot_nt(qq, k)
        ps, ls = [], []
        for hh in range(2):
            sh = s[hh * tq:(hh + 1) * tq] + tab_ref[hh, var]
            p = jnp.exp(sh - jnp.max(sh, axis=-1, keepdims=True))
            ls.append(jnp.sum(p, axis=-1, keepdims=True))
            ps.append(p.astype(BF16))
        oo = _dot(jnp.concatenate(ps, axis=0), v)
        o = jnp.where(head0, oo[:tq] / ls[0], oo[tq:] / ls[1])
        o_ref[qi * tq:(qi + 1) * tq, :] = o.astype(BF16)


def _band_bias_rows(rel_bias):
    tq = ATTN_Q_TILE
    rb = rel_bias.astype(F32)
    r_max = 3 * tq - 1
    n_far = r_max - MAX_REL
    n_m = r_max + BAND_KEYS
    n_neg = n_m - n_far - rb.shape[1]
    heads = rb.shape[0]
    m = jnp.concatenate([jnp.broadcast_to(rb[:, -1:], (heads, n_far)), rb[:, ::-1],
                         jnp.broadcast_to(rb[:, :1], (heads, n_neg))], axis=1)
    rows = []
    for var in range(3):
        base = r_max - var * tq
        rows.append(jnp.concatenate([m[:, base:base + BAND_KEYS], jnp.zeros((heads, 1), F32),
                                     m[:, base - (tq - 1):base]], axis=1))
    w = jnp.stack(rows, axis=1).reshape(heads // 2, 6, BAND_ROLL)
    return jnp.concatenate([w, jnp.zeros((heads // 2, 2, BAND_ROLL), F32)], axis=1)


def _band_attn(q, k, v, q_norm, k_norm, rel_bias, batch, seq, expert_weights):
    T = batch * seq
    tq = ATTN_Q_TILE
    n_steps = (CA_HEADS // 2) * batch
    gains = jnp.stack([jnp.tile(q_norm, 2) * CA_HEAD_DIM ** -0.5, jnp.tile(k_norm, 2)]
                      + [jnp.zeros((LANES,), F32)] * 6)
    kern = functools.partial(_band_attn_kernel, seq=seq, tq=tq)
    slab = pl.BlockSpec((seq, LANES), lambda hp, b: (b, hp))
    sliced = [w.reshape(n_steps, w.shape[0] * w.shape[1] // n_steps, w.shape[2]) for w in expert_weights]
    w_specs = [pl.BlockSpec((1,) + w.shape[1:], lambda hp, b: (hp * batch + b, 0, 0)) for w in sliced]
    outs = pl.pallas_call(
        kern,
        grid=(CA_HEADS // 2, batch),
        in_specs=[slab, slab, slab, pl.BlockSpec((1, 8, BAND_ROLL), lambda hp, b: (hp, 0, 0)),
                  pl.BlockSpec((8, LANES), lambda hp, b: (0, 0))] + w_specs,
        out_specs=[slab] + w_specs,
        out_shape=[jax.ShapeDtypeStruct((T, D_MODEL), BF16)]
                  + [jax.ShapeDtypeStruct(w.shape, BF16) for w in sliced],
        scratch_shapes=[pltpu.VMEM((2, 3, tq, BAND_KEYS), F32), pltpu.VMEM((seq, LANES), BF16),
                        pltpu.VMEM((seq, LANES), BF16)],
        compiler_params=_params("arbitrary", "arbitrary"),
        name="band_attn",
    )(q, k, v, _band_bias_rows(rel_bias), gains, *sliced)
    return outs[0], [wb.reshape(w.shape) for wb, w in zip(outs[1:], expert_weights)]


def _proj_router_kernel(x_ref, o_ref, wo_ref, gn_ref, rhi_ref, rlo_ref, x1_out, h_out, idx_out, w_out):
    rows = x_ref.shape[0] // ROUTER_SUBBLOCKS
    for blk in range(ROUTER_SUBBLOCKS):
        sl = slice(blk * rows, (blk + 1) * rows)
        x1 = x_ref[sl, :] + _dot(o_ref[sl, :], wo_ref[...])
        x1_out[sl, :] = x1
        h = _rms(x1, gn_ref[...])
        h_out[sl, :] = h
        h_hi = h.astype(BF16)
        h_lo = (h - h_hi.astype(F32)).astype(BF16)
        logits = _dot(h_hi, rhi_ref[...]) + (_dot(h_lo, rhi_ref[...]) + _dot(h_hi, rlo_ref[...]))

        lane = lax.broadcasted_iota(jnp.int32, logits.shape, 1)
        lane_f = lane.astype(F32)
        lg = jnp.where(lane < N_EXPERTS, logits, NEG_INF)
        m1 = jnp.max(lg, axis=-1, keepdims=True)
        i1 = jnp.min(jnp.where(lg == m1, lane_f, float(LANES)), axis=-1, keepdims=True)
        lg2 = jnp.where(lane_f == i1, NEG_INF, lg)
        m2 = jnp.max(lg2, axis=-1, keepdims=True)
        i2 = jnp.min(jnp.where(lg2 == m2, lane_f, float(LANES)), axis=-1, keepdims=True)
        e = jnp.exp(m2 - m1)
        w1 = 1.0 / (1.0 + e)
        w2 = e / (1.0 + e)
        idx_out[sl, :] = jnp.where(lane == 0, i1, jnp.where(lane == 1, i2, 0.0)).astype(jnp.int32)
        w_out[sl, :] = jnp.where(lane == 0, w1, jnp.where(lane == 1, w2, 0.0))


def _proj_router(x2, o2, w_o, ffn_norm, router):
    T = x2.shape[0]
    tm = ROUTER_TILE
    r_pad = jnp.zeros((D_MODEL, LANES), F32).at[:, :N_EXPERTS].set(router)
    r_hi = r_pad.astype(BF16)
    r_lo = (r_pad - r_hi.astype(F32)).astype(BF16)
    full = lambda shape: pl.BlockSpec(shape, lambda i: (0,) * len(shape))
    row = pl.BlockSpec((tm, D_MODEL), lambda i: (i, 0))
    small = pl.BlockSpec((tm, LANES), lambda i: (i, 0))
    return pl.pallas_call(
        _proj_router_kernel,
        grid=(T // tm,),
        in_specs=[row, row, full((D_MODEL, D_MODEL)), full((1, D_MODEL)), full((D_MODEL, LANES)),
                  full((D_MODEL, LANES))],
        out_specs=[row, row, small, small],
        out_shape=[jax.ShapeDtypeStruct((T, D_MODEL), F32), jax.ShapeDtypeStruct((T, D_MODEL), F32),
                   jax.ShapeDtypeStruct((T, LANES), jnp.int32), jax.ShapeDtypeStruct((T, LANES), F32)],
        compiler_params=_params("parallel"),
        name="proj_router",
    )(x2, o2, w_o.astype(BF16), ffn_norm[None, :], r_hi, r_lo)


def _moe_plan(top_idx, tm):
    T = top_idx.shape[0]
    n_assign = T * TOP_K
    n_tiles = n_assign // tm + N_EXPERTS
    plane = T + tm
    e_flat = top_idx.reshape(1, n_assign)
    experts = jnp.arange(N_EXPERTS, dtype=jnp.int32)
    onehot = (e_flat == experts[:, None]).astype(jnp.int32)
    csum = jnp.cumsum(onehot, axis=1)
    counts = csum[:, -1]
    tiles_per = (counts + tm - 1) // tm
    tile_end = jnp.cumsum(tiles_per)
    tile_start = tile_end - tiles_per
    group_start = jnp.cumsum(counts) - counts
    order = jnp.sum(onehot * (csum - 1 + group_start[:, None]), axis=0)
    a = jnp.arange(n_assign, dtype=jnp.int32)
    _, sorted_slot = lax.sort((order, (a % TOP_K) * plane + a // TOP_K), num_keys=1)
    sorted_slot = jnp.concatenate([sorted_slot, jnp.zeros((tm,), jnp.int32)])

    tile_id = jnp.arange(n_tiles, dtype=jnp.int32)
    tile_valid = (tile_id < tile_end[-1]).astype(jnp.int32)
    tile_expert = jnp.minimum(jnp.sum((tile_id[:, None] >= tile_end[None, :]).astype(jnp.int32), axis=1),
                              N_EXPERTS - 1)
    of_tile = lambda per_expert: jnp.sum((tile_expert[:, None] == experts[None, :]) * per_expert[None, :], axis=1)
    rows_before = (tile_id - of_tile(tile_start)) * tm
    n_real = jnp.clip(of_tile(counts) - rows_before, 0, tm) * tile_valid
    first = jnp.clip(of_tile(group_start) + rows_before, 0, n_assign)
    return (sorted_slot.astype(jnp.int32), tile_expert.astype(jnp.int32), tile_valid, first.astype(jnp.int32),
            n_real.astype(jnp.int32))


def _moe_ffn_kernel(te_ref, tv_ref, first_ref, nreal_ref, slot_ref, h_hbm, wg_ref, wu_ref, wd_ref, y_hbm,
                    x0, x1, y0, y1, hs_ref, gsem, ssem, *, tm, n_tok):
    i = pl.program_id(0)
    f = pl.program_id(1)
    s = lax.rem(i, 2)
    plane = n_tok + tm
    valid = tv_ref[i] > 0
    prev_valid = (i > 0) & (tv_ref[jnp.maximum(i - 1, 0)] > 0)
    xs = (x0, x1)
    ys = (y0, y1)

    def gather_rows(tile, par):
        base, n_real = first_ref[tile], nreal_ref[tile]

        def copy(j):
            p = slot_ref[base + j]
            tok = jnp.where(j < n_real, jnp.where(p >= plane, p - plane, p), n_tok - 1)
            return pltpu.make_async_copy(h_hbm.at[pl.ds(tok, 1), :], xs[par].at[pl.ds(j, 1), :], gsem.at[par])
        return copy

    def scatter_rows(tile, par):
        base, n_real = first_ref[tile], nreal_ref[tile]

        def copy(j):
            p = jnp.where(j < n_real, slot_ref[base + j], par * plane + n_tok + j)
            return pltpu.make_async_copy(ys[par].at[pl.ds(j, 1), :], y_hbm.at[pl.ds(p, 1), :], ssem.at[par])
        return copy

    def wait_gather(slot):
        pltpu.make_async_copy(h_hbm.at[pl.ds(0, tm), :], x0, gsem.at[slot]).wait()

    def wait_scatter(slot):
        pltpu.make_async_copy(y0, y_hbm.at[pl.ds(0, tm), :], ssem.at[slot]).wait()

    def partial_ffn():
        h = hs_ref[...]
        a = (_silu(_dot(h, wg_ref[0])) * _dot(h, wu_ref[0])).astype(BF16)
        return _dot(a, wd_ref[0])

    @pl.when((f == 0) & (i == 0))
    def _():
        y1[...] = jnp.zeros((tm, D_MODEL), F32)
        for k in range(TOP_K):
            cp = pltpu.make_async_copy(y1, y_hbm.at[pl.ds(k * plane + n_tok, tm), :], ssem.at[1])
            cp.start()
            cp.wait()

        first_tile = gather_rows(0, 0)

        def first(j, carry):
            first_tile(j).start()
            return carry
        lax.fori_loop(0, tm, first, 0)

    @pl.when((f == 0) & ((i == 0) | prev_valid))
    def _():
        wait_gather(s)

    @pl.when((f == 0) & (i >= 2) & prev_valid)
    def _():
        wait_scatter(s)

    for par in range(2):
        @pl.when(valid & (f == 0) & (s == par))
        def _(par=par):
            hs_ref[...] = xs[par][...].astype(BF16)
            next_tile = gather_rows(i + 1, 1 - par)
            for j in range(tm):
                next_tile(j).start()
            ys[par][...] = partial_ffn()

        @pl.when(valid & (f == 1) & (i > 0) & (s == par))
        def _(par=par):
            prev_tile = scatter_rows(i - 1, 1 - par)
            for j in range(tm):
                prev_tile(j).start()
            ys[par][...] += partial_ffn()

        @pl.when(jnp.logical_not(valid) & (f == 1) & prev_valid & (s == par))
        def _(par=par):
            prev_tile = scatter_rows(i - 1, 1 - par)

            def last(j, carry):
                prev_tile(j).start()
                return carry
            lax.fori_loop(0, tm, last, 0)
            wait_scatter(1 - par)

    @pl.when(valid & (f == 1) & (i == 0))
    def _():
        y0[...] += partial_ffn()


def _moe_ffn(h2, sorted_slot, tile_expert, tile_valid, tile_first, tile_n_real, w_gate, w_up, w_down, tm):
    n_tok = h2.shape[0]
    n_tiles = tile_expert.shape[0]
    ff = w_gate.shape[2]
    tf = MOE_FF_TILE
    nf = ff // tf
    assert nf == 2
    ff_blk = lambda i, f, tv: f * tv[i] + (nf - 1) * (1 - tv[i])
    kern = functools.partial(_moe_ffn_kernel, tm=tm, n_tok=n_tok)
    return pl.pallas_call(
        kern,
        grid_spec=pltpu.PrefetchScalarGridSpec(
            num_scalar_prefetch=5,
            grid=(n_tiles, nf),
            in_specs=[
                pl.BlockSpec(memory_space=pl.ANY),
                pl.BlockSpec((1, D_MODEL, tf), lambda i, f, te, tv, *_: (te[i], 0, ff_blk(i, f, tv))),
                pl.BlockSpec((1, D_MODEL, tf), lambda i, f, te, tv, *_: (te[i], 0, ff_blk(i, f, tv))),
                pl.BlockSpec((1, tf, D_MODEL), lambda i, f, te, tv, *_: (te[i], ff_blk(i, f, tv), 0)),
            ],
            out_specs=pl.BlockSpec(memory_space=pl.ANY),
            scratch_shapes=[pltpu.VMEM((tm, D_MODEL), F32)] * 4 + [
                pltpu.VMEM((tm, D_MODEL), BF16), pltpu.SemaphoreType.DMA((2,)), pltpu.SemaphoreType.DMA((2,))],
        ),
        out_shape=jax.ShapeDtypeStruct((TOP_K * (n_tok + tm), D_MODEL), F32),
        compiler_params=_params("arbitrary", "arbitrary"),
        name="moe_ffn",
    )(tile_expert, tile_valid, tile_first, tile_n_real, sorted_slot, h2, w_gate.astype(BF16),
      w_up.astype(BF16), w_down.astype(BF16))


def _moe_sum_kernel(x1_ref, wts_ref, y0_ref, y1_ref, out_ref):
    wts = wts_ref[...]
    out_ref[...] = x1_ref[...] + (wts[:, 0:1] * y0_ref[...] + wts[:, 1:2] * y1_ref[...])


def _moe_sum(x1, wts, y, tm):
    T = x1.shape[0]
    plane_blocks = (T + tm) // tm
    row = pl.BlockSpec((tm, D_MODEL), lambda i: (i, 0))
    return pl.pallas_call(
        _moe_sum_kernel,
        grid=(T // tm,),
        in_specs=[row, pl.BlockSpec((tm, LANES), lambda i: (i, 0)), row,
                  pl.BlockSpec((tm, D_MODEL), lambda i: (plane_blocks + i, 0))],
        out_specs=row,
        out_shape=jax.ShapeDtypeStruct((T, D_MODEL), F32),
        compiler_params=_params("parallel"),
        name="moe_sum",
    )(x1, wts, y, y)


def _mla_dense_layer(x2, pos2, batch, seq, attn_norm, w_in, q_a_norm, kv_a_norm, w_uq, w_ukv, q_norm,
                     k_norm, w_o, ffn_norm, w_gate, w_up, w_down):
    q, k, v = _mla_prep(x2, pos2, attn_norm, w_in, q_a_norm, kv_a_norm, w_uq, w_ukv, q_norm, k_norm)
    o = _mla_attn(q, k, v, batch, seq)
    return _proj_dense_ffn(x2, o, w_o, ffn_norm, w_gate, w_up, w_down)


def _band_moe_layer(x2, batch, seq, attn_norm, w_qkv, q_norm, k_norm, rel_bias, w_o, ffn_norm, router,
                    w_gate, w_up, w_down):
    q, k, v = _ca_prep(x2, attn_norm, w_qkv)
    o, (w_gate, w_up, w_down) = _band_attn(q, k, v, q_norm, k_norm, rel_bias, batch, seq,
                                           (w_gate, w_up, w_down))
    x1, h, idx, wts = _proj_router(x2, o, w_o, ffn_norm, router)
    tm = ROW_TILE
    sorted_slot, tile_expert, tile_valid, tile_first, tile_n_real = _moe_plan(idx[:, :TOP_K], tm)
    y = _moe_ffn(h, sorted_slot, tile_expert, tile_valid, tile_first, tile_n_real, w_gate, w_up, w_down, tm)
    return _moe_sum(x1, wts, y, tm)


def kernel(x, positions, l0_attn_norm, l0_mla_w_in, l0_mla_q_a_norm, l0_mla_kv_a_norm, l0_mla_w_uq, l0_mla_w_ukv, l0_mla_q_norm, l0_mla_k_norm, l0_mla_w_o, l0_ffn_norm, l0_ffn_w_gate, l0_ffn_w_up, l0_ffn_w_down, l1_attn_norm, l1_ca_w_qkv, l1_ca_q_norm, l1_ca_k_norm, l1_ca_rel_bias, l1_ca_w_o, l1_ffn_norm, l1_moe_router, l1_moe_w_gate, l1_moe_w_up, l1_moe_w_down, l2_attn_norm, l2_mla_w_in, l2_mla_q_a_norm, l2_mla_kv_a_norm, l2_mla_w_uq, l2_mla_w_ukv, l2_mla_q_norm, l2_mla_k_norm, l2_mla_w_o, l2_ffn_norm, l2_ffn_w_gate, l2_ffn_w_up, l2_ffn_w_down, l3_attn_norm, l3_ca_w_qkv, l3_ca_q_norm, l3_ca_k_norm, l3_ca_rel_bias, l3_ca_w_o, l3_ffn_norm, l3_moe_router, l3_moe_w_gate, l3_moe_w_up, l3_moe_w_down):
    batch, seq, d = x.shape
    assert d == D_MODEL and seq % ATTN_Q_TILE == 0 and seq >= BAND_KEYS
    assert (batch * seq) % ROW_TILE == 0
    x2 = x.reshape(batch * seq, d)
    pos2 = positions.reshape(batch * seq, 1)
    x2 = _mla_dense_layer(x2, pos2, batch, seq, l0_attn_norm, l0_mla_w_in, l0_mla_q_a_norm, l0_mla_kv_a_norm,
                          l0_mla_w_uq, l0_mla_w_ukv, l0_mla_q_norm, l0_mla_k_norm, l0_mla_w_o, l0_ffn_norm,
                          l0_ffn_w_gate, l0_ffn_w_up, l0_ffn_w_down)
    x2 = _band_moe_layer(x2, batch, seq, l1_attn_norm, l1_ca_w_qkv, l1_ca_q_norm, l1_ca_k_norm, l1_ca_rel_bias,
                         l1_ca_w_o, l1_ffn_norm, l1_moe_router, l1_moe_w_gate, l1_moe_w_up, l1_moe_w_down)
    x2 = _mla_dense_layer(x2, pos2, batch, seq, l2_attn_norm, l2_mla_w_in, l2_mla_q_a_norm, l2_mla_kv_a_norm,
                          l2_mla_w_uq, l2_mla_w_ukv, l2_mla_q_norm, l2_mla_k_norm, l2_mla_w_o, l2_ffn_norm,
                          l2_ffn_w_gate, l2_ffn_w_up, l2_ffn_w_down)
    x2 = _band_moe_layer(x2, batch, seq, l3_attn_norm, l3_ca_w_qkv, l3_ca_q_norm, l3_ca_k_norm, l3_ca_rel_bias,
                         l3_ca_w_o, l3_ffn_norm, l3_moe_router, l3_moe_w_gate, l3_moe_w_up, l3_moe_w_down)
    return x2.reshape(batch, seq, d)
```

```python
import functools

import numpy as np
import jax
import jax.numpy as jnp
from jax import lax
from jax.experimental import pallas as pl
from jax.experimental.pallas import tpu as pltpu

F32 = jnp.float32
BF16 = jnp.bfloat16

D_MODEL = 1024
CHUNK = 64
RMS_EPS = 1e-6

MLA_HEADS = 8
MLA_NOPE = 128
MLA_ROPE = 64
MLA_V = 128
MLA_QK = MLA_NOPE + MLA_ROPE
Q_LORA = 256
KV_LORA = 128
ROPE_THETA = 10000.0

CA_HEADS = 16
CA_HEAD_DIM = 64
LEFT_CHUNKS = 8
MAX_REL = 256

N_EXPERTS = 8
TOP_K = 2

LANES = 128
SUBLANES = 8
VMEM_LIMIT = 56 * 1024 * 1024

ROW_TILE = 512
MLA_PREP_TILE = 1024
ROUTER_TILE = 1024
ROUTER_SUBBLOCKS = 4
MLA_Q_TILE = 512
MLA_HEADS_PER_STEP = 2
ATTN_Q_TILE = 256
BAND_KEYS = 3 * ATTN_Q_TILE
BAND_ROLL = ATTN_Q_TILE + BAND_KEYS
MOE_FF_TILE = 1792
NEG_INF = float("-inf")


def _dot(a, b):
    return jnp.dot(a, b, preferred_element_type=F32)


def _dot_nt(a, b):
    return lax.dot_general(a, b, (((1,), (1,)), ((), ())), preferred_element_type=F32)


def _rms(x, g):
    return x * lax.rsqrt(jnp.mean(x * x, axis=-1, keepdims=True) + RMS_EPS) * g


def _silu(g):
    return g / (1.0 + jnp.exp(-g))


def _params(*sem):
    return pltpu.CompilerParams(dimension_semantics=sem, vmem_limit_bytes=VMEM_LIMIT)


def _mla_prep_kernel(x_ref, pos_ref, gn_ref, win_ref, qan_ref, kvan_ref, wuq_ref, wukv_ref,
                     invf_ref, qtab_ref, ktab_ref, q_out, k_out, v_out):
    h = _rms(x_ref[...], gn_ref[...]).astype(BF16)
    a = _dot(h, win_ref[...])
    qn = _rms(a[:, :Q_LORA], qan_ref[...]).astype(BF16)
    kvn = _rms(a[:, Q_LORA:Q_LORA + KV_LORA], kvan_ref[...]).astype(BF16)
    kr2 = a[:, Q_LORA + KV_LORA:]
    q_all = _dot(qn, wuq_ref[...])
    kv_all = _dot(kvn, wukv_ref[...])

    ang = pos_ref[...].astype(F32) * invf_ref[...]
    cos = jnp.cos(ang)
    sin = jnp.sin(ang)
    lane = lax.broadcasted_iota(jnp.int32, (1, LANES), 1)
    rope_lane = lane < MLA_ROPE

    q_gn, q_ga, q_gb = qtab_ref[0:1, :], qtab_ref[1:2, :], qtab_ref[2:3, :]
    k_gn, k_ga, k_gb = ktab_ref[0:1, :], ktab_ref[1:2, :], ktab_ref[2:3, :]
    q_cos, q_sin = cos * q_ga, sin * q_gb
    k_rot = kr2 * (cos * k_ga) + pltpu.roll(kr2, MLA_ROPE, 1) * (sin * k_gb)
    kr_ss = jnp.sum(jnp.where(rope_lane, kr2 * kr2, 0.0), axis=-1, keepdims=True)
    scale = MLA_QK ** -0.5

    slabs = []
    for hd in range(MLA_HEADS):
        lo = hd * 2 * LANES
        qa = q_all[:, lo:lo + LANES]
        qb = q_all[:, lo + LANES:lo + 2 * LANES]
        kn = kv_all[:, lo:lo + LANES]
        ss = jnp.sum(qa * qa + jnp.where(rope_lane, qb * qb, 0.0), axis=-1, keepdims=True)
        ssk = jnp.sum(kn * kn, axis=-1, keepdims=True) + kr_ss
        slabs.append((qa, qb, kn, ss, ssk))
    for hd, (qa, qb, kn, ss, ssk) in enumerate(slabs):
        lo = hd * 2 * LANES
        r = lax.rsqrt(ss * (1.0 / MLA_QK) + RMS_EPS) * scale
        rk = lax.rsqrt(ssk * (1.0 / MLA_QK) + RMS_EPS)
        q_rot = (qb * q_cos + pltpu.roll(qb, MLA_ROPE, 1) * q_sin) * r
        q_out[hd, :, :MLA_NOPE] = (qa * q_gn * r).astype(BF16)
        q_out[hd, :, MLA_NOPE:] = q_rot[:, :MLA_ROPE].astype(BF16)
        k_out[hd, :, :MLA_NOPE] = (kn * k_gn * rk).astype(BF16)
        k_out[hd, :, MLA_NOPE:] = (k_rot * rk)[:, :MLA_ROPE].astype(BF16)
        v_out[hd] = kv_all[:, lo + LANES:lo + 2 * LANES].astype(BF16)


def _rope_partner():
    d = np.arange(MLA_ROPE)
    half = MLA_ROPE // 2
    partner = np.where(d < half, d + half, d - half)
    sign = np.where(d < half, -1.0, 1.0).astype(np.float32)
    return partner, sign


def _mla_tables(g):
    partner, sign = _rope_partner()
    zeros = jnp.zeros((LANES - MLA_ROPE,), F32)
    g_rope = g[MLA_NOPE:]
    rows = [g[:MLA_NOPE],
            jnp.concatenate([g_rope, zeros]),
            jnp.concatenate([g_rope[partner] * sign, zeros])]
    rows += [jnp.zeros((LANES,), F32)] * 5
    return jnp.stack(rows)


def _mla_prep(x2, pos2, attn_norm, w_in, q_a_norm, kv_a_norm, w_uq, w_ukv, q_norm, k_norm):
    T = x2.shape[0]
    tm = MLA_PREP_TILE
    partner, _ = _rope_partner()
    rope0 = Q_LORA + KV_LORA
    w_in_ext = jnp.concatenate([w_in, w_in[:, rope0 + partner]], axis=1).astype(BF16)
    wq = w_uq.reshape(Q_LORA, MLA_HEADS, MLA_QK)
    w_uq_ext = jnp.concatenate([wq, wq[:, :, MLA_NOPE + partner]], axis=2)
    w_uq_ext = w_uq_ext.reshape(Q_LORA, MLA_HEADS * 2 * LANES).astype(BF16)
    w_ukv_b = w_ukv.astype(BF16)
    inv = 1.0 / (ROPE_THETA ** (jnp.arange(0, MLA_ROPE, 2, dtype=F32) / MLA_ROPE))
    invf = jnp.concatenate([inv, inv, jnp.zeros((LANES - MLA_ROPE,), F32)])[None, :]

    full = lambda shape: pl.BlockSpec(shape, lambda i: (0,) * len(shape))
    out_qk = jax.ShapeDtypeStruct((MLA_HEADS, T, MLA_QK), BF16)
    out_v = jax.ShapeDtypeStruct((MLA_HEADS, T, MLA_V), BF16)
    return pl.pallas_call(
        _mla_prep_kernel,
        grid=(T // tm,),
        in_specs=[
            pl.BlockSpec((tm, D_MODEL), lambda i: (i, 0)),
            pl.BlockSpec((tm, 1), lambda i: (i, 0)),
            full((1, D_MODEL)),
            full((D_MODEL, 4 * LANES)),
            full((1, Q_LORA)),
            full((1, KV_LORA)),
            full((Q_LORA, MLA_HEADS * 2 * LANES)),
            full((KV_LORA, MLA_HEADS * 2 * LANES)),
            full((1, LANES)),
            full((8, LANES)),
            full((8, LANES)),
        ],
        out_specs=[
            pl.BlockSpec((MLA_HEADS, tm, MLA_QK), lambda i: (0, i, 0)),
            pl.BlockSpec((MLA_HEADS, tm, MLA_QK), lambda i: (0, i, 0)),
            pl.BlockSpec((MLA_HEADS, tm, MLA_V), lambda i: (0, i, 0)),
        ],
        out_shape=[out_qk, out_qk, out_v],
        compiler_params=_params("parallel"),
        name="mla_prep",
    )(x2, pos2, attn_norm[None, :], w_in_ext, q_a_norm[None, :], kv_a_norm[None, :],
      w_uq_ext, w_ukv_b, invf, _mla_tables(q_norm), _mla_tables(k_norm))


def _mla_attn_kernel(q_ref, k_ref, v_ref, o_ref, *, seq, tq):
    row_c = lax.broadcasted_iota(jnp.int32, (tq, tq), 0) // CHUNK
    col_c = lax.broadcasted_iota(jnp.int32, (tq, tq), 1) // CHUNK
    diag_mask = col_c <= row_c
    for qi in range(seq // tq):
        lo, hi = qi * tq, (qi + 1) * tq
        for hh in range(MLA_HEADS_PER_STEP):
            q = q_ref[hh, lo:hi, :]
            s_d = jnp.where(diag_mask, _dot_nt(q, k_ref[hh, lo:hi, :]), NEG_INF)
            m = jnp.max(s_d, axis=-1, keepdims=True)
            if qi > 0:
                s_o = _dot_nt(q, k_ref[hh, :lo, :])
                m = jnp.maximum(m, jnp.max(s_o, axis=-1, keepdims=True))
            p_d = jnp.exp(s_d - m)
            l = jnp.sum(p_d, axis=-1, keepdims=True)
            o = _dot(p_d.astype(BF16), v_ref[hh, lo:hi, :])
            if qi > 0:
                p_o = jnp.exp(s_o - m)
                l = l + jnp.sum(p_o, axis=-1, keepdims=True)
                o = o + _dot(p_o.astype(BF16), v_ref[hh, :lo, :])
            o_ref[lo:hi, hh * MLA_V:(hh + 1) * MLA_V] = (o / l).astype(BF16)


def _mla_attn(q, k, v, batch, seq):
    T = batch * seq
    hps = MLA_HEADS_PER_STEP
    kern = functools.partial(_mla_attn_kernel, seq=seq, tq=MLA_Q_TILE)
    return pl.pallas_call(
        kern,
        grid=(MLA_HEADS // hps, batch),
        in_specs=[
            pl.BlockSpec((hps, seq, MLA_QK), lambda h, b: (h, b, 0)),
            pl.BlockSpec((hps, seq, MLA_QK), lambda h, b: (h, b, 0)),
            pl.BlockSpec((hps, seq, MLA_V), lambda h, b: (h, b, 0)),
        ],
        out_specs=pl.BlockSpec((seq, hps * MLA_V), lambda h, b: (b, h)),
        out_shape=jax.ShapeDtypeStruct((T, MLA_HEADS * MLA_V), BF16),
        compiler_params=_params("parallel", "parallel"),
        name="mla_attn",
    )(q, k, v)


def _proj_dense_ffn_kernel(x_ref, o_ref, wo_ref, gn_ref, wg_ref, wu_ref, wd_ref, out_ref):
    x1 = x_ref[...] + _dot(o_ref[...], wo_ref[...])
    h = _rms(x1, gn_ref[...]).astype(BF16)
    a = (_silu(_dot(h, wg_ref[...])) * _dot(h, wu_ref[...])).astype(BF16)
    out_ref[...] = x1 + _dot(a, wd_ref[...])


def _proj_dense_ffn(x2, o2, w_o, ffn_norm, w_gate, w_up, w_down):
    T = x2.shape[0]
    ff = w_gate.shape[1]
    tm = ROW_TILE
    const = lambda shape: pl.BlockSpec(shape, lambda i: (0, 0), pipeline_mode=pl.Buffered(1))
    row = pl.BlockSpec((tm, D_MODEL), lambda i: (i, 0))
    return pl.pallas_call(
        _proj_dense_ffn_kernel,
        grid=(T // tm,),
        in_specs=[row, row, const((D_MODEL, D_MODEL)), const((1, D_MODEL)), const((D_MODEL, ff)),
                  const((D_MODEL, ff)), const((ff, D_MODEL))],
        out_specs=row,
        out_shape=jax.ShapeDtypeStruct((T, D_MODEL), F32),
        compiler_params=_params("parallel"),
        name="proj_dense_ffn",
    )(x2, o2, w_o.astype(BF16), ffn_norm[None, :], w_gate.astype(BF16), w_up.astype(BF16),
      w_down.astype(BF16))


def _ca_prep_kernel(x_ref, gn_ref, w_ref, q_out, k_out, v_out):
    h = _rms(x_ref[...], gn_ref[...]).astype(BF16)
    qkv = _dot(h, w_ref[...])
    q_out[...] = qkv[:, :D_MODEL].astype(BF16)
    k_out[...] = qkv[:, D_MODEL:2 * D_MODEL].astype(BF16)
    v_out[...] = qkv[:, 2 * D_MODEL:].astype(BF16)


def _ca_prep(x2, attn_norm, w_qkv):
    T = x2.shape[0]
    tm = ROW_TILE
    full = lambda shape: pl.BlockSpec(shape, lambda i: (0,) * len(shape))
    out = jax.ShapeDtypeStruct((T, D_MODEL), BF16)
    row = pl.BlockSpec((tm, D_MODEL), lambda i: (i, 0))
    return pl.pallas_call(
        _ca_prep_kernel,
        grid=(T // tm,),
        in_specs=[row, full((1, D_MODEL)), full((D_MODEL, 3 * D_MODEL))],
        out_specs=[row, row, row],
        out_shape=[out, out, out],
        compiler_params=_params("parallel"),
        name="ca_prep",
    )(x2, attn_norm[None, :], w_qkv.astype(BF16))


def _band_attn_kernel(q_ref, k_ref, v_ref, w_ref, g_ref, wg_ref, wu_ref, wd_ref,
                      o_ref, wg_out, wu_out, wd_out, tab_ref, qn_ref, kn_ref, *, seq, tq):
    wg_out[...] = wg_ref[...].astype(BF16)
    wu_out[...] = wu_ref[...].astype(BF16)
    wd_out[...] = wd_ref[...].astype(BF16)

    @pl.when(pl.program_id(1) == 0)
    def _():
        row = lax.broadcasted_iota(jnp.int32, (tq, BAND_KEYS), 0)
        col = lax.broadcasted_iota(jnp.int32, (tq, BAND_KEYS), 1)
        shift = CHUNK.bit_length() - 1
        for hh in range(2):
            for var in range(3):
                w = w_ref[0, 3 * hh + var:3 * hh + var + 1, :]
                t = pltpu.roll(jnp.broadcast_to(w, (tq, BAND_ROLL)), 0, 1, stride=1, stride_axis=0)
                gap = lax.shift_right_logical(row + var * tq, shift) - lax.shift_right_logical(col, shift)
                tab_ref[hh, var] = jnp.where((gap >= 0) & (gap <= LEFT_CHUNKS), t[:, :BAND_KEYS], NEG_INF)

    lane = lax.broadcasted_iota(jnp.int32, (1, LANES), 1)
    head0 = lane < CA_HEAD_DIM

    def head_normed(x_ref, gain):
        x = x_ref[...].astype(F32)
        sq = x * x
        ss0 = jnp.sum(jnp.where(head0, sq, 0.0), axis=-1, keepdims=True)
        ss1 = jnp.sum(jnp.where(head0, 0.0, sq), axis=-1, keepdims=True)
        r = jnp.where(head0, lax.rsqrt(ss0 * (1.0 / CA_HEAD_DIM) + RMS_EPS),
                      lax.rsqrt(ss1 * (1.0 / CA_HEAD_DIM) + RMS_EPS))
        return (x * r * gain).astype(BF16)

    qn_ref[...] = head_normed(q_ref, g_ref[0:1, :])
    kn_ref[...] = head_normed(k_ref, g_ref[1:2, :])

    for qi in range(seq // tq):
        var = min(qi, 2)
        ks = max(qi - 2, 0) * tq
        q = qn_ref[qi * tq:(qi + 1) * tq, :]
        k = kn_ref[ks:ks + BAND_KEYS, :]
        v = v_ref[ks:ks + BAND_KEYS, :]
        zero = jnp.zeros_like(q)
        qq = jnp.concatenate([jnp.where(head0, q, zero), jnp.where(head0, zero, q)], axis=0)
        s = _dot_nt(qq, k)
        ps, ls = [], []
        for hh in range(2):
            sh = s[hh * tq:(hh + 1) * tq] + tab_ref[hh, var]
            p = jnp.exp(sh - jnp.max(sh, axis=-1, keepdims=True))
            ls.append(jnp.sum(p, axis=-1, keepdims=True))
            ps.append(p.astype(BF16))
        oo = _dot(jnp.concatenate(ps, axis=0), v)
        o = jnp.where(head0, oo[:tq] / ls[0], oo[tq:] / ls[1])
        o_ref[qi * tq:(qi + 1) * tq, :] = o.astype(BF16)


def _band_bias_rows(rel_bias):
    tq = ATTN_Q_TILE
    rb = rel_bias.astype(F32)
    r_max = 3 * tq - 1
    n_far = r_max - MAX_REL
    n_m = r_max + BAND_KEYS
    n_neg = n_m - n_far - rb.shape[1]
    heads = rb.shape[0]
    m = jnp.concatenate([jnp.broadcast_to(rb[:, -1:], (heads, n_far)), rb[:, ::-1],
                         jnp.broadcast_to(rb[:, :1], (heads, n_neg))], axis=1)
    rows = []
    for var in range(3):
        base = r_max - var * tq
        rows.append(jnp.concatenate([m[:, base:base + BAND_KEYS], jnp.zeros((heads, 1), F32),
                                     m[:, base - (tq - 1):base]], axis=1))
    w = jnp.stack(rows, axis=1).reshape(heads // 2, 6, BAND_ROLL)
    return jnp.concatenate([w, jnp.zeros((heads // 2, 2, BAND_ROLL), F32)], axis=1)


def _band_attn(q, k, v, q_norm, k_norm, rel_bias, batch, seq, expert_weights):
    T = batch * seq
    tq = ATTN_Q_TILE
    n_steps = (CA_HEADS // 2) * batch
    gains = jnp.stack([jnp.tile(q_norm, 2) * CA_HEAD_DIM ** -0.5, jnp.tile(k_norm, 2)]
                      + [jnp.zeros((LANES,), F32)] * 6)
    kern = functools.partial(_band_attn_kernel, seq=seq, tq=tq)
    slab = pl.BlockSpec((seq, LANES), lambda hp, b: (b, hp))
    sliced = [w.reshape(n_steps, w.shape[0] * w.shape[1] // n_steps, w.shape[2]) for w in expert_weights]
    w_specs = [pl.BlockSpec((1,) + w.shape[1:], lambda hp, b: (hp * batch + b, 0, 0)) for w in sliced]
    outs = pl.pallas_call(
        kern,
        grid=(CA_HEADS // 2, batch),
        in_specs=[slab, slab, slab, pl.BlockSpec((1, 8, BAND_ROLL), lambda hp, b: (hp, 0, 0)),
                  pl.BlockSpec((8, LANES), lambda hp, b: (0, 0))] + w_specs,
        out_specs=[slab] + w_specs,
        out_shape=[jax.ShapeDtypeStruct((T, D_MODEL), BF16)]
                  + [jax.ShapeDtypeStruct(w.shape, BF16) for w in sliced],
        scratch_shapes=[pltpu.VMEM((2, 3, tq, BAND_KEYS), F32), pltpu.VMEM((seq, LANES), BF16),
                        pltpu.VMEM((seq, LANES), BF16)],
        compiler_params=_params("arbitrary", "arbitrary"),
        name="band_attn",
    )(q, k, v, _band_bias_rows(rel_bias), gains, *sliced)
    return outs[0], [wb.reshape(w.shape) for wb, w in zip(outs[1:], expert_weights)]


def _proj_router_kernel(x_ref, o_ref, wo_ref, gn_ref, rhi_ref, rlo_ref, x1_out, h_out, idx_out, w_out):
    rows = x_ref.shape[0] // ROUTER_SUBBLOCKS
    for blk in range(ROUTER_SUBBLOCKS):
        sl = slice(blk * rows, (blk + 1) * rows)
        x1 = x_ref[sl, :] + _dot(o_ref[sl, :], wo_ref[...])
        x1_out[sl, :] = x1
        h = _rms(x1, gn_ref[...])
        h_out[sl, :] = h
        h_hi = h.astype(BF16)
        h_lo = (h - h_hi.astype(F32)).astype(BF16)
        logits = _dot(h_hi, rhi_ref[...]) + (_dot(h_lo, rhi_ref[...]) + _dot(h_hi, rlo_ref[...]))

        lane = lax.broadcasted_iota(jnp.int32, logits.shape, 1)
        lane_f = lane.astype(F32)
        lg = jnp.where(lane < N_EXPERTS, logits, NEG_INF)
        m1 = jnp.max(lg, axis=-1, keepdims=True)
        i1 = jnp.min(jnp.where(lg == m1, lane_f, float(LANES)), axis=-1, keepdims=True)
        lg2 = jnp.where(lane_f == i1, NEG_INF, lg)
        m2 = jnp.max(lg2, axis=-1, keepdims=True)
        i2 = jnp.min(jnp.where(lg2 == m2, lane_f, float(LANES)), axis=-1, keepdims=True)
        e = jnp.exp(m2 - m1)
        w1 = 1.0 / (1.0 + e)
        w2 = e / (1.0 + e)
        idx_out[sl, :] = jnp.where(lane == 0, i1, jnp.where(lane == 1, i2, 0.0)).astype(jnp.int32)
        w_out[sl, :] = jnp.where(lane == 0, w1, jnp.where(lane == 1, w2, 0.0))


def _proj_router(x2, o2, w_o, ffn_norm, router):
    T = x2.shape[0]
    tm = ROUTER_TILE
    r_pad = jnp.zeros((D_MODEL, LANES), F32).at[:, :N_EXPERTS].set(router)
    r_hi = r_pad.astype(BF16)
    r_lo = (r_pad - r_hi.astype(F32)).astype(BF16)
    full = lambda shape: pl.BlockSpec(shape, lambda i: (0,) * len(shape))
    row = pl.BlockSpec((tm, D_MODEL), lambda i: (i, 0))
    small = pl.BlockSpec((tm, LANES), lambda i: (i, 0))
    return pl.pallas_call(
        _proj_router_kernel,
        grid=(T // tm,),
        in_specs=[row, row, full((D_MODEL, D_MODEL)), full((1, D_MODEL)), full((D_MODEL, LANES)),
                  full((D_MODEL, LANES))],
        out_specs=[row, row, small, small],
        out_shape=[jax.ShapeDtypeStruct((T, D_MODEL), F32), jax.ShapeDtypeStruct((T, D_MODEL), F32),
                   jax.ShapeDtypeStruct((T, LANES), jnp.int32), jax.ShapeDtypeStruct((T, LANES), F32)],
        compiler_params=_params("parallel"),
        name="proj_router",
    )(x2, o2, w_o.astype(BF16), ffn_norm[None, :], r_hi, r_lo)


def _moe_plan(top_idx, tm):
    T = top_idx.shape[0]
    n_assign = T * TOP_K
    n_tiles = n_assign // tm + N_EXPERTS
    plane = T + tm
    e_flat = top_idx.reshape(1, n_assign)
    experts = jnp.arange(N_EXPERTS, dtype=jnp.int32)
    onehot = (e_flat == experts[:, None]).astype(jnp.int32)
    csum = jnp.cumsum(onehot, axis=1)
    counts = csum[:, -1]
    tiles_per = (counts + tm - 1) // tm
    tile_end = jnp.cumsum(tiles_per)
    tile_start = tile_end - tiles_per
    group_start = jnp.cumsum(counts) - counts
    order = jnp.sum(onehot * (csum - 1 + group_start[:, None]), axis=0)
    a = jnp.arange(n_assign, dtype=jnp.int32)
    _, sorted_slot = lax.sort((order, (a % TOP_K) * plane + a // TOP_K), num_keys=1)
    sorted_slot = jnp.concatenate([sorted_slot, jnp.zeros((tm,), jnp.int32)])

    tile_id = jnp.arange(n_tiles, dtype=jnp.int32)
    tile_valid = (tile_id < tile_end[-1]).astype(jnp.int32)
    tile_expert = jnp.minimum(jnp.sum((tile_id[:, None] >= tile_end[None, :]).astype(jnp.int32), axis=1),
                              N_EXPERTS - 1)
    of_tile = lambda per_expert: jnp.sum((tile_expert[:, None] == experts[None, :]) * per_expert[None, :], axis=1)
    rows_before = (tile_id - of_tile(tile_start)) * tm
    n_real = jnp.clip(of_tile(counts) - rows_before, 0, tm) * tile_valid
    first = jnp.clip(of_tile(group_start) + rows_before, 0, n_assign)
    return (sorted_slot.astype(jnp.int32), tile_expert.astype(jnp.int32), tile_valid, first.astype(jnp.int32),
            n_real.astype(jnp.int32))


def _moe_ffn_kernel(te_ref, tv_ref, first_ref, nreal_ref, slot_ref, h_hbm, wg_ref, wu_ref, wd_ref, y_hbm,
                    x0, x1, y0, y1, acc_ref, hs_ref, gsem, ssem, *, tm, n_tok):
    i = pl.program_id(0)
    f = pl.program_id(1)
    s = lax.rem(i, 2)
    plane = n_tok + tm
    valid = tv_ref[i] > 0
    prev_valid = (i > 0) & (tv_ref[jnp.maximum(i - 1, 0)] > 0)
    xs = (x0, x1)
    ys = (y0, y1)

    def gather_rows(tile, par):
        base, n_real = first_ref[tile], nreal_ref[tile]

        def copy(j):
            p = slot_ref[base + j]
            tok = jnp.where(j < n_real, jnp.where(p >= plane, p - plane, p), n_tok - 1)
            return pltpu.make_async_copy(h_hbm.at[pl.ds(tok, 1), :], xs[par].at[pl.ds(j, 1), :], gsem.at[par])
        return copy

    def scatter_rows(tile, par):
        base, n_real = first_ref[tile], nreal_ref[tile]

        def copy(j):
            p = jnp.where(j < n_real, slot_ref[base + j], par * plane + n_tok + j)
            return pltpu.make_async_copy(ys[par].at[pl.ds(j, 1), :], y_hbm.at[pl.ds(p, 1), :], ssem.at[par])
        return copy

    def wait_gather(slot):
        pltpu.make_async_copy(h_hbm.at[pl.ds(0, tm), :], x0, gsem.at[slot]).wait()

    def wait_scatter(slot):
        pltpu.make_async_copy(y0, y_hbm.at[pl.ds(0, tm), :], ssem.at[slot]).wait()

    def partial_ffn():
        h = hs_ref[...]
        a = (_silu(_dot(h, wg_ref[0])) * _dot(h, wu_ref[0])).astype(BF16)
        return _dot(a, wd_ref[0])

    @pl.when((f == 0) & (i == 0))
    def _():
        y1[...] = jnp.zeros((tm, D_MODEL), F32)
        for k in range(TOP_K):
            cp = pltpu.make_async_copy(y1, y_hbm.at[pl.ds(k * plane + n_tok, tm), :], ssem.at[1])
            cp.start()
            cp.wait()

        first_tile = gather_rows(0, 0)

        def first(j, carry):
            first_tile(j).start()
            return carry
        lax.fori_loop(0, tm, first, 0)

    @pl.when((f == 0) & ((i == 0) | prev_valid))
    def _():
        wait_gather(s)

    @pl.when((f == 1) & (i >= 2) & prev_valid)
    def _():
        wait_scatter(s)

    for par in range(2):
        @pl.when(valid & (f == 0) & (s == par))
        def _(par=par):
            hs_ref[...] = xs[par][...].astype(BF16)
            next_tile = gather_rows(i + 1, 1 - par)
            for j in range(tm):
                next_tile(j).start()
            acc_ref[...] = partial_ffn()

        @pl.when(valid & (f == 1) & (i > 0) & (s == par))
        def _(par=par):
            prev_tile = scatter_rows(i - 1, 1 - par)
            for j in range(tm):
                prev_tile(j).start()
            ys[par][...] = acc_ref[...] + partial_ffn()

        @pl.when(jnp.logical_not(valid) & (f == 1) & prev_valid & (s == par))
        def _(par=par):
            prev_tile = scatter_rows(i - 1, 1 - par)

            def last(j, carry):
                prev_tile(j).start()
                return carry
            lax.fori_loop(0, tm, last, 0)
            wait_scatter(1 - par)

    @pl.when(valid & (f == 1) & (i == 0))
    def _():
        y0[...] = acc_ref[...] + partial_ffn()


def _moe_ffn(h2, sorted_slot, tile_expert, tile_valid, tile_first, tile_n_real, w_gate, w_up, w_down, tm):
    n_tok = h2.shape[0]
    n_tiles = tile_expert.shape[0]
    ff = w_gate.shape[2]
    tf = MOE_FF_TILE
    nf = ff // tf
    assert nf == 2
    ff_blk = lambda i, f, tv: f * tv[i] + (nf - 1) * (1 - tv[i])
    kern = functools.partial(_moe_ffn_kernel, tm=tm, n_tok=n_tok)
    return pl.pallas_call(
        kern,
        grid_spec=pltpu.PrefetchScalarGridSpec(
            num_scalar_prefetch=5,
            grid=(n_tiles, nf),
            in_specs=[
                pl.BlockSpec(memory_space=pl.ANY),
                pl.BlockSpec((1, D_MODEL, tf), lambda i, f, te, tv, *_: (te[i], 0, ff_blk(i, f, tv))),
                pl.BlockSpec((1, D_MODEL, tf), lambda i, f, te, tv, *_: (te[i], 0, ff_blk(i, f, tv))),
                pl.BlockSpec((1, tf, D_MODEL), lambda i, f, te, tv, *_: (te[i], ff_blk(i, f, tv), 0)),
            ],
            out_specs=pl.BlockSpec(memory_space=pl.ANY),
            scratch_shapes=[pltpu.VMEM((tm, D_MODEL), F32)] * 5 + [
                pltpu.VMEM((tm, D_MODEL), BF16), pltpu.SemaphoreType.DMA((2,)), pltpu.SemaphoreType.DMA((2,))],
        ),
        out_shape=jax.ShapeDtypeStruct((TOP_K * (n_tok + tm), D_MODEL), F32),
        compiler_params=_params("arbitrary", "arbitrary"),
        name="moe_ffn",
    )(tile_expert, tile_valid, tile_first, tile_n_real, sorted_slot, h2, w_gate.astype(BF16),
      w_up.astype(BF16), w_down.astype(BF16))


def _moe_sum_kernel(x1_ref, wts_ref, y0_ref, y1_ref, out_ref):
    wts = wts_ref[...]
    out_ref[...] = x1_ref[...] + (wts[:, 0:1] * y0_ref[...] + wts[:, 1:2] * y1_ref[...])


def _moe_sum(x1, wts, y, tm):
    T = x1.shape[0]
    plane_blocks = (T + tm) // tm
    row = pl.BlockSpec((tm, D_MODEL), lambda i: (i, 0))
    return pl.pallas_call(
        _moe_sum_kernel,
        grid=(T // tm,),
        in_specs=[row, pl.BlockSpec((tm, LANES), lambda i: (i, 0)), row,
                  pl.BlockSpec((tm, D_MODEL), lambda i: (plane_blocks + i, 0))],
        out_specs=row,
        out_shape=jax.ShapeDtypeStruct((T, D_MODEL), F32),
        compiler_params=_params("parallel"),
        name="moe_sum",
    )(x1, wts, y, y)


def _mla_dense_layer(x2, pos2, batch, seq, attn_norm, w_in, q_a_norm, kv_a_norm, w_uq, w_ukv, q_norm,
                     k_norm, w_o, ffn_norm, w_gate, w_up, w_down):
    q, k, v = _mla_prep(x2, pos2, attn_norm, w_in, q_a_norm, kv_a_norm, w_uq, w_ukv, q_norm, k_norm)
    o = _mla_attn(q, k, v, batch, seq)
    return _proj_dense_ffn(x2, o, w_o, ffn_norm, w_gate, w_up, w_down)


def _band_moe_layer(x2, batch, seq, attn_norm, w_qkv, q_norm, k_norm, rel_bias, w_o, ffn_norm, router,
                    w_gate, w_up, w_down):
    q, k, v = _ca_prep(x2, attn_norm, w_qkv)
    o, (w_gate, w_up, w_down) = _band_attn(q, k, v, q_norm, k_norm, rel_bias, batch, seq,
                                           (w_gate, w_up, w_down))
    x1, h, idx, wts = _proj_router(x2, o, w_o, ffn_norm, router)
    tm = ROW_TILE
    sorted_slot, tile_expert, tile_valid, tile_first, tile_n_real = _moe_plan(idx[:, :TOP_K], tm)
    y = _moe_ffn(h, sorted_slot, tile_expert, tile_valid, tile_first, tile_n_real, w_gate, w_up, w_down, tm)
    return _moe_sum(x1, wts, y, tm)


def kernel(x, positions, l0_attn_norm, l0_mla_w_in, l0_mla_q_a_norm, l0_mla_kv_a_norm, l0_mla_w_uq, l0_mla_w_ukv, l0_mla_q_norm, l0_mla_k_norm, l0_mla_w_o, l0_ffn_norm, l0_ffn_w_gate, l0_ffn_w_up, l0_ffn_w_down, l1_attn_norm, l1_ca_w_qkv, l1_ca_q_norm, l1_ca_k_norm, l1_ca_rel_bias, l1_ca_w_o, l1_ffn_norm, l1_moe_router, l1_moe_w_gate, l1_moe_w_up, l1_moe_w_down, l2_attn_norm, l2_mla_w_in, l2_mla_q_a_norm, l2_mla_kv_a_norm, l2_mla_w_uq, l2_mla_w_ukv, l2_mla_q_norm, l2_mla_k_norm, l2_mla_w_o, l2_ffn_norm, l2_ffn_w_gate, l2_ffn_w_up, l2_ffn_w_down, l3_attn_norm, l3_ca_w_qkv, l3_ca_q_norm, l3_ca_k_norm, l3_ca_rel_bias, l3_ca_w_o, l3_ffn_norm, l3_moe_router, l3_moe_w_gate, l3_moe_w_up, l3_moe_w_down):
    batch, seq, d = x.shape
    assert d == D_MODEL and seq % ATTN_Q_TILE == 0 and seq >= BAND_KEYS
    assert (batch * seq) % ROW_TILE == 0
    x2 = x.reshape(batch * seq, d)
    pos2 = positions.reshape(batch * seq, 1)
    x2 = _mla_dense_layer(x2, pos2, batch, seq, l0_attn_norm, l0_mla_w_in, l0_mla_q_a_norm, l0_mla_kv_a_norm,
                          l0_mla_w_uq, l0_mla_w_ukv, l0_mla_q_norm, l0_mla_k_norm, l0_mla_w_o, l0_ffn_norm,
                          l0_ffn_w_gate, l0_ffn_w_up, l0_ffn_w_down)
    x2 = _band_moe_layer(x2, batch, seq, l1_attn_norm, l1_ca_w_qkv, l1_ca_q_norm, l1_ca_k_norm, l1_ca_rel_bias,
                         l1_ca_w_o, l1_ffn_norm, l1_moe_router, l1_moe_w_gate, l1_moe_w_up, l1_moe_w_down)
    x2 = _mla_dense_layer(x2, pos2, batch, seq, l2_attn_norm, l2_mla_w_in, l2_mla_q_a_norm, l2_mla_kv_a_norm,
                          l2_mla_w_uq, l2_mla_w_ukv, l2_mla_q_norm, l2_mla_k_norm, l2_mla_w_o, l2_ffn_norm,
                          l2_ffn_w_gate, l2_ffn_w_up, l2_ffn_w_down)
    x2 = _band_moe_layer(x2, batch, seq, l3_attn_norm, l3_ca_w_qkv, l3_ca_q_norm, l3_ca_k_norm, l3_ca_rel_bias,
                         l3_ca_w_o, l3_ffn_norm, l3_moe_router, l3_moe_w_gate, l3_moe_w_up, l3_moe_w_down)
    return x2.reshape(batch, seq, d)
```

```python
import functools

import numpy as np
import jax
import jax.numpy as jnp
from jax import lax
from jax.experimental import pallas as pl
from jax.experimental.pallas import tpu as pltpu

F32 = jnp.float32
BF16 = jnp.bfloat16

D_MODEL = 1024
CHUNK = 64
RMS_EPS = 1e-6

MLA_HEADS = 8
MLA_NOPE = 128
MLA_ROPE = 64
MLA_V = 128
MLA_QK = MLA_NOPE + MLA_ROPE
Q_LORA = 256
KV_LORA = 128
ROPE_THETA = 10000.0

CA_HEADS = 16
CA_HEAD_DIM = 64
LEFT_CHUNKS = 8
MAX_REL = 256

N_EXPERTS = 8
TOP_K = 2

LANES = 128
SUBLANES = 8
VMEM_LIMIT = 56 * 1024 * 1024

ROW_TILE = 512
MLA_PREP_TILE = 1024
ROUTER_TILE = 1024
ROUTER_SUBBLOCKS = 4
MLA_Q_TILE = 512
MLA_HEADS_PER_STEP = 2
ATTN_Q_TILE = 256
BAND_KEYS = 3 * ATTN_Q_TILE
BAND_ROLL = ATTN_Q_TILE + BAND_KEYS
MOE_FF_TILE = 1792
NEG_INF = float("-inf")


def _dot(a, b):
    return jnp.dot(a, b, preferred_element_type=F32)


def _dot_nt(a, b):
    return lax.dot_general(a, b, (((1,), (1,)), ((), ())), preferred_element_type=F32)


def _rms(x, g):
    return x * lax.rsqrt(jnp.mean(x * x, axis=-1, keepdims=True) + RMS_EPS) * g


def _silu(g):
    return g / (1.0 + jnp.exp(-g))


def _params(*sem):
    return pltpu.CompilerParams(dimension_semantics=sem, vmem_limit_bytes=VMEM_LIMIT)


def _mla_prep_kernel(x_ref, pos_ref, gn_ref, win_ref, qan_ref, kvan_ref, wuq_ref, wukv_ref,
                     invf_ref, qtab_ref, ktab_ref, q_out, k_out, v_out):
    h = _rms(x_ref[...], gn_ref[...]).astype(BF16)
    a = _dot(h, win_ref[...])
    qn = _rms(a[:, :Q_LORA], qan_ref[...]).astype(BF16)
    kvn = _rms(a[:, Q_LORA:Q_LORA + KV_LORA], kvan_ref[...]).astype(BF16)
    kr2 = a[:, Q_LORA + KV_LORA:]
    q_all = _dot(qn, wuq_ref[...])
    kv_all = _dot(kvn, wukv_ref[...])

    ang = pos_ref[...].astype(F32) * invf_ref[...]
    cos = jnp.cos(ang)
    sin = jnp.sin(ang)
    lane = lax.broadcasted_iota(jnp.int32, (1, LANES), 1)
    rope_lane = lane < MLA_ROPE

    q_gn, q_ga, q_gb = qtab_ref[0:1, :], qtab_ref[1:2, :], qtab_ref[2:3, :]
    k_gn, k_ga, k_gb = ktab_ref[0:1, :], ktab_ref[1:2, :], ktab_ref[2:3, :]
    q_cos, q_sin = cos * q_ga, sin * q_gb
    k_rot = kr2 * (cos * k_ga) + pltpu.roll(kr2, MLA_ROPE, 1) * (sin * k_gb)
    kr_ss = jnp.sum(jnp.where(rope_lane, kr2 * kr2, 0.0), axis=-1, keepdims=True)
    scale = MLA_QK ** -0.5

    slabs = []
    for hd in range(MLA_HEADS):
        lo = hd * 2 * LANES
        qa = q_all[:, lo:lo + LANES]
        qb = q_all[:, lo + LANES:lo + 2 * LANES]
        kn = kv_all[:, lo:lo + LANES]
        ss = jnp.sum(qa * qa + jnp.where(rope_lane, qb * qb, 0.0), axis=-1, keepdims=True)
        ssk = jnp.sum(kn * kn, axis=-1, keepdims=True) + kr_ss
        slabs.append((qa, qb, kn, ss, ssk))
    for hd, (qa, qb, kn, ss, ssk) in enumerate(slabs):
        lo = hd * 2 * LANES
        r = lax.rsqrt(ss * (1.0 / MLA_QK) + RMS_EPS) * scale
        rk = lax.rsqrt(ssk * (1.0 / MLA_QK) + RMS_EPS)
        q_rot = (qb * q_cos + pltpu.roll(qb, MLA_ROPE, 1) * q_sin) * r
        q_out[hd, :, :MLA_NOPE] = (qa * q_gn * r).astype(BF16)
        q_out[hd, :, MLA_NOPE:] = q_rot[:, :MLA_ROPE].astype(BF16)
        k_out[hd, :, :MLA_NOPE] = (kn * k_gn * rk).astype(BF16)
        k_out[hd, :, MLA_NOPE:] = (k_rot * rk)[:, :MLA_ROPE].astype(BF16)
        v_out[hd] = kv_all[:, lo + LANES:lo + 2 * LANES].astype(BF16)


def _rope_partner():
    d = np.arange(MLA_ROPE)
    half = MLA_ROPE // 2
    partner = np.where(d < half, d + half, d - half)
    sign = np.where(d < half, -1.0, 1.0).astype(np.float32)
    return partner, sign


def _mla_tables(g):
    partner, sign = _rope_partner()
    zeros = jnp.zeros((LANES - MLA_ROPE,), F32)
    g_rope = g[MLA_NOPE:]
    rows = [g[:MLA_NOPE],
            jnp.concatenate([g_rope, zeros]),
            jnp.concatenate([g_rope[partner] * sign, zeros])]
    rows += [jnp.zeros((LANES,), F32)] * 5
    return jnp.stack(rows)


def _mla_prep(x2, pos2, attn_norm, w_in, q_a_norm, kv_a_norm, w_uq, w_ukv, q_norm, k_norm):
    T = x2.shape[0]
    tm = MLA_PREP_TILE
    partner, _ = _rope_partner()
    rope0 = Q_LORA + KV_LORA
    w_in_ext = jnp.concatenate([w_in, w_in[:, rope0 + partner]], axis=1).astype(BF16)
    wq = w_uq.reshape(Q_LORA, MLA_HEADS, MLA_QK)
    w_uq_ext = jnp.concatenate([wq, wq[:, :, MLA_NOPE + partner]], axis=2)
    w_uq_ext = w_uq_ext.reshape(Q_LORA, MLA_HEADS * 2 * LANES).astype(BF16)
    w_ukv_b = w_ukv.astype(BF16)
    inv = 1.0 / (ROPE_THETA ** (jnp.arange(0, MLA_ROPE, 2, dtype=F32) / MLA_ROPE))
    invf = jnp.concatenate([inv, inv, jnp.zeros((LANES - MLA_ROPE,), F32)])[None, :]

    full = lambda shape: pl.BlockSpec(shape, lambda i: (0,) * len(shape))
    out_qk = jax.ShapeDtypeStruct((MLA_HEADS, T, MLA_QK), BF16)
    out_v = jax.ShapeDtypeStruct((MLA_HEADS, T, MLA_V), BF16)
    return pl.pallas_call(
        _mla_prep_kernel,
        grid=(T // tm,),
        in_specs=[
            pl.BlockSpec((tm, D_MODEL), lambda i: (i, 0)),
            pl.BlockSpec((tm, 1), lambda i: (i, 0)),
            full((1, D_MODEL)),
            full((D_MODEL, 4 * LANES)),
            full((1, Q_LORA)),
            full((1, KV_LORA)),
            full((Q_LORA, MLA_HEADS * 2 * LANES)),
            full((KV_LORA, MLA_HEADS * 2 * LANES)),
            full((1, LANES)),
            full((8, LANES)),
            full((8, LANES)),
        ],
        out_specs=[
            pl.BlockSpec((MLA_HEADS, tm, MLA_QK), lambda i: (0, i, 0)),
            pl.BlockSpec((MLA_HEADS, tm, MLA_QK), lambda i: (0, i, 0)),
            pl.BlockSpec((MLA_HEADS, tm, MLA_V), lambda i: (0, i, 0)),
        ],
        out_shape=[out_qk, out_qk, out_v],
        compiler_params=_params("parallel"),
        name="mla_prep",
    )(x2, pos2, attn_norm[None, :], w_in_ext, q_a_norm[None, :], kv_a_norm[None, :],
      w_uq_ext, w_ukv_b, invf, _mla_tables(q_norm), _mla_tables(k_norm))


def _mla_attn_kernel(q_ref, k_ref, v_ref, o_ref, *, seq, tq):
    row_c = lax.broadcasted_iota(jnp.int32, (tq, tq), 0) // CHUNK
    col_c = lax.broadcasted_iota(jnp.int32, (tq, tq), 1) // CHUNK
    diag_mask = col_c <= row_c
    for qi in range(seq // tq):
        lo, hi = qi * tq, (qi + 1) * tq
        for hh in range(MLA_HEADS_PER_STEP):
            q = q_ref[hh, lo:hi, :]
            s_d = jnp.where(diag_mask, _dot_nt(q, k_ref[hh, lo:hi, :]), NEG_INF)
            m = jnp.max(s_d, axis=-1, keepdims=True)
            if qi > 0:
                s_o = _dot_nt(q, k_ref[hh, :lo, :])
                m = jnp.maximum(m, jnp.max(s_o, axis=-1, keepdims=True))
            p_d = jnp.exp(s_d - m)
            l = jnp.sum(p_d, axis=-1, keepdims=True)
            o = _dot(p_d.astype(BF16), v_ref[hh, lo:hi, :])
            if qi > 0:
                p_o = jnp.exp(s_o - m)
                l = l + jnp.sum(p_o, axis=-1, keepdims=True)
                o = o + _dot(p_o.astype(BF16), v_ref[hh, :lo, :])
            o_ref[lo:hi, hh * MLA_V:(hh + 1) * MLA_V] = (o / l).astype(BF16)


def _mla_attn(q, k, v, batch, seq):
    T = batch * seq
    hps = MLA_HEADS_PER_STEP
    kern = functools.partial(_mla_attn_kernel, seq=seq, tq=MLA_Q_TILE)
    return pl.pallas_call(
        kern,
        grid=(MLA_HEADS // hps, batch),
        in_specs=[
            pl.BlockSpec((hps, seq, MLA_QK), lambda h, b: (h, b, 0)),
            pl.BlockSpec((hps, seq, MLA_QK), lambda h, b: (h, b, 0)),
            pl.BlockSpec((hps, seq, MLA_V), lambda h, b: (h, b, 0)),
        ],
        out_specs=pl.BlockSpec((seq, hps * MLA_V), lambda h, b: (b, h)),
        out_shape=jax.ShapeDtypeStruct((T, MLA_HEADS * MLA_V), BF16),
        compiler_params=_params("parallel", "parallel"),
        name="mla_attn",
    )(q, k, v)


def _proj_dense_ffn_kernel(x_ref, o_ref, wo_ref, gn_ref, wg_ref, wu_ref, wd_ref, out_ref):
    x1 = x_ref[...] + _dot(o_ref[...], wo_ref[...])
    h = _rms(x1, gn_ref[...]).astype(BF16)
    a = (_silu(_dot(h, wg_ref[...])) * _dot(h, wu_ref[...])).astype(BF16)
    out_ref[...] = x1 + _dot(a, wd_ref[...])


def _proj_dense_ffn(x2, o2, w_o, ffn_norm, w_gate, w_up, w_down):
    T = x2.shape[0]
    ff = w_gate.shape[1]
    tm = ROW_TILE
    const = lambda shape: pl.BlockSpec(shape, lambda i: (0, 0), pipeline_mode=pl.Buffered(1))
    row = pl.BlockSpec((tm, D_MODEL), lambda i: (i, 0))
    return pl.pallas_call(
        _proj_dense_ffn_kernel,
        grid=(T // tm,),
        in_specs=[row, row, const((D_MODEL, D_MODEL)), const((1, D_MODEL)), const((D_MODEL, ff)),
                  const((D_MODEL, ff)), const((ff, D_MODEL))],
        out_specs=row,
        out_shape=jax.ShapeDtypeStruct((T, D_MODEL), F32),
        compiler_params=_params("parallel"),
        name="proj_dense_ffn",
    )(x2, o2, w_o.astype(BF16), ffn_norm[None, :], w_gate.astype(BF16), w_up.astype(BF16),
      w_down.astype(BF16))


def _ca_prep_kernel(x_ref, gn_ref, w_ref, q_out, k_out, v_out):
    h = _rms(x_ref[...], gn_ref[...]).astype(BF16)
    qkv = _dot(h, w_ref[...])
    q_out[...] = qkv[:, :D_MODEL].astype(BF16)
    k_out[...] = qkv[:, D_MODEL:2 * D_MODEL].astype(BF16)
    v_out[...] = qkv[:, 2 * D_MODEL:].astype(BF16)


def _ca_prep(x2, attn_norm, w_qkv):
    T = x2.shape[0]
    tm = ROW_TILE
    full = lambda shape: pl.BlockSpec(shape, lambda i: (0,) * len(shape))
    out = jax.ShapeDtypeStruct((T, D_MODEL), BF16)
    row = pl.BlockSpec((tm, D_MODEL), lambda i: (i, 0))
    return pl.pallas_call(
        _ca_prep_kernel,
        grid=(T // tm,),
        in_specs=[row, full((1, D_MODEL)), full((D_MODEL, 3 * D_MODEL))],
        out_specs=[row, row, row],
        out_shape=[out, out, out],
        compiler_params=_params("parallel"),
        name="ca_prep",
    )(x2, attn_norm[None, :], w_qkv.astype(BF16))


def _band_attn_kernel(q_ref, k_ref, v_ref, w_ref, g_ref, wg_ref, wu_ref, wd_ref,
                      o_ref, wg_out, wu_out, wd_out, tab_ref, qn_ref, kn_ref, *, seq, tq):
    wg_out[...] = wg_ref[...].astype(BF16)
    wu_out[...] = wu_ref[...].astype(BF16)
    wd_out[...] = wd_ref[...].astype(BF16)

    @pl.when(pl.program_id(1) == 0)
    def _():
        row = lax.broadcasted_iota(jnp.int32, (tq, BAND_KEYS), 0)
        col = lax.broadcasted_iota(jnp.int32, (tq, BAND_KEYS), 1)
        shift = CHUNK.bit_length() - 1
        for hh in range(2):
            for var in range(3):
                w = w_ref[0, 3 * hh + var:3 * hh + var + 1, :]
                t = pltpu.roll(jnp.broadcast_to(w, (tq, BAND_ROLL)), 0, 1, stride=1, stride_axis=0)
                gap = lax.shift_right_logical(row + var * tq, shift) - lax.shift_right_logical(col, shift)
                tab_ref[hh, var] = jnp.where((gap >= 0) & (gap <= LEFT_CHUNKS), t[:, :BAND_KEYS], NEG_INF)

    lane = lax.broadcasted_iota(jnp.int32, (1, LANES), 1)
    head0 = lane < CA_HEAD_DIM

    def head_normed(x_ref, gain):
        x = x_ref[...].astype(F32)
        sq = x * x
        ss0 = jnp.sum(jnp.where(head0, sq, 0.0), axis=-1, keepdims=True)
        ss1 = jnp.sum(jnp.where(head0, 0.0, sq), axis=-1, keepdims=True)
        r = jnp.where(head0, lax.rsqrt(ss0 * (1.0 / CA_HEAD_DIM) + RMS_EPS),
                      lax.rsqrt(ss1 * (1.0 / CA_HEAD_DIM) + RMS_EPS))
        return (x * r * gain).astype(BF16)

    qn_ref[...] = head_normed(q_ref, g_ref[0:1, :])
    kn_ref[...] = head_normed(k_ref, g_ref[1:2, :])

    for qi in range(seq // tq):
        var = min(qi, 2)
        ks = max(qi - 2, 0) * tq
        q = qn_ref[qi * tq:(qi + 1) * tq, :]
        k = kn_ref[ks:ks + BAND_KEYS, :]
        v = v_ref[ks:ks + BAND_KEYS, :]
        zero = jnp.zeros_like(q)
        qq = jnp.concatenate([jnp.where(head0, q, zero), jnp.where(head0, zero, q)], axis=0)
        s = _dot_nt(qq, k)
        ps, ls = [], []
        for hh in range(2):
            sh = s[hh * tq:(hh + 1) * tq] + tab_ref[hh, var]
            p = jnp.exp(sh - jnp.max(sh, axis=-1, keepdims=True))
            ls.append(jnp.sum(p, axis=-1, keepdims=True))
            ps.append(p.astype(BF16))
        oo = _dot(jnp.concatenate(ps, axis=0), v)
        o = jnp.where(head0, oo[:tq] / ls[0], oo[tq:] / ls[1])
        o_ref[qi * tq:(qi + 1) * tq, :] = o.astype(BF16)


def _band_bias_rows(rel_bias):
    tq = ATTN_Q_TILE
    rb = rel_bias.astype(F32)
    r_max = 3 * tq - 1
    n_far = r_max - MAX_REL
    n_m = r_max + BAND_KEYS
    n_neg = n_m - n_far - rb.shape[1]
    heads = rb.shape[0]
    m = jnp.concatenate([jnp.broadcast_to(rb[:, -1:], (heads, n_far)), rb[:, ::-1],
                         jnp.broadcast_to(rb[:, :1], (heads, n_neg))], axis=1)
    rows = []
    for var in range(3):
        base = r_max - var * tq
        rows.append(jnp.concatenate([m[:, base:base + BAND_KEYS], jnp.zeros((heads, 1), F32),
                                     m[:, base - (tq - 1):base]], axis=1))
    w = jnp.stack(rows, axis=1).reshape(heads // 2, 6, BAND_ROLL)
    return jnp.concatenate([w, jnp.zeros((heads // 2, 2, BAND_ROLL), F32)], axis=1)


def _band_attn(q, k, v, q_norm, k_norm, rel_bias, batch, seq, expert_weights):
    T = batch * seq
    tq = ATTN_Q_TILE
    n_steps = (CA_HEADS // 2) * batch
    gains = jnp.stack([jnp.tile(q_norm, 2) * CA_HEAD_DIM ** -0.5, jnp.tile(k_norm, 2)]
                      + [jnp.zeros((LANES,), F32)] * 6)
    kern = functools.partial(_band_attn_kernel, seq=seq, tq=tq)
    slab = pl.BlockSpec((seq, LANES), lambda hp, b: (b, hp))
    sliced = [w.reshape(n_steps, w.shape[0] * w.shape[1] // n_steps, w.shape[2]) for w in expert_weights]
    w_specs = [pl.BlockSpec((1,) + w.shape[1:], lambda hp, b: (hp * batch + b, 0, 0)) for w in sliced]
    outs = pl.pallas_call(
        kern,
        grid=(CA_HEADS // 2, batch),
        in_specs=[slab, slab, slab, pl.BlockSpec((1, 8, BAND_ROLL), lambda hp, b: (hp, 0, 0)),
                  pl.BlockSpec((8, LANES), lambda hp, b: (0, 0))] + w_specs,
        out_specs=[slab] + w_specs,
        out_shape=[jax.ShapeDtypeStruct((T, D_MODEL), BF16)]
                  + [jax.ShapeDtypeStruct(w.shape, BF16) for w in sliced],
        scratch_shapes=[pltpu.VMEM((2, 3, tq, BAND_KEYS), F32), pltpu.VMEM((seq, LANES), BF16),
                        pltpu.VMEM((seq, LANES), BF16)],
        compiler_params=_params("arbitrary", "arbitrary"),
        name="band_attn",
    )(q, k, v, _band_bias_rows(rel_bias), gains, *sliced)
    return outs[0], [wb.reshape(w.shape) for wb, w in zip(outs[1:], expert_weights)]


def _proj_router_kernel(x_ref, o_ref, wo_ref, gn_ref, rhi_ref, rlo_ref, x1_out, h_out, idx_out, w_out):
    rows = x_ref.shape[0] // ROUTER_SUBBLOCKS
    for blk in range(ROUTER_SUBBLOCKS):
        sl = slice(blk * rows, (blk + 1) * rows)
        x1 = x_ref[sl, :] + _dot(o_ref[sl, :], wo_ref[...])
        x1_out[sl, :] = x1
        h = _rms(x1, gn_ref[...])
        h_out[sl, :] = h
        h_hi = h.astype(BF16)
        h_lo = (h - h_hi.astype(F32)).astype(BF16)
        logits = _dot(h_hi, rhi_ref[...]) + (_dot(h_lo, rhi_ref[...]) + _dot(h_hi, rlo_ref[...]))

        lane = lax.broadcasted_iota(jnp.int32, logits.shape, 1)
        lane_f = lane.astype(F32)
        lg = jnp.where(lane < N_EXPERTS, logits, NEG_INF)
        m1 = jnp.max(lg, axis=-1, keepdims=True)
        i1 = jnp.min(jnp.where(lg == m1, lane_f, float(LANES)), axis=-1, keepdims=True)
        lg2 = jnp.where(lane_f == i1, NEG_INF, lg)
        m2 = jnp.max(lg2, axis=-1, keepdims=True)
        i2 = jnp.min(jnp.where(lg2 == m2, lane_f, float(LANES)), axis=-1, keepdims=True)
        e = jnp.exp(m2 - m1)
        w1 = 1.0 / (1.0 + e)
        w2 = e / (1.0 + e)
        idx_out[sl, :] = jnp.where(lane == 0, i1, jnp.where(lane == 1, i2, 0.0)).astype(jnp.int32)
        w_out[sl, :] = jnp.where(lane == 0, w1, jnp.where(lane == 1, w2, 0.0))


def _proj_router(x2, o2, w_o, ffn_norm, router):
    T = x2.shape[0]
    tm = ROUTER_TILE
    r_pad = jnp.zeros((D_MODEL, LANES), F32).at[:, :N_EXPERTS].set(router)
    r_hi = r_pad.astype(BF16)
    r_lo = (r_pad - r_hi.astype(F32)).astype(BF16)
    full = lambda shape: pl.BlockSpec(shape, lambda i: (0,) * len(shape))
    row = pl.BlockSpec((tm, D_MODEL), lambda i: (i, 0))
    small = pl.BlockSpec((tm, LANES), lambda i: (i, 0))
    return pl.pallas_call(
        _proj_router_kernel,
        grid=(T // tm,),
        in_specs=[row, row, full((D_MODEL, D_MODEL)), full((1, D_MODEL)), full((D_MODEL, LANES)),
                  full((D_MODEL, LANES))],
        out_specs=[row, row, small, small],
        out_shape=[jax.ShapeDtypeStruct((T, D_MODEL), F32), jax.ShapeDtypeStruct((T, D_MODEL), F32),
                   jax.ShapeDtypeStruct((T, LANES), jnp.int32), jax.ShapeDtypeStruct((T, LANES), F32)],
        compiler_params=_params("parallel"),
        name="proj_router",
    )(x2, o2, w_o.astype(BF16), ffn_norm[None, :], r_hi, r_lo)


def _moe_plan(top_idx, tm):
    T = top_idx.shape[0]
    n_assign = T * TOP_K
    n_tiles = n_assign // tm + N_EXPERTS
    plane = T + tm
    e_flat = top_idx.reshape(1, n_assign)
    experts = jnp.arange(N_EXPERTS, dtype=jnp.int32)
    onehot = (e_flat == experts[:, None]).astype(jnp.int32)
    csum = jnp.cumsum(onehot, axis=1)
    counts = csum[:, -1]
    tiles_per = (counts + tm - 1) // tm
    tile_end = jnp.cumsum(tiles_per)
    tile_start = tile_end - tiles_per
    group_start = jnp.cumsum(counts) - counts
    order = jnp.sum(onehot * (csum - 1 + group_start[:, None]), axis=0)
    a = jnp.arange(n_assign, dtype=jnp.int32)
    _, sorted_slot = lax.sort((order, (a % TOP_K) * plane + a // TOP_K), num_keys=1)
    sorted_slot = jnp.concatenate([sorted_slot, jnp.zeros((tm,), jnp.int32)])

    tile_id = jnp.arange(n_tiles, dtype=jnp.int32)
    tile_valid = (tile_id < tile_end[-1]).astype(jnp.int32)
    tile_expert = jnp.minimum(jnp.sum((tile_id[:, None] >= tile_end[None, :]).astype(jnp.int32), axis=1),
                              N_EXPERTS - 1)
    of_tile = lambda per_expert: jnp.sum((tile_expert[:, None] == experts[None, :]) * per_expert[None, :], axis=1)
    rows_before = (tile_id - of_tile(tile_start)) * tm
    n_real = jnp.clip(of_tile(counts) - rows_before, 0, tm) * tile_valid
    first = jnp.clip(of_tile(group_start) + rows_before, 0, n_assign)
    return (sorted_slot.astype(jnp.int32), tile_expert.astype(jnp.int32), tile_valid, first.astype(jnp.int32),
            n_real.astype(jnp.int32))


def _moe_ffn_kernel(te_ref, tv_ref, first_ref, nreal_ref, slot_ref, h_hbm, wg_ref, wu_ref, wd_ref, y_hbm,
                    xbuf, y0, y1, acc_ref, hs_ref, gsem, ssem, *, tm, n_tok):
    i = pl.program_id(0)
    f = pl.program_id(1)
    s = lax.rem(i, 2)
    plane = n_tok + tm
    valid = tv_ref[i] > 0
    prev_valid = (i > 0) & (tv_ref[jnp.maximum(i - 1, 0)] > 0)
    ys = (y0, y1)

    def gather_rows(tile, par):
        base, n_real = first_ref[tile], nreal_ref[tile]

        def copy(j):
            p = slot_ref[base + j]
            tok = jnp.where(j < n_real, jnp.where(p >= plane, p - plane, p), n_tok - 1)
            return pltpu.make_async_copy(h_hbm.at[pl.ds(tok, 1), :], xbuf.at[par, pl.ds(j, 1), :], gsem.at[par])
        return copy

    def scatter_rows(tile, par):
        base, n_real = first_ref[tile], nreal_ref[tile]

        def copy(j):
            p = jnp.where(j < n_real, slot_ref[base + j], par * plane + n_tok + j)
            return pltpu.make_async_copy(ys[par].at[pl.ds(j, 1), :], y_hbm.at[pl.ds(p, 1), :], ssem.at[par])
        return copy

    def wait_gather(slot):
        pltpu.make_async_copy(h_hbm.at[pl.ds(0, tm), :], xbuf.at[0], gsem.at[slot]).wait()

    def wait_scatter(slot):
        pltpu.make_async_copy(y0, y_hbm.at[pl.ds(0, tm), :], ssem.at[slot]).wait()

    def partial_ffn():
        h = hs_ref[...]
        a = (_silu(_dot(h, wg_ref[0])) * _dot(h, wu_ref[0])).astype(BF16)
        return _dot(a, wd_ref[0])

    @pl.when((f == 0) & (i == 0))
    def _():
        y1[...] = jnp.zeros((tm, D_MODEL), F32)
        for k in range(TOP_K):
            cp = pltpu.make_async_copy(y1, y_hbm.at[pl.ds(k * plane + n_tok, tm), :], ssem.at[1])
            cp.start()
            cp.wait()

        first_tile = gather_rows(0, 0)

        def first(j, carry):
            first_tile(j).start()
            return carry
        lax.fori_loop(0, tm, first, 0)

    @pl.when((f == 0) & ((i == 0) | prev_valid))
    def _():
        wait_gather(s)

    @pl.when((f == 1) & (i >= 2) & prev_valid)
    def _():
        wait_scatter(s)

    @pl.when(valid & (f == 0))
    def _():
        hs_ref[...] = xbuf[s].astype(BF16)
        next_tile = gather_rows(i + 1, 1 - s)
        for j in range(tm):
            next_tile(j).start()
        acc_ref[...] = partial_ffn()

    for par in range(2):
        @pl.when(valid & (f == 1) & (i > 0) & (s == par))
        def _(par=par):
            prev_tile = scatter_rows(i - 1, 1 - par)
            for j in range(tm):
                prev_tile(j).start()
            ys[par][...] = acc_ref[...] + partial_ffn()

        @pl.when(jnp.logical_not(valid) & (f == 1) & prev_valid & (s == par))
        def _(par=par):
            prev_tile = scatter_rows(i - 1, 1 - par)

            def last(j, carry):
                prev_tile(j).start()
                return carry
            lax.fori_loop(0, tm, last, 0)
            wait_scatter(1 - par)

    @pl.when(valid & (f == 1) & (i == 0))
    def _():
        y0[...] = acc_ref[...] + partial_ffn()


def _moe_ffn(h2, sorted_slot, tile_expert, tile_valid, tile_first, tile_n_real, w_gate, w_up, w_down, tm):
    n_tok = h2.shape[0]
    n_tiles = tile_expert.shape[0]
    ff = w_gate.shape[2]
    tf = MOE_FF_TILE
    nf = ff // tf
    assert nf == 2
    ff_blk = lambda i, f, tv: f * tv[i] + (nf - 1) * (1 - tv[i])
    kern = functools.partial(_moe_ffn_kernel, tm=tm, n_tok=n_tok)
    return pl.pallas_call(
        kern,
        grid_spec=pltpu.PrefetchScalarGridSpec(
            num_scalar_prefetch=5,
            grid=(n_tiles, nf),
            in_specs=[
                pl.BlockSpec(memory_space=pl.ANY),
                pl.BlockSpec((1, D_MODEL, tf), lambda i, f, te, tv, *_: (te[i], 0, ff_blk(i, f, tv))),
                pl.BlockSpec((1, D_MODEL, tf), lambda i, f, te, tv, *_: (te[i], 0, ff_blk(i, f, tv))),
                pl.BlockSpec((1, tf, D_MODEL), lambda i, f, te, tv, *_: (te[i], ff_blk(i, f, tv), 0)),
            ],
            out_specs=pl.BlockSpec(memory_space=pl.ANY),
            scratch_shapes=[pltpu.VMEM((2, tm, D_MODEL), F32)] + [pltpu.VMEM((tm, D_MODEL), F32)] * 3 + [
                pltpu.VMEM((tm, D_MODEL), BF16), pltpu.SemaphoreType.DMA((2,)), pltpu.SemaphoreType.DMA((2,))],
        ),
        out_shape=jax.ShapeDtypeStruct((TOP_K * (n_tok + tm), D_MODEL), F32),
        compiler_params=_params("arbitrary", "arbitrary"),
        name="moe_ffn",
    )(tile_expert, tile_valid, tile_first, tile_n_real, sorted_slot, h2, w_gate.astype(BF16),
      w_up.astype(BF16), w_down.astype(BF16))


def _moe_sum_kernel(x1_ref, wts_ref, y0_ref, y1_ref, out_ref):
    wts = wts_ref[...]
    out_ref[...] = x1_ref[...] + (wts[:, 0:1] * y0_ref[...] + wts[:, 1:2] * y1_ref[...])


def _moe_sum(x1, wts, y, tm):
    T = x1.shape[0]
    plane_blocks = (T + tm) // tm
    row = pl.BlockSpec((tm, D_MODEL), lambda i: (i, 0))
    return pl.pallas_call(
        _moe_sum_kernel,
        grid=(T // tm,),
        in_specs=[row, pl.BlockSpec((tm, LANES), lambda i: (i, 0)), row,
                  pl.BlockSpec((tm, D_MODEL), lambda i: (plane_blocks + i, 0))],
        out_specs=row,
        out_shape=jax.ShapeDtypeStruct((T, D_MODEL), F32),
        compiler_params=_params("parallel"),
        name="moe_sum",
    )(x1, wts, y, y)


def _mla_dense_layer(x2, pos2, batch, seq, attn_norm, w_in, q_a_norm, kv_a_norm, w_uq, w_ukv, q_norm,
                     k_norm, w_o, ffn_norm, w_gate, w_up, w_down):
    q, k, v = _mla_prep(x2, pos2, attn_norm, w_in, q_a_norm, kv_a_norm, w_uq, w_ukv, q_norm, k_norm)
    o = _mla_attn(q, k, v, batch, seq)
    return _proj_dense_ffn(x2, o, w_o, ffn_norm, w_gate, w_up, w_down)


def _band_moe_layer(x2, batch, seq, attn_norm, w_qkv, q_norm, k_norm, rel_bias, w_o, ffn_norm, router,
                    w_gate, w_up, w_down):
    q, k, v = _ca_prep(x2, attn_norm, w_qkv)
    o, (w_gate, w_up, w_down) = _band_attn(q, k, v, q_norm, k_norm, rel_bias, batch, seq,
                                           (w_gate, w_up, w_down))
    x1, h, idx, wts = _proj_router(x2, o, w_o, ffn_norm, router)
    tm = ROW_TILE
    sorted_slot, tile_expert, tile_valid, tile_first, tile_n_real = _moe_plan(idx[:, :TOP_K], tm)
    y = _moe_ffn(h, sorted_slot, tile_expert, tile_valid, tile_first, tile_n_real, w_gate, w_up, w_down, tm)
    return _moe_sum(x1, wts, y, tm)


def kernel(x, positions, l0_attn_norm, l0_mla_w_in, l0_mla_q_a_norm, l0_mla_kv_a_norm, l0_mla_w_uq, l0_mla_w_ukv, l0_mla_q_norm, l0_mla_k_norm, l0_mla_w_o, l0_ffn_norm, l0_ffn_w_gate, l0_ffn_w_up, l0_ffn_w_down, l1_attn_norm, l1_ca_w_qkv, l1_ca_q_norm, l1_ca_k_norm, l1_ca_rel_bias, l1_ca_w_o, l1_ffn_norm, l1_moe_router, l1_moe_w_gate, l1_moe_w_up, l1_moe_w_down, l2_attn_norm, l2_mla_w_in, l2_mla_q_a_norm, l2_mla_kv_a_norm, l2_mla_w_uq, l2_mla_w_ukv, l2_mla_q_norm, l2_mla_k_norm, l2_mla_w_o, l2_ffn_norm, l2_ffn_w_gate, l2_ffn_w_up, l2_ffn_w_down, l3_attn_norm, l3_ca_w_qkv, l3_ca_q_norm, l3_ca_k_norm, l3_ca_rel_bias, l3_ca_w_o, l3_ffn_norm, l3_moe_router, l3_moe_w_gate, l3_moe_w_up, l3_moe_w_down):
    batch, seq, d = x.shape
    assert d == D_MODEL and seq % ATTN_Q_TILE == 0 and seq >= BAND_KEYS
    assert (batch * seq) % ROW_TILE == 0
    x2 = x.reshape(batch * seq, d)
    pos2 = positions.reshape(batch * seq, 1)
    x2 = _mla_dense_layer(x2, pos2, batch, seq, l0_attn_norm, l0_mla_w_in, l0_mla_q_a_norm, l0_mla_kv_a_norm,
                          l0_mla_w_uq, l0_mla_w_ukv, l0_mla_q_norm, l0_mla_k_norm, l0_mla_w_o, l0_ffn_norm,
                          l0_ffn_w_gate, l0_ffn_w_up, l0_ffn_w_down)
    x2 = _band_moe_layer(x2, batch, seq, l1_attn_norm, l1_ca_w_qkv, l1_ca_q_norm, l1_ca_k_norm, l1_ca_rel_bias,
                         l1_ca_w_o, l1_ffn_norm, l1_moe_router, l1_moe_w_gate, l1_moe_w_up, l1_moe_w_down)
    x2 = _mla_dense_layer(x2, pos2, batch, seq, l2_attn_norm, l2_mla_w_in, l2_mla_q_a_norm, l2_mla_kv_a_norm,
                          l2_mla_w_uq, l2_mla_w_ukv, l2_mla_q_norm, l2_mla_k_norm, l2_mla_w_o, l2_ffn_norm,
                          l2_ffn_w_gate, l2_ffn_w_up, l2_ffn_w_down)
    x2 = _band_moe_layer(x2, batch, seq, l3_attn_norm, l3_ca_w_qkv, l3_ca_q_norm, l3_ca_k_norm, l3_ca_rel_bias,
                         l3_ca_w_o, l3_ffn_norm, l3_moe_router, l3_moe_w_gate, l3_moe_w_up, l3_moe_w_down)
    return x2.reshape(batch, seq, d)
```

```python
import functools

import numpy as np
import jax
import jax.numpy as jnp
from jax import lax
from jax.experimental import pallas as pl
from jax.experimental.pallas import tpu as pltpu

F32 = jnp.float32
BF16 = jnp.bfloat16

D_MODEL = 1024
CHUNK = 64
RMS_EPS = 1e-6

MLA_HEADS = 8
MLA_NOPE = 128
MLA_ROPE = 64
MLA_V = 128
MLA_QK = MLA_NOPE + MLA_ROPE
Q_LORA = 256
KV_LORA = 128
ROPE_THETA = 10000.0

CA_HEADS = 16
CA_HEAD_DIM = 64
LEFT_CHUNKS = 8
MAX_REL = 256

N_EXPERTS = 8
TOP_K = 2

LANES = 128
SUBLANES = 8
VMEM_LIMIT = 56 * 1024 * 1024

ROW_TILE = 512
MLA_PREP_TILE = 1024
ROUTER_TILE = 1024
ROUTER_SUBBLOCKS = 4
MLA_Q_TILE = 512
MLA_HEADS_PER_STEP = 2
ATTN_Q_TILE = 256
BAND_KEYS = 3 * ATTN_Q_TILE
BAND_ROLL = ATTN_Q_TILE + BAND_KEYS
MOE_FF_TILE = 1792
DMA_THREADS = 2
NEG_INF = float("-inf")


def _dot(a, b):
    return jnp.dot(a, b, preferred_element_type=F32)


def _dot_nt(a, b):
    return lax.dot_general(a, b, (((1,), (1,)), ((), ())), preferred_element_type=F32)


def _rms(x, g):
    return x * lax.rsqrt(jnp.mean(x * x, axis=-1, keepdims=True) + RMS_EPS) * g


def _silu(g):
    return g / (1.0 + jnp.exp(-g))


def _params(*sem):
    return pltpu.CompilerParams(dimension_semantics=sem, vmem_limit_bytes=VMEM_LIMIT)


def _mla_prep_kernel(x_ref, pos_ref, gn_ref, win_ref, qan_ref, kvan_ref, wuq_ref, wukv_ref,
                     invf_ref, qtab_ref, ktab_ref, q_out, k_out, v_out):
    h = _rms(x_ref[...], gn_ref[...]).astype(BF16)
    a = _dot(h, win_ref[...])
    qn = _rms(a[:, :Q_LORA], qan_ref[...]).astype(BF16)
    kvn = _rms(a[:, Q_LORA:Q_LORA + KV_LORA], kvan_ref[...]).astype(BF16)
    kr2 = a[:, Q_LORA + KV_LORA:]
    q_all = _dot(qn, wuq_ref[...])
    kv_all = _dot(kvn, wukv_ref[...])

    ang = pos_ref[...].astype(F32) * invf_ref[...]
    cos = jnp.cos(ang)
    sin = jnp.sin(ang)
    lane = lax.broadcasted_iota(jnp.int32, (1, LANES), 1)
    rope_lane = lane < MLA_ROPE

    q_gn, q_ga, q_gb = qtab_ref[0:1, :], qtab_ref[1:2, :], qtab_ref[2:3, :]
    k_gn, k_ga, k_gb = ktab_ref[0:1, :], ktab_ref[1:2, :], ktab_ref[2:3, :]
    q_cos, q_sin = cos * q_ga, sin * q_gb
    k_rot = kr2 * (cos * k_ga) + pltpu.roll(kr2, MLA_ROPE, 1) * (sin * k_gb)
    kr_ss = jnp.sum(jnp.where(rope_lane, kr2 * kr2, 0.0), axis=-1, keepdims=True)
    scale = MLA_QK ** -0.5

    slabs = []
    for hd in range(MLA_HEADS):
        lo = hd * 2 * LANES
        qa = q_all[:, lo:lo + LANES]
        qb = q_all[:, lo + LANES:lo + 2 * LANES]
        kn = kv_all[:, lo:lo + LANES]
        ss = jnp.sum(qa * qa + jnp.where(rope_lane, qb * qb, 0.0), axis=-1, keepdims=True)
        ssk = jnp.sum(kn * kn, axis=-1, keepdims=True) + kr_ss
        slabs.append((qa, qb, kn, ss, ssk))
    for hd, (qa, qb, kn, ss, ssk) in enumerate(slabs):
        lo = hd * 2 * LANES
        r = lax.rsqrt(ss * (1.0 / MLA_QK) + RMS_EPS) * scale
        rk = lax.rsqrt(ssk * (1.0 / MLA_QK) + RMS_EPS)
        q_rot = (qb * q_cos + pltpu.roll(qb, MLA_ROPE, 1) * q_sin) * r
        q_out[hd, :, :MLA_NOPE] = (qa * q_gn * r).astype(BF16)
        q_out[hd, :, MLA_NOPE:] = q_rot[:, :MLA_ROPE].astype(BF16)
        k_out[hd, :, :MLA_NOPE] = (kn * k_gn * rk).astype(BF16)
        k_out[hd, :, MLA_NOPE:] = (k_rot * rk)[:, :MLA_ROPE].astype(BF16)
        v_out[hd] = kv_all[:, lo + LANES:lo + 2 * LANES].astype(BF16)


def _rope_partner():
    d = np.arange(MLA_ROPE)
    half = MLA_ROPE // 2
    partner = np.where(d < half, d + half, d - half)
    sign = np.where(d < half, -1.0, 1.0).astype(np.float32)
    return partner, sign


def _mla_tables(g):
    partner, sign = _rope_partner()
    zeros = jnp.zeros((LANES - MLA_ROPE,), F32)
    g_rope = g[MLA_NOPE:]
    rows = [g[:MLA_NOPE],
            jnp.concatenate([g_rope, zeros]),
            jnp.concatenate([g_rope[partner] * sign, zeros])]
    rows += [jnp.zeros((LANES,), F32)] * 5
    return jnp.stack(rows)


def _mla_prep(x2, pos2, attn_norm, w_in, q_a_norm, kv_a_norm, w_uq, w_ukv, q_norm, k_norm):
    T = x2.shape[0]
    tm = MLA_PREP_TILE
    partner, _ = _rope_partner()
    rope0 = Q_LORA + KV_LORA
    w_in_ext = jnp.concatenate([w_in, w_in[:, rope0 + partner]], axis=1).astype(BF16)
    wq = w_uq.reshape(Q_LORA, MLA_HEADS, MLA_QK)
    w_uq_ext = jnp.concatenate([wq, wq[:, :, MLA_NOPE + partner]], axis=2)
    w_uq_ext = w_uq_ext.reshape(Q_LORA, MLA_HEADS * 2 * LANES).astype(BF16)
    w_ukv_b = w_ukv.astype(BF16)
    inv = 1.0 / (ROPE_THETA ** (jnp.arange(0, MLA_ROPE, 2, dtype=F32) / MLA_ROPE))
    invf = jnp.concatenate([inv, inv, jnp.zeros((LANES - MLA_ROPE,), F32)])[None, :]

    full = lambda shape: pl.BlockSpec(shape, lambda i: (0,) * len(shape))
    out_qk = jax.ShapeDtypeStruct((MLA_HEADS, T, MLA_QK), BF16)
    out_v = jax.ShapeDtypeStruct((MLA_HEADS, T, MLA_V), BF16)
    return pl.pallas_call(
        _mla_prep_kernel,
        grid=(T // tm,),
        in_specs=[
            pl.BlockSpec((tm, D_MODEL), lambda i: (i, 0)),
            pl.BlockSpec((tm, 1), lambda i: (i, 0)),
            full((1, D_MODEL)),
            full((D_MODEL, 4 * LANES)),
            full((1, Q_LORA)),
            full((1, KV_LORA)),
            full((Q_LORA, MLA_HEADS * 2 * LANES)),
            full((KV_LORA, MLA_HEADS * 2 * LANES)),
            full((1, LANES)),
            full((8, LANES)),
            full((8, LANES)),
        ],
        out_specs=[
            pl.BlockSpec((MLA_HEADS, tm, MLA_QK), lambda i: (0, i, 0)),
            pl.BlockSpec((MLA_HEADS, tm, MLA_QK), lambda i: (0, i, 0)),
            pl.BlockSpec((MLA_HEADS, tm, MLA_V), lambda i: (0, i, 0)),
        ],
        out_shape=[out_qk, out_qk, out_v],
        compiler_params=_params("parallel"),
        name="mla_prep",
    )(x2, pos2, attn_norm[None, :], w_in_ext, q_a_norm[None, :], kv_a_norm[None, :],
      w_uq_ext, w_ukv_b, invf, _mla_tables(q_norm), _mla_tables(k_norm))


def _mla_attn_kernel(q_ref, k_ref, v_ref, o_ref, *, seq, tq):
    row_c = lax.broadcasted_iota(jnp.int32, (tq, tq), 0) // CHUNK
    col_c = lax.broadcasted_iota(jnp.int32, (tq, tq), 1) // CHUNK
    diag_mask = col_c <= row_c
    for qi in range(seq // tq):
        lo, hi = qi * tq, (qi + 1) * tq
        for hh in range(MLA_HEADS_PER_STEP):
            q = q_ref[hh, lo:hi, :]
            s_d = jnp.where(diag_mask, _dot_nt(q, k_ref[hh, lo:hi, :]), NEG_INF)
            m = jnp.max(s_d, axis=-1, keepdims=True)
            if qi > 0:
                s_o = _dot_nt(q, k_ref[hh, :lo, :])
                m = jnp.maximum(m, jnp.max(s_o, axis=-1, keepdims=True))
            p_d = jnp.exp(s_d - m)
            l = jnp.sum(p_d, axis=-1, keepdims=True)
            o = _dot(p_d.astype(BF16), v_ref[hh, lo:hi, :])
            if qi > 0:
                p_o = jnp.exp(s_o - m)
                l = l + jnp.sum(p_o, axis=-1, keepdims=True)
                o = o + _dot(p_o.astype(BF16), v_ref[hh, :lo, :])
            o_ref[lo:hi, hh * MLA_V:(hh + 1) * MLA_V] = (o / l).astype(BF16)


def _mla_attn(q, k, v, batch, seq):
    T = batch * seq
    hps = MLA_HEADS_PER_STEP
    kern = functools.partial(_mla_attn_kernel, seq=seq, tq=MLA_Q_TILE)
    return pl.pallas_call(
        kern,
        grid=(MLA_HEADS // hps, batch),
        in_specs=[
            pl.BlockSpec((hps, seq, MLA_QK), lambda h, b: (h, b, 0)),
            pl.BlockSpec((hps, seq, MLA_QK), lambda h, b: (h, b, 0)),
            pl.BlockSpec((hps, seq, MLA_V), lambda h, b: (h, b, 0)),
        ],
        out_specs=pl.BlockSpec((seq, hps * MLA_V), lambda h, b: (b, h)),
        out_shape=jax.ShapeDtypeStruct((T, MLA_HEADS * MLA_V), BF16),
        compiler_params=_params("parallel", "parallel"),
        name="mla_attn",
    )(q, k, v)


def _proj_dense_ffn_kernel(x_ref, o_ref, wo_ref, gn_ref, wg_ref, wu_ref, wd_ref, out_ref):
    x1 = x_ref[...] + _dot(o_ref[...], wo_ref[...])
    h = _rms(x1, gn_ref[...]).astype(BF16)
    a = (_silu(_dot(h, wg_ref[...])) * _dot(h, wu_ref[...])).astype(BF16)
    out_ref[...] = x1 + _dot(a, wd_ref[...])


def _proj_dense_ffn(x2, o2, w_o, ffn_norm, w_gate, w_up, w_down):
    T = x2.shape[0]
    ff = w_gate.shape[1]
    tm = ROW_TILE
    const = lambda shape: pl.BlockSpec(shape, lambda i: (0, 0), pipeline_mode=pl.Buffered(1))
    row = pl.BlockSpec((tm, D_MODEL), lambda i: (i, 0))
    return pl.pallas_call(
        _proj_dense_ffn_kernel,
        grid=(T // tm,),
        in_specs=[row, row, const((D_MODEL, D_MODEL)), const((1, D_MODEL)), const((D_MODEL, ff)),
                  const((D_MODEL, ff)), const((ff, D_MODEL))],
        out_specs=row,
        out_shape=jax.ShapeDtypeStruct((T, D_MODEL), F32),
        compiler_params=_params("parallel"),
        name="proj_dense_ffn",
    )(x2, o2, w_o.astype(BF16), ffn_norm[None, :], w_gate.astype(BF16), w_up.astype(BF16),
      w_down.astype(BF16))


def _ca_prep_kernel(x_ref, gn_ref, w_ref, q_out, k_out, v_out):
    h = _rms(x_ref[...], gn_ref[...]).astype(BF16)
    qkv = _dot(h, w_ref[...])
    q_out[...] = qkv[:, :D_MODEL].astype(BF16)
    k_out[...] = qkv[:, D_MODEL:2 * D_MODEL].astype(BF16)
    v_out[...] = qkv[:, 2 * D_MODEL:].astype(BF16)


def _ca_prep(x2, attn_norm, w_qkv):
    T = x2.shape[0]
    tm = ROW_TILE
    full = lambda shape: pl.BlockSpec(shape, lambda i: (0,) * len(shape))
    out = jax.ShapeDtypeStruct((T, D_MODEL), BF16)
    row = pl.BlockSpec((tm, D_MODEL), lambda i: (i, 0))
    return pl.pallas_call(
        _ca_prep_kernel,
        grid=(T // tm,),
        in_specs=[row, full((1, D_MODEL)), full((D_MODEL, 3 * D_MODEL))],
        out_specs=[row, row, row],
        out_shape=[out, out, out],
        compiler_params=_params("parallel"),
        name="ca_prep",
    )(x2, attn_norm[None, :], w_qkv.astype(BF16))


def _band_attn_kernel(q_ref, k_ref, v_ref, w_ref, g_ref, wg_ref, wu_ref, wd_ref,
                      o_ref, wg_out, wu_out, wd_out, tab_ref, qn_ref, kn_ref, *, seq, tq):
    wg_out[...] = wg_ref[...].astype(BF16)
    wu_out[...] = wu_ref[...].astype(BF16)
    wd_out[...] = wd_ref[...].astype(BF16)

    @pl.when(pl.program_id(1) == 0)
    def _():
        row = lax.broadcasted_iota(jnp.int32, (tq, BAND_KEYS), 0)
        col = lax.broadcasted_iota(jnp.int32, (tq, BAND_KEYS), 1)
        shift = CHUNK.bit_length() - 1
        for hh in range(2):
            for var in range(3):
                w = w_ref[0, 3 * hh + var:3 * hh + var + 1, :]
                t = pltpu.roll(jnp.broadcast_to(w, (tq, BAND_ROLL)), 0, 1, stride=1, stride_axis=0)
                gap = lax.shift_right_logical(row + var * tq, shift) - lax.shift_right_logical(col, shift)
                tab_ref[hh, var] = jnp.where((gap >= 0) & (gap <= LEFT_CHUNKS), t[:, :BAND_KEYS], NEG_INF)

    lane = lax.broadcasted_iota(jnp.int32, (1, LANES), 1)
    head0 = lane < CA_HEAD_DIM

    def head_normed(x_ref, gain):
        x = x_ref[...].astype(F32)
        sq = x * x
        ss0 = jnp.sum(jnp.where(head0, sq, 0.0), axis=-1, keepdims=True)
        ss1 = jnp.sum(jnp.where(head0, 0.0, sq), axis=-1, keepdims=True)
        r = jnp.where(head0, lax.rsqrt(ss0 * (1.0 / CA_HEAD_DIM) + RMS_EPS),
                      lax.rsqrt(ss1 * (1.0 / CA_HEAD_DIM) + RMS_EPS))
        return (x * r * gain).astype(BF16)

    qn_ref[...] = head_normed(q_ref, g_ref[0:1, :])
    kn_ref[...] = head_normed(k_ref, g_ref[1:2, :])

    for qi in range(seq // tq):
        var = min(qi, 2)
        ks = max(qi - 2, 0) * tq
        q = qn_ref[qi * tq:(qi + 1) * tq, :]
        k = kn_ref[ks:ks + BAND_KEYS, :]
        v = v_ref[ks:ks + BAND_KEYS, :]
        zero = jnp.zeros_like(q)
        qq = jnp.concatenate([jnp.where(head0, q, zero), jnp.where(head0, zero, q)], axis=0)
        s = _dot_nt(qq, k)
        ps, ls = [], []
        for hh in range(2):
            sh = s[hh * tq:(hh + 1) * tq] + tab_ref[hh, var]
            p = jnp.exp(sh - jnp.max(sh, axis=-1, keepdims=True))
            ls.append(jnp.sum(p, axis=-1, keepdims=True))
            ps.append(p.astype(BF16))
        oo = _dot(jnp.concatenate(ps, axis=0), v)
        o = jnp.where(head0, oo[:tq] / ls[0], oo[tq:] / ls[1])
        o_ref[qi * tq:(qi + 1) * tq, :] = o.astype(BF16)


def _band_bias_rows(rel_bias):
    tq = ATTN_Q_TILE
    rb = rel_bias.astype(F32)
    r_max = 3 * tq - 1
    n_far = r_max - MAX_REL
    n_m = r_max + BAND_KEYS
    n_neg = n_m - n_far - rb.shape[1]
    heads = rb.shape[0]
    m = jnp.concatenate([jnp.broadcast_to(rb[:, -1:], (heads, n_far)), rb[:, ::-1],
                         jnp.broadcast_to(rb[:, :1], (heads, n_neg))], axis=1)
    rows = []
    for var in range(3):
        base = r_max - var * tq
        rows.append(jnp.concatenate([m[:, base:base + BAND_KEYS], jnp.zeros((heads, 1), F32),
                                     m[:, base - (tq - 1):base]], axis=1))
    w = jnp.stack(rows, axis=1).reshape(heads // 2, 6, BAND_ROLL)
    return jnp.concatenate([w, jnp.zeros((heads // 2, 2, BAND_ROLL), F32)], axis=1)


def _band_attn(q, k, v, q_norm, k_norm, rel_bias, batch, seq, expert_weights):
    T = batch * seq
    tq = ATTN_Q_TILE
    n_steps = (CA_HEADS // 2) * batch
    gains = jnp.stack([jnp.tile(q_norm, 2) * CA_HEAD_DIM ** -0.5, jnp.tile(k_norm, 2)]
                      + [jnp.zeros((LANES,), F32)] * 6)
    kern = functools.partial(_band_attn_kernel, seq=seq, tq=tq)
    slab = pl.BlockSpec((seq, LANES), lambda hp, b: (b, hp))
    sliced = [w.reshape(n_steps, w.shape[0] * w.shape[1] // n_steps, w.shape[2]) for w in expert_weights]
    w_specs = [pl.BlockSpec((1,) + w.shape[1:], lambda hp, b: (hp * batch + b, 0, 0)) for w in sliced]
    outs = pl.pallas_call(
        kern,
        grid=(CA_HEADS // 2, batch),
        in_specs=[slab, slab, slab, pl.BlockSpec((1, 8, BAND_ROLL), lambda hp, b: (hp, 0, 0)),
                  pl.BlockSpec((8, LANES), lambda hp, b: (0, 0))] + w_specs,
        out_specs=[slab] + w_specs,
        out_shape=[jax.ShapeDtypeStruct((T, D_MODEL), BF16)]
                  + [jax.ShapeDtypeStruct(w.shape, BF16) for w in sliced],
        scratch_shapes=[pltpu.VMEM((2, 3, tq, BAND_KEYS), F32), pltpu.VMEM((seq, LANES), BF16),
                        pltpu.VMEM((seq, LANES), BF16)],
        compiler_params=_params("arbitrary", "arbitrary"),
        name="band_attn",
    )(q, k, v, _band_bias_rows(rel_bias), gains, *sliced)
    return outs[0], [wb.reshape(w.shape) for wb, w in zip(outs[1:], expert_weights)]


def _proj_router_kernel(x_ref, o_ref, wo_ref, gn_ref, rhi_ref, rlo_ref, x1_out, h_out, idx_out, w_out):
    rows = x_ref.shape[0] // ROUTER_SUBBLOCKS
    for blk in range(ROUTER_SUBBLOCKS):
        sl = slice(blk * rows, (blk + 1) * rows)
        x1 = x_ref[sl, :] + _dot(o_ref[sl, :], wo_ref[...])
        x1_out[sl, :] = x1
        h = _rms(x1, gn_ref[...])
        h_out[sl, :] = h
        h_hi = h.astype(BF16)
        h_lo = (h - h_hi.astype(F32)).astype(BF16)
        logits = _dot(h_hi, rhi_ref[...]) + (_dot(h_lo, rhi_ref[...]) + _dot(h_hi, rlo_ref[...]))

        lane = lax.broadcasted_iota(jnp.int32, logits.shape, 1)
        lane_f = lane.astype(F32)
        lg = jnp.where(lane < N_EXPERTS, logits, NEG_INF)
        m1 = jnp.max(lg, axis=-1, keepdims=True)
        i1 = jnp.min(jnp.where(lg == m1, lane_f, float(LANES)), axis=-1, keepdims=True)
        lg2 = jnp.where(lane_f == i1, NEG_INF, lg)
        m2 = jnp.max(lg2, axis=-1, keepdims=True)
        i2 = jnp.min(jnp.where(lg2 == m2, lane_f, float(LANES)), axis=-1, keepdims=True)
        e = jnp.exp(m2 - m1)
        w1 = 1.0 / (1.0 + e)
        w2 = e / (1.0 + e)
        idx_out[sl, :] = jnp.where(lane == 0, i1, jnp.where(lane == 1, i2, 0.0)).astype(jnp.int32)
        w_out[sl, :] = jnp.where(lane == 0, w1, jnp.where(lane == 1, w2, 0.0))


def _proj_router(x2, o2, w_o, ffn_norm, router):
    T = x2.shape[0]
    tm = ROUTER_TILE
    r_pad = jnp.zeros((D_MODEL, LANES), F32).at[:, :N_EXPERTS].set(router)
    r_hi = r_pad.astype(BF16)
    r_lo = (r_pad - r_hi.astype(F32)).astype(BF16)
    full = lambda shape: pl.BlockSpec(shape, lambda i: (0,) * len(shape))
    row = pl.BlockSpec((tm, D_MODEL), lambda i: (i, 0))
    small = pl.BlockSpec((tm, LANES), lambda i: (i, 0))
    return pl.pallas_call(
        _proj_router_kernel,
        grid=(T // tm,),
        in_specs=[row, row, full((D_MODEL, D_MODEL)), full((1, D_MODEL)), full((D_MODEL, LANES)),
                  full((D_MODEL, LANES))],
        out_specs=[row, row, small, small],
        out_shape=[jax.ShapeDtypeStruct((T, D_MODEL), F32), jax.ShapeDtypeStruct((T, D_MODEL), F32),
                   jax.ShapeDtypeStruct((T, LANES), jnp.int32), jax.ShapeDtypeStruct((T, LANES), F32)],
        compiler_params=_params("parallel"),
        name="proj_router",
    )(x2, o2, w_o.astype(BF16), ffn_norm[None, :], r_hi, r_lo)


def _moe_plan(top_idx, tm):
    T = top_idx.shape[0]
    n_assign = T * TOP_K
    n_tiles = n_assign // tm + N_EXPERTS
    plane = T + tm
    e_flat = top_idx.reshape(1, n_assign)
    experts = jnp.arange(N_EXPERTS, dtype=jnp.int32)
    onehot = (e_flat == experts[:, None]).astype(jnp.int32)
    csum = jnp.cumsum(onehot, axis=1)
    counts = csum[:, -1]
    tiles_per = (counts + tm - 1) // tm
    tile_end = jnp.cumsum(tiles_per)
    tile_start = tile_end - tiles_per
    group_start = jnp.cumsum(counts) - counts
    order = jnp.sum(onehot * (csum - 1 + group_start[:, None]), axis=0)
    a = jnp.arange(n_assign, dtype=jnp.int32)
    _, sorted_slot = lax.sort((order, (a % TOP_K) * plane + a // TOP_K), num_keys=1)
    sorted_slot = jnp.concatenate([sorted_slot, jnp.zeros((tm,), jnp.int32)])

    tile_id = jnp.arange(n_tiles, dtype=jnp.int32)
    tile_valid = (tile_id < tile_end[-1]).astype(jnp.int32)
    tile_expert = jnp.minimum(jnp.sum((tile_id[:, None] >= tile_end[None, :]).astype(jnp.int32), axis=1),
                              N_EXPERTS - 1)
    of_tile = lambda per_expert: jnp.sum((tile_expert[:, None] == experts[None, :]) * per_expert[None, :], axis=1)
    rows_before = (tile_id - of_tile(tile_start)) * tm
    n_real = jnp.clip(of_tile(counts) - rows_before, 0, tm) * tile_valid
    first = jnp.clip(of_tile(group_start) + rows_before, 0, n_assign)
    return (sorted_slot.astype(jnp.int32), tile_expert.astype(jnp.int32), tile_valid, first.astype(jnp.int32),
            n_real.astype(jnp.int32))


def _moe_ffn_kernel(te_ref, tv_ref, first_ref, nreal_ref, slot_ref, h_hbm, wg_ref, wu_ref, wd_ref, y_hbm,
                    xbuf, y0, y1, acc_ref, hs_ref, gsem, ssem, *, tm, n_tok):
    i = pl.program_id(0)
    f = pl.program_id(1)
    s = lax.rem(i, 2)
    plane = n_tok + tm
    valid = tv_ref[i] > 0
    prev_valid = (i > 0) & (tv_ref[jnp.maximum(i - 1, 0)] > 0)
    ys = (y0, y1)

    def gather_rows(tile, par):
        base, n_real = first_ref[tile], nreal_ref[tile]

        def copy(j):
            p = slot_ref[base + j]
            tok = jnp.where(j < n_real, jnp.where(p >= plane, p - plane, p), n_tok - 1)
            return pltpu.make_async_copy(h_hbm.at[pl.ds(tok, 1), :], xbuf.at[par, pl.ds(j, 1), :], gsem.at[par])
        return copy

    def scatter_rows(tile, par):
        base, n_real = first_ref[tile], nreal_ref[tile]

        def copy(j):
            p = jnp.where(j < n_real, slot_ref[base + j], par * plane + n_tok + j)
            return pltpu.make_async_copy(ys[par].at[pl.ds(j, 1), :], y_hbm.at[pl.ds(p, 1), :], ssem.at[par])
        return copy

    def wait_gather(slot):
        pltpu.make_async_copy(h_hbm.at[pl.ds(0, tm), :], xbuf.at[0], gsem.at[slot]).wait()

    def wait_scatter(slot):
        pltpu.make_async_copy(y0, y_hbm.at[pl.ds(0, tm), :], ssem.at[slot]).wait()

    def partial_ffn():
        h = hs_ref[...]
        a = (_silu(_dot(h, wg_ref[0])) * _dot(h, wu_ref[0])).astype(BF16)
        return _dot(a, wd_ref[0])

    @pl.when((f == 0) & (i == 0))
    def _():
        y1[...] = jnp.zeros((tm, D_MODEL), F32)
        for k in range(TOP_K):
            cp = pltpu.make_async_copy(y1, y_hbm.at[pl.ds(k * plane + n_tok, tm), :], ssem.at[1])
            cp.start()
            cp.wait()

        first_tile = gather_rows(0, 0)

        def first(j, carry):
            first_tile(j).start()
            return carry
        lax.fori_loop(0, tm, first, 0)

    @pl.when((f == 0) & ((i == 0) | prev_valid))
    def _():
        wait_gather(s)

    @pl.when((f == 1) & (i >= 2) & prev_valid)
    def _():
        wait_scatter(s)

    @pl.when(valid & (f == 0))
    def _():
        hs_ref[...] = xbuf[s].astype(BF16)
        next_tile = gather_rows(i + 1, 1 - s)
        for j in range(tm):
            next_tile(j).start(priority=j % DMA_THREADS)
        acc_ref[...] = partial_ffn()

    for par in range(2):
        @pl.when(valid & (f == 1) & (i > 0) & (s == par))
        def _(par=par):
            prev_tile = scatter_rows(i - 1, 1 - par)
            for j in range(tm):
                prev_tile(j).start(priority=j % DMA_THREADS)
            ys[par][...] = acc_ref[...] + partial_ffn()

        @pl.when(jnp.logical_not(valid) & (f == 1) & prev_valid & (s == par))
        def _(par=par):
            prev_tile = scatter_rows(i - 1, 1 - par)

            def last(j, carry):
                prev_tile(j).start()
                return carry
            lax.fori_loop(0, tm, last, 0)
            wait_scatter(1 - par)

    @pl.when(valid & (f == 1) & (i == 0))
    def _():
        y0[...] = acc_ref[...] + partial_ffn()


def _moe_ffn(h2, sorted_slot, tile_expert, tile_valid, tile_first, tile_n_real, w_gate, w_up, w_down, tm):
    n_tok = h2.shape[0]
    n_tiles = tile_expert.shape[0]
    ff = w_gate.shape[2]
    tf = MOE_FF_TILE
    nf = ff // tf
    assert nf == 2
    ff_blk = lambda i, f, tv: f * tv[i] + (nf - 1) * (1 - tv[i])
    kern = functools.partial(_moe_ffn_kernel, tm=tm, n_tok=n_tok)
    return pl.pallas_call(
        kern,
        grid_spec=pltpu.PrefetchScalarGridSpec(
            num_scalar_prefetch=5,
            grid=(n_tiles, nf),
            in_specs=[
                pl.BlockSpec(memory_space=pl.ANY),
                pl.BlockSpec((1, D_MODEL, tf), lambda i, f, te, tv, *_: (te[i], 0, ff_blk(i, f, tv))),
                pl.BlockSpec((1, D_MODEL, tf), lambda i, f, te, tv, *_: (te[i], 0, ff_blk(i, f, tv))),
                pl.BlockSpec((1, tf, D_MODEL), lambda i, f, te, tv, *_: (te[i], ff_blk(i, f, tv), 0)),
            ],
            out_specs=pl.BlockSpec(memory_space=pl.ANY),
            scratch_shapes=[pltpu.VMEM((2, tm, D_MODEL), F32)] + [pltpu.VMEM((tm, D_MODEL), F32)] * 3 + [
                pltpu.VMEM((tm, D_MODEL), BF16), pltpu.SemaphoreType.DMA((2,)), pltpu.SemaphoreType.DMA((2,))],
        ),
        out_shape=jax.ShapeDtypeStruct((TOP_K * (n_tok + tm), D_MODEL), F32),
        compiler_params=_params("arbitrary", "arbitrary"),
        name="moe_ffn",
    )(tile_expert, tile_valid, tile_first, tile_n_real, sorted_slot, h2, w_gate.astype(BF16),
      w_up.astype(BF16), w_down.astype(BF16))


def _moe_sum_kernel(x1_ref, wts_ref, y0_ref, y1_ref, out_ref):
    wts = wts_ref[...]
    out_ref[...] = x1_ref[...] + (wts[:, 0:1] * y0_ref[...] + wts[:, 1:2] * y1_ref[...])


def _moe_sum(x1, wts, y, tm):
    T = x1.shape[0]
    plane_blocks = (T + tm) // tm
    row = pl.BlockSpec((tm, D_MODEL), lambda i: (i, 0))
    return pl.pallas_call(
        _moe_sum_kernel,
        grid=(T // tm,),
        in_specs=[row, pl.BlockSpec((tm, LANES), lambda i: (i, 0)), row,
                  pl.BlockSpec((tm, D_MODEL), lambda i: (plane_blocks + i, 0))],
        out_specs=row,
        out_shape=jax.ShapeDtypeStruct((T, D_MODEL), F32),
        compiler_params=_params("parallel"),
        name="moe_sum",
    )(x1, wts, y, y)


def _mla_dense_layer(x2, pos2, batch, seq, attn_norm, w_in, q_a_norm, kv_a_norm, w_uq, w_ukv, q_norm,
                     k_norm, w_o, ffn_norm, w_gate, w_up, w_down):
    q, k, v = _mla_prep(x2, pos2, attn_norm, w_in, q_a_norm, kv_a_norm, w_uq, w_ukv, q_norm, k_norm)
    o = _mla_attn(q, k, v, batch, seq)
    return _proj_dense_ffn(x2, o, w_o, ffn_norm, w_gate, w_up, w_down)


def _band_moe_layer(x2, batch, seq, attn_norm, w_qkv, q_norm, k_norm, rel_bias, w_o, ffn_norm, router,
                    w_gate, w_up, w_down):
    q, k, v = _ca_prep(x2, attn_norm, w_qkv)
    o, (w_gate, w_up, w_down) = _band_attn(q, k, v, q_norm, k_norm, rel_bias, batch, seq,
                                           (w_gate, w_up, w_down))
    x1, h, idx, wts = _proj_router(x2, o, w_o, ffn_norm, router)
    tm = ROW_TILE
    sorted_slot, tile_expert, tile_valid, tile_first, tile_n_real = _moe_plan(idx[:, :TOP_K], tm)
    y = _moe_ffn(h, sorted_slot, tile_expert, tile_valid, tile_first, tile_n_real, w_gate, w_up, w_down, tm)
    return _moe_sum(x1, wts, y, tm)


def kernel(x, positions, l0_attn_norm, l0_mla_w_in, l0_mla_q_a_norm, l0_mla_kv_a_norm, l0_mla_w_uq, l0_mla_w_ukv, l0_mla_q_norm, l0_mla_k_norm, l0_mla_w_o, l0_ffn_norm, l0_ffn_w_gate, l0_ffn_w_up, l0_ffn_w_down, l1_attn_norm, l1_ca_w_qkv, l1_ca_q_norm, l1_ca_k_norm, l1_ca_rel_bias, l1_ca_w_o, l1_ffn_norm, l1_moe_router, l1_moe_w_gate, l1_moe_w_up, l1_moe_w_down, l2_attn_norm, l2_mla_w_in, l2_mla_q_a_norm, l2_mla_kv_a_norm, l2_mla_w_uq, l2_mla_w_ukv, l2_mla_q_norm, l2_mla_k_norm, l2_mla_w_o, l2_ffn_norm, l2_ffn_w_gate, l2_ffn_w_up, l2_ffn_w_down, l3_attn_norm, l3_ca_w_qkv, l3_ca_q_norm, l3_ca_k_norm, l3_ca_rel_bias, l3_ca_w_o, l3_ffn_norm, l3_moe_router, l3_moe_w_gate, l3_moe_w_up, l3_moe_w_down):
    batch, seq, d = x.shape
    assert d == D_MODEL and seq % ATTN_Q_TILE == 0 and seq >= BAND_KEYS
    assert (batch * seq) % ROW_TILE == 0
    x2 = x.reshape(batch * seq, d)
    pos2 = positions.reshape(batch * seq, 1)
    x2 = _mla_dense_layer(x2, pos2, batch, seq, l0_attn_norm, l0_mla_w_in, l0_mla_q_a_norm, l0_mla_kv_a_norm,
                          l0_mla_w_uq, l0_mla_w_ukv, l0_mla_q_norm, l0_mla_k_norm, l0_mla_w_o, l0_ffn_norm,
                          l0_ffn_w_gate, l0_ffn_w_up, l0_ffn_w_down)
    x2 = _band_moe_layer(x2, batch, seq, l1_attn_norm, l1_ca_w_qkv, l1_ca_q_norm, l1_ca_k_norm, l1_ca_rel_bias,
                         l1_ca_w_o, l1_ffn_norm, l1_moe_router, l1_moe_w_gate, l1_moe_w_up, l1_moe_w_down)
    x2 = _mla_dense_layer(x2, pos2, batch, seq, l2_attn_norm, l2_mla_w_in, l2_mla_q_a_norm, l2_mla_kv_a_norm,
                          l2_mla_w_uq, l2_mla_w_ukv, l2_mla_q_norm, l2_mla_k_norm, l2_mla_w_o, l2_ffn_norm,
                          l2_ffn_w_gate, l2_ffn_w_up, l2_ffn_w_down)
    x2 = _band_moe_layer(x2, batch, seq, l3_attn_norm, l3_ca_w_qkv, l3_ca_q_norm, l3_ca_k_norm, l3_ca_rel_bias,
                         l3_ca_w_o, l3_ffn_norm, l3_moe_router, l3_moe_w_gate, l3_moe_w_up, l3_moe_w_down)
    return x2.reshape(batch, seq, d)
```

```python
import functools

import numpy as np
import jax
import jax.numpy as jnp
from jax import lax
from jax.experimental import pallas as pl
from jax.experimental.pallas import tpu as pltpu

F32 = jnp.float32
BF16 = jnp.bfloat16

D_MODEL = 1024
CHUNK = 64
RMS_EPS = 1e-6

MLA_HEADS = 8
MLA_NOPE = 128
MLA_ROPE = 64
MLA_V = 128
MLA_QK = MLA_NOPE + MLA_ROPE
Q_LORA = 256
KV_LORA = 128
ROPE_THETA = 10000.0

CA_HEADS = 16
CA_HEAD_DIM = 64
LEFT_CHUNKS = 8
MAX_REL = 256

N_EXPERTS = 8
TOP_K = 2

LANES = 128
BF16_ROWS = 16
VMEM_LIMIT = 56 * 1024 * 1024

ROW_TILE = 512
MLA_PREP_TILE = 1024
ROUTER_TILE = 1024
ROUTER_SUBBLOCKS = 4
MLA_Q_TILE = 512
MLA_HEADS_PER_STEP = 2
ATTN_Q_TILE = 256
BAND_KEYS = 3 * ATTN_Q_TILE
BAND_ROLL = ATTN_Q_TILE + BAND_KEYS
MOE_FF_TILE = 1792
NEG_INF = float("-inf")


def _dot(a, b):
    return jnp.dot(a, b, preferred_element_type=F32)


def _dot_nt(a, b):
    return lax.dot_general(a, b, (((1,), (1,)), ((), ())), preferred_element_type=F32)


def _rms(x, g):
    return x * lax.rsqrt(jnp.mean(x * x, axis=-1, keepdims=True) + RMS_EPS) * g


def _silu(g):
    return g / (1.0 + jnp.exp(-g))


def _params(*sem):
    return pltpu.CompilerParams(dimension_semantics=sem, vmem_limit_bytes=VMEM_LIMIT)


def _mla_prep_kernel(x_ref, pos_ref, gn_ref, win_ref, qan_ref, kvan_ref, wuq_ref, wukv_ref,
                     invf_ref, qtab_ref, ktab_ref, q_out, k_out, v_out):
    h = _rms(x_ref[...], gn_ref[...]).astype(BF16)
    a = _dot(h, win_ref[...])
    qn = _rms(a[:, :Q_LORA], qan_ref[...]).astype(BF16)
    kvn = _rms(a[:, Q_LORA:Q_LORA + KV_LORA], kvan_ref[...]).astype(BF16)
    kr2 = a[:, Q_LORA + KV_LORA:]
    q_all = _dot(qn, wuq_ref[...])
    kv_all = _dot(kvn, wukv_ref[...])

    ang = pos_ref[...].astype(F32) * invf_ref[...]
    cos = jnp.cos(ang)
    sin = jnp.sin(ang)
    lane = lax.broadcasted_iota(jnp.int32, (1, LANES), 1)
    rope_lane = lane < MLA_ROPE

    q_gn, q_ga, q_gb = qtab_ref[0:1, :], qtab_ref[1:2, :], qtab_ref[2:3, :]
    k_gn, k_ga, k_gb = ktab_ref[0:1, :], ktab_ref[1:2, :], ktab_ref[2:3, :]
    q_cos, q_sin = cos * q_ga, sin * q_gb
    k_rot = kr2 * (cos * k_ga) + pltpu.roll(kr2, MLA_ROPE, 1) * (sin * k_gb)
    kr_ss = jnp.sum(jnp.where(rope_lane, kr2 * kr2, 0.0), axis=-1, keepdims=True)
    scale = MLA_QK ** -0.5

    slabs = []
    for hd in range(MLA_HEADS):
        lo = hd * 2 * LANES
        qa = q_all[:, lo:lo + LANES]
        qb = q_all[:, lo + LANES:lo + 2 * LANES]
        kn = kv_all[:, lo:lo + LANES]
        ss = jnp.sum(qa * qa + jnp.where(rope_lane, qb * qb, 0.0), axis=-1, keepdims=True)
        ssk = jnp.sum(kn * kn, axis=-1, keepdims=True) + kr_ss
        slabs.append((qa, qb, kn, ss, ssk))
    for hd, (qa, qb, kn, ss, ssk) in enumerate(slabs):
        lo = hd * 2 * LANES
        r = lax.rsqrt(ss * (1.0 / MLA_QK) + RMS_EPS) * scale
        rk = lax.rsqrt(ssk * (1.0 / MLA_QK) + RMS_EPS)
        q_rot = (qb * q_cos + pltpu.roll(qb, MLA_ROPE, 1) * q_sin) * r
        q_out[hd, :, :MLA_NOPE] = (qa * q_gn * r).astype(BF16)
        q_out[hd, :, MLA_NOPE:] = q_rot[:, :MLA_ROPE].astype(BF16)
        k_out[hd, :, :MLA_NOPE] = (kn * k_gn * rk).astype(BF16)
        k_out[hd, :, MLA_NOPE:] = (k_rot * rk)[:, :MLA_ROPE].astype(BF16)
        v_out[hd] = kv_all[:, lo + LANES:lo + 2 * LANES].astype(BF16)


def _rope_partner():
    d = np.arange(MLA_ROPE)
    half = MLA_ROPE // 2
    partner = np.where(d < half, d + half, d - half)
    sign = np.where(d < half, -1.0, 1.0).astype(np.float32)
    return partner, sign


def _mla_tables(g):
    partner, sign = _rope_partner()
    zeros = jnp.zeros((LANES - MLA_ROPE,), F32)
    g_rope = g[MLA_NOPE:]
    rows = [g[:MLA_NOPE],
            jnp.concatenate([g_rope, zeros]),
            jnp.concatenate([g_rope[partner] * sign, zeros])]
    rows += [jnp.zeros((LANES,), F32)] * 5
    return jnp.stack(rows)


def _mla_prep(x2, pos2, attn_norm, w_in, q_a_norm, kv_a_norm, w_uq, w_ukv, q_norm, k_norm):
    T = x2.shape[0]
    tm = MLA_PREP_TILE
    partner, _ = _rope_partner()
    rope0 = Q_LORA + KV_LORA
    w_in_ext = jnp.concatenate([w_in, w_in[:, rope0 + partner]], axis=1).astype(BF16)
    wq = w_uq.reshape(Q_LORA, MLA_HEADS, MLA_QK)
    w_uq_ext = jnp.concatenate([wq, wq[:, :, MLA_NOPE + partner]], axis=2)
    w_uq_ext = w_uq_ext.reshape(Q_LORA, MLA_HEADS * 2 * LANES).astype(BF16)
    w_ukv_b = w_ukv.astype(BF16)
    inv = 1.0 / (ROPE_THETA ** (jnp.arange(0, MLA_ROPE, 2, dtype=F32) / MLA_ROPE))
    invf = jnp.concatenate([inv, inv, jnp.zeros((LANES - MLA_ROPE,), F32)])[None, :]

    full = lambda shape: pl.BlockSpec(shape, lambda i: (0,) * len(shape))
    out_qk = jax.ShapeDtypeStruct((MLA_HEADS, T, MLA_QK), BF16)
    out_v = jax.ShapeDtypeStruct((MLA_HEADS, T, MLA_V), BF16)
    return pl.pallas_call(
        _mla_prep_kernel,
        grid=(T // tm,),
        in_specs=[
            pl.BlockSpec((tm, D_MODEL), lambda i: (i, 0)),
            pl.BlockSpec((tm, 1), lambda i: (i, 0)),
            full((1, D_MODEL)),
            full((D_MODEL, 4 * LANES)),
            full((1, Q_LORA)),
            full((1, KV_LORA)),
            full((Q_LORA, MLA_HEADS * 2 * LANES)),
            full((KV_LORA, MLA_HEADS * 2 * LANES)),
            full((1, LANES)),
            full((8, LANES)),
            full((8, LANES)),
        ],
        out_specs=[
            pl.BlockSpec((MLA_HEADS, tm, MLA_QK), lambda i: (0, i, 0)),
            pl.BlockSpec((MLA_HEADS, tm, MLA_QK), lambda i: (0, i, 0)),
            pl.BlockSpec((MLA_HEADS, tm, MLA_V), lambda i: (0, i, 0)),
        ],
        out_shape=[out_qk, out_qk, out_v],
        compiler_params=_params("parallel"),
        name="mla_prep",
    )(x2, pos2, attn_norm[None, :], w_in_ext, q_a_norm[None, :], kv_a_norm[None, :],
      w_uq_ext, w_ukv_b, invf, _mla_tables(q_norm), _mla_tables(k_norm))


def _mla_attn_kernel(q_ref, k_ref, v_ref, *refs, seq, tq):
    n_w = (len(refs) - 1) // 2
    o_ref = refs[n_w]
    for w_in, w_out in zip(refs[:n_w], refs[n_w + 1:]):
        w_out[...] = w_in[...].astype(BF16)

    row_c = lax.broadcasted_iota(jnp.int32, (tq, tq), 0) // CHUNK
    col_c = lax.broadcasted_iota(jnp.int32, (tq, tq), 1) // CHUNK
    diag_mask = col_c <= row_c
    for qi in range(seq // tq):
        lo, hi = qi * tq, (qi + 1) * tq
        for hh in range(MLA_HEADS_PER_STEP):
            q = q_ref[hh, lo:hi, :]
            s_d = jnp.where(diag_mask, _dot_nt(q, k_ref[hh, lo:hi, :]), NEG_INF)
            m = jnp.max(s_d, axis=-1, keepdims=True)
            if qi > 0:
                s_o = _dot_nt(q, k_ref[hh, :lo, :])
                m = jnp.maximum(m, jnp.max(s_o, axis=-1, keepdims=True))
            p_d = jnp.exp(s_d - m)
            l = jnp.sum(p_d, axis=-1, keepdims=True)
            o = _dot(p_d.astype(BF16), v_ref[hh, lo:hi, :])
            if qi > 0:
                p_o = jnp.exp(s_o - m)
                l = l + jnp.sum(p_o, axis=-1, keepdims=True)
                o = o + _dot(p_o.astype(BF16), v_ref[hh, :lo, :])
            o_ref[lo:hi, hh * MLA_V:(hh + 1) * MLA_V] = (o / l).astype(BF16)


def _row_slices(w, n_steps):
    rows = w.shape[0]
    n_blocks = n_steps
    while rows % n_blocks or (rows // n_blocks) % BF16_ROWS:
        n_blocks //= 2
    return w.reshape(n_blocks, rows // n_blocks, w.shape[1]), n_steps // n_blocks


def _mla_attn(q, k, v, batch, seq, weights):
    T = batch * seq
    hps = MLA_HEADS_PER_STEP
    n_steps = (MLA_HEADS // hps) * batch
    kern = functools.partial(_mla_attn_kernel, seq=seq, tq=MLA_Q_TILE)
    sliced = [_row_slices(w, n_steps) for w in weights]
    w_specs = [pl.BlockSpec((1,) + w3.shape[1:], lambda h, b, rep=rep: ((h * batch + b) // rep, 0, 0))
               for w3, rep in sliced]
    outs = pl.pallas_call(
        kern,
        grid=(MLA_HEADS // hps, batch),
        in_specs=[
            pl.BlockSpec((hps, seq, MLA_QK), lambda h, b: (h, b, 0)),
            pl.BlockSpec((hps, seq, MLA_QK), lambda h, b: (h, b, 0)),
            pl.BlockSpec((hps, seq, MLA_V), lambda h, b: (h, b, 0)),
        ] + w_specs,
        out_specs=[pl.BlockSpec((seq, hps * MLA_V), lambda h, b: (b, h))] + w_specs,
        out_shape=[jax.ShapeDtypeStruct((T, MLA_HEADS * MLA_V), BF16)]
                  + [jax.ShapeDtypeStruct(w3.shape, BF16) for w3, _ in sliced],
        compiler_params=_params("arbitrary", "arbitrary"),
        name="mla_attn",
    )(q, k, v, *[w3 for w3, _ in sliced])
    return outs[0], [wb.reshape(w.shape) for wb, w in zip(outs[1:], weights)]


def _proj_dense_ffn_kernel(x_ref, o_ref, wo_ref, gn_ref, wg_ref, wu_ref, wd_ref, out_ref):
    x1 = x_ref[...] + _dot(o_ref[...], wo_ref[...])
    h = _rms(x1, gn_ref[...]).astype(BF16)
    a = (_silu(_dot(h, wg_ref[...])) * _dot(h, wu_ref[...])).astype(BF16)
    out_ref[...] = x1 + _dot(a, wd_ref[...])


def _proj_dense_ffn(x2, o2, w_o, ffn_norm, w_gate, w_up, w_down):
    T = x2.shape[0]
    ff = w_gate.shape[1]
    tm = ROW_TILE
    const = lambda shape: pl.BlockSpec(shape, lambda i: (0, 0), pipeline_mode=pl.Buffered(1))
    row = pl.BlockSpec((tm, D_MODEL), lambda i: (i, 0))
    return pl.pallas_call(
        _proj_dense_ffn_kernel,
        grid=(T // tm,),
        in_specs=[row, row, const((D_MODEL, D_MODEL)), const((1, D_MODEL)), const((D_MODEL, ff)),
                  const((D_MODEL, ff)), const((ff, D_MODEL))],
        out_specs=row,
        out_shape=jax.ShapeDtypeStruct((T, D_MODEL), F32),
        compiler_params=_params("parallel"),
        name="proj_dense_ffn",
    )(x2, o2, w_o.astype(BF16), ffn_norm[None, :], w_gate.astype(BF16), w_up.astype(BF16),
      w_down.astype(BF16))


def _ca_prep_kernel(x_ref, gn_ref, w_ref, q_out, k_out, v_out):
    h = _rms(x_ref[...], gn_ref[...]).astype(BF16)
    qkv = _dot(h, w_ref[...])
    q_out[...] = qkv[:, :D_MODEL].astype(BF16)
    k_out[...] = qkv[:, D_MODEL:2 * D_MODEL].astype(BF16)
    v_out[...] = qkv[:, 2 * D_MODEL:].astype(BF16)


def _ca_prep(x2, attn_norm, w_qkv):
    T = x2.shape[0]
    tm = ROW_TILE
    full = lambda shape: pl.BlockSpec(shape, lambda i: (0,) * len(shape))
    out = jax.ShapeDtypeStruct((T, D_MODEL), BF16)
    row = pl.BlockSpec((tm, D_MODEL), lambda i: (i, 0))
    return pl.pallas_call(
        _ca_prep_kernel,
        grid=(T // tm,),
        in_specs=[row, full((1, D_MODEL)), full((D_MODEL, 3 * D_MODEL))],
        out_specs=[row, row, row],
        out_shape=[out, out, out],
        compiler_params=_params("parallel"),
        name="ca_prep",
    )(x2, attn_norm[None, :], w_qkv.astype(BF16))


def _band_attn_kernel(q_ref, k_ref, v_ref, w_ref, g_ref, wg_ref, wu_ref, wd_ref,
                      o_ref, wg_out, wu_out, wd_out, tab_ref, qn_ref, kn_ref, *, seq, tq):
    wg_out[...] = wg_ref[...].astype(BF16)
    wu_out[...] = wu_ref[...].astype(BF16)
    wd_out[...] = wd_ref[...].astype(BF16)

    @pl.when(pl.program_id(1) == 0)
    def _():
        row = lax.broadcasted_iota(jnp.int32, (tq, BAND_KEYS), 0)
        col = lax.broadcasted_iota(jnp.int32, (tq, BAND_KEYS), 1)
        shift = CHUNK.bit_length() - 1
        for hh in range(2):
            for var in range(3):
                w = w_ref[0, 3 * hh + var:3 * hh + var + 1, :]
                t = pltpu.roll(jnp.broadcast_to(w, (tq, BAND_ROLL)), 0, 1, stride=1, stride_axis=0)
                gap = lax.shift_right_logical(row + var * tq, shift) - lax.shift_right_logical(col, shift)
                tab_ref[hh, var] = jnp.where((gap >= 0) & (gap <= LEFT_CHUNKS), t[:, :BAND_KEYS], NEG_INF)

    lane = lax.broadcasted_iota(jnp.int32, (1, LANES), 1)
    head0 = lane < CA_HEAD_DIM

    def head_normed(x_ref, gain):
        x = x_ref[...].astype(F32)
        sq = x * x
        ss0 = jnp.sum(jnp.where(head0, sq, 0.0), axis=-1, keepdims=True)
        ss1 = jnp.sum(jnp.where(head0, 0.0, sq), axis=-1, keepdims=True)
        r = jnp.where(head0, lax.rsqrt(ss0 * (1.0 / CA_HEAD_DIM) + RMS_EPS),
                      lax.rsqrt(ss1 * (1.0 / CA_HEAD_DIM) + RMS_EPS))
        return (x * r * gain).astype(BF16)

    qn_ref[...] = head_normed(q_ref, g_ref[0:1, :])
    kn_ref[...] = head_normed(k_ref, g_ref[1:2, :])

    for qi in range(seq // tq):
        var = min(qi, 2)
        ks = max(qi - 2, 0) * tq
        q = qn_ref[qi * tq:(qi + 1) * tq, :]
        k = kn_ref[ks:ks + BAND_KEYS, :]
        v = v_ref[ks:ks + BAND_KEYS, :]
        zero = jnp.zeros_like(q)
        qq = jnp.concatenate([jnp.where(head0, q, zero), jnp.where(head0, zero, q)], axis=0)
        s = _dot_nt(qq, k)
        ps, ls = [], []
        for hh in range(2):
            sh = s[hh * tq:(hh + 1) * tq] + tab_ref[hh, var]
            p = jnp.exp(sh - jnp.max(sh, axis=-1, keepdims=True))
            ls.append(jnp.sum(p, axis=-1, keepdims=True))
            ps.append(p.astype(BF16))
        oo = _dot(jnp.concatenate(ps, axis=0), v)
        o = jnp.where(head0, oo[:tq] / ls[0], oo[tq:] / ls[1])
        o_ref[qi * tq:(qi + 1) * tq, :] = o.astype(BF16)


def _band_bias_rows(rel_bias):
    tq = ATTN_Q_TILE
    rb = rel_bias.astype(F32)
    r_max = 3 * tq - 1
    n_far = r_max - MAX_REL
    n_m = r_max + BAND_KEYS
    n_neg = n_m - n_far - rb.shape[1]
    heads = rb.shape[0]
    m = jnp.concatenate([jnp.broadcast_to(rb[:, -1:], (heads, n_far)), rb[:, ::-1],
                         jnp.broadcast_to(rb[:, :1], (heads, n_neg))], axis=1)
    rows = []
    for var in range(3):
        base = r_max - var * tq
        rows.append(jnp.concatenate([m[:, base:base + BAND_KEYS], jnp.zeros((heads, 1), F32),
                                     m[:, base - (tq - 1):base]], axis=1))
    w = jnp.stack(rows, axis=1).reshape(heads // 2, 6, BAND_ROLL)
    return jnp.concatenate([w, jnp.zeros((heads // 2, 2, BAND_ROLL), F32)], axis=1)


def _band_attn(q, k, v, q_norm, k_norm, rel_bias, batch, seq, expert_weights):
    T = batch * seq
    tq = ATTN_Q_TILE
    n_steps = (CA_HEADS // 2) * batch
    gains = jnp.stack([jnp.tile(q_norm, 2) * CA_HEAD_DIM ** -0.5, jnp.tile(k_norm, 2)]
                      + [jnp.zeros((LANES,), F32)] * 6)
    kern = functools.partial(_band_attn_kernel, seq=seq, tq=tq)
    slab = pl.BlockSpec((seq, LANES), lambda hp, b: (b, hp))
    sliced = [w.reshape(n_steps, w.shape[0] * w.shape[1] // n_steps, w.shape[2]) for w in expert_weights]
    w_specs = [pl.BlockSpec((1,) + w.shape[1:], lambda hp, b: (hp * batch + b, 0, 0)) for w in sliced]
    outs = pl.pallas_call(
        kern,
        grid=(CA_HEADS // 2, batch),
        in_specs=[slab, slab, slab, pl.BlockSpec((1, 8, BAND_ROLL), lambda hp, b: (hp, 0, 0)),
                  pl.BlockSpec((8, LANES), lambda hp, b: (0, 0))] + w_specs,
        out_specs=[slab] + w_specs,
        out_shape=[jax.ShapeDtypeStruct((T, D_MODEL), BF16)]
                  + [jax.ShapeDtypeStruct(w.shape, BF16) for w in sliced],
        scratch_shapes=[pltpu.VMEM((2, 3, tq, BAND_KEYS), F32), pltpu.VMEM((seq, LANES), BF16),
                        pltpu.VMEM((seq, LANES), BF16)],
        compiler_params=_params("arbitrary", "arbitrary"),
        name="band_attn",
    )(q, k, v, _band_bias_rows(rel_bias), gains, *sliced)
    return outs[0], [wb.reshape(w.shape) for wb, w in zip(outs[1:], expert_weights)]


def _proj_router_kernel(x_ref, o_ref, wo_ref, gn_ref, rhi_ref, rlo_ref, x1_out, h_out, idx_out, w_out):
    rows = x_ref.shape[0] // ROUTER_SUBBLOCKS
    for blk in range(ROUTER_SUBBLOCKS):
        sl = slice(blk * rows, (blk + 1) * rows)
        x1 = x_ref[sl, :] + _dot(o_ref[sl, :], wo_ref[...])
        x1_out[sl, :] = x1
        h = _rms(x1, gn_ref[...])
        h_out[sl, :] = h
        h_hi = h.astype(BF16)
        h_lo = (h - h_hi.astype(F32)).astype(BF16)
        logits = _dot(h_hi, rhi_ref[...]) + (_dot(h_lo, rhi_ref[...]) + _dot(h_hi, rlo_ref[...]))

        lane = lax.broadcasted_iota(jnp.int32, logits.shape, 1)
        lane_f = lane.astype(F32)
        lg = jnp.where(lane < N_EXPERTS, logits, NEG_INF)
        m1 = jnp.max(lg, axis=-1, keepdims=True)
        i1 = jnp.min(jnp.where(lg == m1, lane_f, float(LANES)), axis=-1, keepdims=True)
        lg2 = jnp.where(lane_f == i1, NEG_INF, lg)
        m2 = jnp.max(lg2, axis=-1, keepdims=True)
        i2 = jnp.min(jnp.where(lg2 == m2, lane_f, float(LANES)), axis=-1, keepdims=True)
        e = jnp.exp(m2 - m1)
        w1 = 1.0 / (1.0 + e)
        w2 = e / (1.0 + e)
        idx_out[sl, :] = jnp.where(lane == 0, i1, jnp.where(lane == 1, i2, 0.0)).astype(jnp.int32)
        w_out[sl, :] = jnp.where(lane == 0, w1, jnp.where(lane == 1, w2, 0.0))


def _proj_router(x2, o2, w_o, ffn_norm, router):
    T = x2.shape[0]
    tm = ROUTER_TILE
    r_pad = jnp.zeros((D_MODEL, LANES), F32).at[:, :N_EXPERTS].set(router)
    r_hi = r_pad.astype(BF16)
    r_lo = (r_pad - r_hi.astype(F32)).astype(BF16)
    full = lambda shape: pl.BlockSpec(shape, lambda i: (0,) * len(shape))
    row = pl.BlockSpec((tm, D_MODEL), lambda i: (i, 0))
    small = pl.BlockSpec((tm, LANES), lambda i: (i, 0))
    return pl.pallas_call(
        _proj_router_kernel,
        grid=(T // tm,),
        in_specs=[row, row, full((D_MODEL, D_MODEL)), full((1, D_MODEL)), full((D_MODEL, LANES)),
                  full((D_MODEL, LANES))],
        out_specs=[row, row, small, small],
        out_shape=[jax.ShapeDtypeStruct((T, D_MODEL), F32), jax.ShapeDtypeStruct((T, D_MODEL), F32),
                   jax.ShapeDtypeStruct((T, LANES), jnp.int32), jax.ShapeDtypeStruct((T, LANES), F32)],
        compiler_params=_params("parallel"),
        name="proj_router",
    )(x2, o2, w_o.astype(BF16), ffn_norm[None, :], r_hi, r_lo)


def _moe_plan(top_idx, tm):
    T = top_idx.shape[0]
    n_assign = T * TOP_K
    n_tiles = n_assign // tm + N_EXPERTS
    plane = T + tm
    e_flat = top_idx.reshape(1, n_assign)
    experts = jnp.arange(N_EXPERTS, dtype=jnp.int32)
    onehot = (e_flat == experts[:, None]).astype(jnp.int32)
    csum = jnp.cumsum(onehot, axis=1)
    counts = csum[:, -1]
    tiles_per = (counts + tm - 1) // tm
    tile_end = jnp.cumsum(tiles_per)
    tile_start = tile_end - tiles_per
    group_start = jnp.cumsum(counts) - counts
    order = jnp.sum(onehot * (csum - 1 + group_start[:, None]), axis=0)
    a = jnp.arange(n_assign, dtype=jnp.int32)
    _, sorted_slot = lax.sort((order, (a % TOP_K) * plane + a // TOP_K), num_keys=1)
    sorted_slot = jnp.concatenate([sorted_slot, jnp.zeros((tm,), jnp.int32)])

    tile_id = jnp.arange(n_tiles, dtype=jnp.int32)
    tile_valid = (tile_id < tile_end[-1]).astype(jnp.int32)
    tile_expert = jnp.minimum(jnp.sum((tile_id[:, None] >= tile_end[None, :]).astype(jnp.int32), axis=1),
                              N_EXPERTS - 1)
    of_tile = lambda per_expert: jnp.sum((tile_expert[:, None] == experts[None, :]) * per_expert[None, :], axis=1)
    rows_before = (tile_id - of_tile(tile_start)) * tm
    n_real = jnp.clip(of_tile(counts) - rows_before, 0, tm) * tile_valid
    first = jnp.clip(of_tile(group_start) + rows_before, 0, n_assign)
    return (sorted_slot.astype(jnp.int32), tile_expert.astype(jnp.int32), tile_valid, first.astype(jnp.int32),
            n_real.astype(jnp.int32))


def _moe_ffn_kernel(te_ref, tv_ref, first_ref, nreal_ref, slot_ref, h_hbm, wg_ref, wu_ref, wd_ref, y_hbm,
                    xbuf, y0, y1, acc_ref, hs_ref, gsem, ssem, *, tm, n_tok):
    i = pl.program_id(0)
    f = pl.program_id(1)
    s = lax.rem(i, 2)
    plane = n_tok + tm
    valid = tv_ref[i] > 0
    prev_valid = (i > 0) & (tv_ref[jnp.maximum(i - 1, 0)] > 0)
    ys = (y0, y1)

    def gather_rows(tile, par):
        base, n_real = first_ref[tile], nreal_ref[tile]

        def copy(j):
            p = slot_ref[base + j]
            tok = jnp.where(j < n_real, jnp.where(p >= plane, p - plane, p), n_tok - 1)
            return pltpu.make_async_copy(h_hbm.at[pl.ds(tok, 1), :], xbuf.at[par, pl.ds(j, 1), :], gsem.at[par])
        return copy

    def scatter_rows(tile, par):
        base, n_real = first_ref[tile], nreal_ref[tile]

        def copy(j):
            p = jnp.where(j < n_real, slot_ref[base + j], par * plane + n_tok + j)
            return pltpu.make_async_copy(ys[par].at[pl.ds(j, 1), :], y_hbm.at[pl.ds(p, 1), :], ssem.at[par])
        return copy

    def wait_gather(slot):
        pltpu.make_async_copy(h_hbm.at[pl.ds(0, tm), :], xbuf.at[0], gsem.at[slot]).wait()

    def wait_scatter(slot):
        pltpu.make_async_copy(y0, y_hbm.at[pl.ds(0, tm), :], ssem.at[slot]).wait()

    def partial_ffn():
        h = hs_ref[...]
        a = (_silu(_dot(h, wg_ref[0])) * _dot(h, wu_ref[0])).astype(BF16)
        return _dot(a, wd_ref[0])

    @pl.when((f == 0) & (i == 0))
    def _():
        y1[...] = jnp.zeros((tm, D_MODEL), F32)
        for k in range(TOP_K):
            cp = pltpu.make_async_copy(y1, y_hbm.at[pl.ds(k * plane + n_tok, tm), :], ssem.at[1])
            cp.start()
            cp.wait()

        first_tile = gather_rows(0, 0)

        def first(j, carry):
            first_tile(j).start()
            return carry
        lax.fori_loop(0, tm, first, 0)

    @pl.when((f == 0) & ((i == 0) | prev_valid))
    def _():
        wait_gather(s)

    @pl.when((f == 1) & (i >= 2) & prev_valid)
    def _():
        wait_scatter(s)

    @pl.when(valid & (f == 0))
    def _():
        hs_ref[...] = xbuf[s].astype(BF16)
        next_tile = gather_rows(i + 1, 1 - s)
        for j in range(tm):
            next_tile(j).start()
        acc_ref[...] = partial_ffn()

    for par in range(2):
        @pl.when(valid & (f == 1) & (i > 0) & (s == par))
        def _(par=par):
            prev_tile = scatter_rows(i - 1, 1 - par)
            for j in range(tm):
                prev_tile(j).start()
            ys[par][...] = acc_ref[...] + partial_ffn()

        @pl.when(jnp.logical_not(valid) & (f == 1) & prev_valid & (s == par))
        def _(par=par):
            prev_tile = scatter_rows(i - 1, 1 - par)

            def last(j, carry):
                prev_tile(j).start()
                return carry
            lax.fori_loop(0, tm, last, 0)
            wait_scatter(1 - par)

    @pl.when(valid & (f == 1) & (i == 0))
    def _():
        y0[...] = acc_ref[...] + partial_ffn()


def _moe_ffn(h2, sorted_slot, tile_expert, tile_valid, tile_first, tile_n_real, w_gate, w_up, w_down, tm):
    n_tok = h2.shape[0]
    n_tiles = tile_expert.shape[0]
    ff = w_gate.shape[2]
    tf = MOE_FF_TILE
    nf = ff // tf
    assert nf == 2
    ff_blk = lambda i, f, tv: f * tv[i] + (nf - 1) * (1 - tv[i])
    kern = functools.partial(_moe_ffn_kernel, tm=tm, n_tok=n_tok)
    return pl.pallas_call(
        kern,
        grid_spec=pltpu.PrefetchScalarGridSpec(
            num_scalar_prefetch=5,
            grid=(n_tiles, nf),
            in_specs=[
                pl.BlockSpec(memory_space=pl.ANY),
                pl.BlockSpec((1, D_MODEL, tf), lambda i, f, te, tv, *_: (te[i], 0, ff_blk(i, f, tv))),
                pl.BlockSpec((1, D_MODEL, tf), lambda i, f, te, tv, *_: (te[i], 0, ff_blk(i, f, tv))),
                pl.BlockSpec((1, tf, D_MODEL), lambda i, f, te, tv, *_: (te[i], ff_blk(i, f, tv), 0)),
            ],
            out_specs=pl.BlockSpec(memory_space=pl.ANY),
            scratch_shapes=[pltpu.VMEM((2, tm, D_MODEL), F32)] + [pltpu.VMEM((tm, D_MODEL), F32)] * 3 + [
                pltpu.VMEM((tm, D_MODEL), BF16), pltpu.SemaphoreType.DMA((2,)), pltpu.SemaphoreType.DMA((2,))],
        ),
        out_shape=jax.ShapeDtypeStruct((TOP_K * (n_tok + tm), D_MODEL), F32),
        compiler_params=_params("arbitrary", "arbitrary"),
        name="moe_ffn",
    )(tile_expert, tile_valid, tile_first, tile_n_real, sorted_slot, h2, w_gate.astype(BF16),
      w_up.astype(BF16), w_down.astype(BF16))


def _moe_sum_kernel(x1_ref, wts_ref, y0_ref, y1_ref, out_ref):
    wts = wts_ref[...]
    out_ref[...] = x1_ref[...] + (wts[:, 0:1] * y0_ref[...] + wts[:, 1:2] * y1_ref[...])


def _moe_sum(x1, wts, y, tm):
    T = x1.shape[0]
    plane_blocks = (T + tm) // tm
    row = pl.BlockSpec((tm, D_MODEL), lambda i: (i, 0))
    return pl.pallas_call(
        _moe_sum_kernel,
        grid=(T // tm,),
        in_specs=[row, pl.BlockSpec((tm, LANES), lambda i: (i, 0)), row,
                  pl.BlockSpec((tm, D_MODEL), lambda i: (plane_blocks + i, 0))],
        out_specs=row,
        out_shape=jax.ShapeDtypeStruct((T, D_MODEL), F32),
        compiler_params=_params("parallel"),
        name="moe_sum",
    )(x1, wts, y, y)


def _mla_dense_layer(x2, pos2, batch, seq, attn_norm, w_in, q_a_norm, kv_a_norm, w_uq, w_ukv, q_norm,
                     k_norm, w_o, ffn_norm, w_gate, w_up, w_down):
    q, k, v = _mla_prep(x2, pos2, attn_norm, w_in, q_a_norm, kv_a_norm, w_uq, w_ukv, q_norm, k_norm)
    o, (w_o, w_gate, w_up, w_down) = _mla_attn(q, k, v, batch, seq, (w_o, w_gate, w_up, w_down))
    return _proj_dense_ffn(x2, o, w_o, ffn_norm, w_gate, w_up, w_down)


def _band_moe_layer(x2, batch, seq, attn_norm, w_qkv, q_norm, k_norm, rel_bias, w_o, ffn_norm, router,
                    w_gate, w_up, w_down):
    q, k, v = _ca_prep(x2, attn_norm, w_qkv)
    o, (w_gate, w_up, w_down) = _band_attn(q, k, v, q_norm, k_norm, rel_bias, batch, seq,
                                           (w_gate, w_up, w_down))
    x1, h, idx, wts = _proj_router(x2, o, w_o, ffn_norm, router)
    tm = ROW_TILE
    sorted_slot, tile_expert, tile_valid, tile_first, tile_n_real = _moe_plan(idx[:, :TOP_K], tm)
    y = _moe_ffn(h, sorted_slot, tile_expert, tile_valid, tile_first, tile_n_real, w_gate, w_up, w_down, tm)
    return _moe_sum(x1, wts, y, tm)


def kernel(x, positions, l0_attn_norm, l0_mla_w_in, l0_mla_q_a_norm, l0_mla_kv_a_norm, l0_mla_w_uq, l0_mla_w_ukv, l0_mla_q_norm, l0_mla_k_norm, l0_mla_w_o, l0_ffn_norm, l0_ffn_w_gate, l0_ffn_w_up, l0_ffn_w_down, l1_attn_norm, l1_ca_w_qkv, l1_ca_q_norm, l1_ca_k_norm, l1_ca_rel_bias, l1_ca_w_o, l1_ffn_norm, l1_moe_router, l1_moe_w_gate, l1_moe_w_up, l1_moe_w_down, l2_attn_norm, l2_mla_w_in, l2_mla_q_a_norm, l2_mla_kv_a_norm, l2_mla_w_uq, l2_mla_w_ukv, l2_mla_q_norm, l2_mla_k_norm, l2_mla_w_o, l2_ffn_norm, l2_ffn_w_gate, l2_ffn_w_up, l2_ffn_w_down, l3_attn_norm, l3_ca_w_qkv, l3_ca_q_norm, l3_ca_k_norm, l3_ca_rel_bias, l3_ca_w_o, l3_ffn_norm, l3_moe_router, l3_moe_w_gate, l3_moe_w_up, l3_moe_w_down):
    batch, seq, d = x.shape
    assert d == D_MODEL and seq % ATTN_Q_TILE == 0 and seq >= BAND_KEYS
    assert (batch * seq) % ROW_TILE == 0
    x2 = x.reshape(batch * seq, d)
    pos2 = positions.reshape(batch * seq, 1)
    x2 = _mla_dense_layer(x2, pos2, batch, seq, l0_attn_norm, l0_mla_w_in, l0_mla_q_a_norm, l0_mla_kv_a_norm,
                          l0_mla_w_uq, l0_mla_w_ukv, l0_mla_q_norm, l0_mla_k_norm, l0_mla_w_o, l0_ffn_norm,
                          l0_ffn_w_gate, l0_ffn_w_up, l0_ffn_w_down)
    x2 = _band_moe_layer(x2, batch, seq, l1_attn_norm, l1_ca_w_qkv, l1_ca_q_norm, l1_ca_k_norm, l1_ca_rel_bias,
                         l1_ca_w_o, l1_ffn_norm, l1_moe_router, l1_moe_w_gate, l1_moe_w_up, l1_moe_w_down)
    x2 = _mla_dense_layer(x2, pos2, batch, seq, l2_attn_norm, l2_mla_w_in, l2_mla_q_a_norm, l2_mla_kv_a_norm,
                          l2_mla_w_uq, l2_mla_w_ukv, l2_mla_q_norm, l2_mla_k_norm, l2_mla_w_o, l2_ffn_norm,
                          l2_ffn_w_gate, l2_ffn_w_up, l2_ffn_w_down)
    x2 = _band_moe_layer(x2, batch, seq, l3_attn_norm, l3_ca_w_qkv, l3_ca_q_norm, l3_ca_k_norm, l3_ca_rel_bias,
                         l3_ca_w_o, l3_ffn_norm, l3_moe_router, l3_moe_w_gate, l3_moe_w_up, l3_moe_w_down)
    return x2.reshape(batch, seq, d)
```

```python
import functools

import numpy as np
import jax
import jax.numpy as jnp
from jax import lax
from jax.experimental import pallas as pl
from jax.experimental.pallas import tpu as pltpu

F32 = jnp.float32
BF16 = jnp.bfloat16

D_MODEL = 1024
CHUNK = 64
RMS_EPS = 1e-6

MLA_HEADS = 8
MLA_NOPE = 128
MLA_ROPE = 64
MLA_V = 128
MLA_QK = MLA_NOPE + MLA_ROPE
Q_LORA = 256
KV_LORA = 128
ROPE_THETA = 10000.0

CA_HEADS = 16
CA_HEAD_DIM = 64
LEFT_CHUNKS = 8
MAX_REL = 256

N_EXPERTS = 8
TOP_K = 2

LANES = 128
SUBLANES = 8
VMEM_LIMIT = 56 * 1024 * 1024

ROW_TILE = 512
MLA_PREP_TILE = 1024
ROUTER_TILE = 1024
ROUTER_SUBBLOCKS = 4
MLA_Q_TILE = 512
MLA_HEADS_PER_STEP = 2
ATTN_Q_TILE = 256
BAND_KEYS = 3 * ATTN_Q_TILE
BAND_ROLL = ATTN_Q_TILE + BAND_KEYS
MOE_FF_TILE = 1792
NEG_INF = float("-inf")


def _dot(a, b):
    return jnp.dot(a, b, preferred_element_type=F32)


def _dot_nt(a, b):
    return lax.dot_general(a, b, (((1,), (1,)), ((), ())), preferred_element_type=F32)


def _rms(x, g):
    return x * lax.rsqrt(jnp.mean(x * x, axis=-1, keepdims=True) + RMS_EPS) * g


def _silu(g):
    return g / (1.0 + jnp.exp(-g))


def _params(*sem):
    return pltpu.CompilerParams(dimension_semantics=sem, vmem_limit_bytes=VMEM_LIMIT)


def _mla_prep_kernel(x_ref, pos_ref, gn_ref, win_ref, qan_ref, kvan_ref, wuq_ref, wukv_ref,
                     invf_ref, qtab_ref, ktab_ref, q_out, k_out, v_out):
    h = _rms(x_ref[...], gn_ref[...]).astype(BF16)
    a = _dot(h, win_ref[...])
    qn = _rms(a[:, :Q_LORA], qan_ref[...]).astype(BF16)
    kvn = _rms(a[:, Q_LORA:Q_LORA + KV_LORA], kvan_ref[...]).astype(BF16)
    kr2 = a[:, Q_LORA + KV_LORA:]
    q_all = _dot(qn, wuq_ref[...])
    kv_all = _dot(kvn, wukv_ref[...])

    ang = pos_ref[...].astype(F32) * invf_ref[...]
    cos = jnp.cos(ang)
    sin = jnp.sin(ang)
    lane = lax.broadcasted_iota(jnp.int32, (1, LANES), 1)
    rope_lane = lane < MLA_ROPE

    q_gn, q_ga, q_gb = qtab_ref[0:1, :], qtab_ref[1:2, :], qtab_ref[2:3, :]
    k_gn, k_ga, k_gb = ktab_ref[0:1, :], ktab_ref[1:2, :], ktab_ref[2:3, :]
    q_cos, q_sin = cos * q_ga, sin * q_gb
    k_rot = kr2 * (cos * k_ga) + pltpu.roll(kr2, MLA_ROPE, 1) * (sin * k_gb)
    kr_ss = jnp.sum(jnp.where(rope_lane, kr2 * kr2, 0.0), axis=-1, keepdims=True)
    scale = MLA_QK ** -0.5

    slabs = []
    for hd in range(MLA_HEADS):
        lo = hd * 2 * LANES
        qa = q_all[:, lo:lo + LANES]
        qb = q_all[:, lo + LANES:lo + 2 * LANES]
        kn = kv_all[:, lo:lo + LANES]
        ss = jnp.sum(qa * qa + jnp.where(rope_lane, qb * qb, 0.0), axis=-1, keepdims=True)
        ssk = jnp.sum(kn * kn, axis=-1, keepdims=True) + kr_ss
        slabs.append((qa, qb, kn, ss, ssk))
    for hd, (qa, qb, kn, ss, ssk) in enumerate(slabs):
        lo = hd * 2 * LANES
        r = lax.rsqrt(ss * (1.0 / MLA_QK) + RMS_EPS) * scale
        rk = lax.rsqrt(ssk * (1.0 / MLA_QK) + RMS_EPS)
        q_rot = (qb * q_cos + pltpu.roll(qb, MLA_ROPE, 1) * q_sin) * r
        q_out[hd, :, :MLA_NOPE] = (qa * q_gn * r).astype(BF16)
        q_out[hd, :, MLA_NOPE:] = q_rot[:, :MLA_ROPE].astype(BF16)
        k_out[hd, :, :MLA_NOPE] = (kn * k_gn * rk).astype(BF16)
        k_out[hd, :, MLA_NOPE:] = (k_rot * rk)[:, :MLA_ROPE].astype(BF16)
        v_out[hd] = kv_all[:, lo + LANES:lo + 2 * LANES].astype(BF16)


def _rope_partner():
    d = np.arange(MLA_ROPE)
    half = MLA_ROPE // 2
    partner = np.where(d < half, d + half, d - half)
    sign = np.where(d < half, -1.0, 1.0).astype(np.float32)
    return partner, sign


def _mla_tables(g):
    partner, sign = _rope_partner()
    zeros = jnp.zeros((LANES - MLA_ROPE,), F32)
    g_rope = g[MLA_NOPE:]
    rows = [g[:MLA_NOPE],
            jnp.concatenate([g_rope, zeros]),
            jnp.concatenate([g_rope[partner] * sign, zeros])]
    rows += [jnp.zeros((LANES,), F32)] * 5
    return jnp.stack(rows)


def _mla_prep(x2, pos2, attn_norm, w_in, q_a_norm, kv_a_norm, w_uq, w_ukv, q_norm, k_norm):
    T = x2.shape[0]
    tm = MLA_PREP_TILE
    partner, _ = _rope_partner()
    rope0 = Q_LORA + KV_LORA
    w_in_ext = jnp.concatenate([w_in, w_in[:, rope0 + partner]], axis=1).astype(BF16)
    wq = w_uq.reshape(Q_LORA, MLA_HEADS, MLA_QK)
    w_uq_ext = jnp.concatenate([wq, wq[:, :, MLA_NOPE + partner]], axis=2)
    w_uq_ext = w_uq_ext.reshape(Q_LORA, MLA_HEADS * 2 * LANES).astype(BF16)
    w_ukv_b = w_ukv.astype(BF16)
    inv = 1.0 / (ROPE_THETA ** (jnp.arange(0, MLA_ROPE, 2, dtype=F32) / MLA_ROPE))
    invf = jnp.concatenate([inv, inv, jnp.zeros((LANES - MLA_ROPE,), F32)])[None, :]

    full = lambda shape: pl.BlockSpec(shape, lambda i: (0,) * len(shape))
    out_qk = jax.ShapeDtypeStruct((MLA_HEADS, T, MLA_QK), BF16)
    out_v = jax.ShapeDtypeStruct((MLA_HEADS, T, MLA_V), BF16)
    return pl.pallas_call(
        _mla_prep_kernel,
        grid=(T // tm,),
        in_specs=[
            pl.BlockSpec((tm, D_MODEL), lambda i: (i, 0)),
            pl.BlockSpec((tm, 1), lambda i: (i, 0)),
            full((1, D_MODEL)),
            full((D_MODEL, 4 * LANES)),
            full((1, Q_LORA)),
            full((1, KV_LORA)),
            full((Q_LORA, MLA_HEADS * 2 * LANES)),
            full((KV_LORA, MLA_HEADS * 2 * LANES)),
            full((1, LANES)),
            full((8, LANES)),
            full((8, LANES)),
        ],
        out_specs=[
            pl.BlockSpec((MLA_HEADS, tm, MLA_QK), lambda i: (0, i, 0)),
            pl.BlockSpec((MLA_HEADS, tm, MLA_QK), lambda i: (0, i, 0)),
            pl.BlockSpec((MLA_HEADS, tm, MLA_V), lambda i: (0, i, 0)),
        ],
        out_shape=[out_qk, out_qk, out_v],
        compiler_params=_params("parallel"),
        name="mla_prep",
    )(x2, pos2, attn_norm[None, :], w_in_ext, q_a_norm[None, :], kv_a_norm[None, :],
      w_uq_ext, w_ukv_b, invf, _mla_tables(q_norm), _mla_tables(k_norm))


def _mla_attn_kernel(q_ref, k_ref, v_ref, o_ref, *, seq, tq):
    row_c = lax.broadcasted_iota(jnp.int32, (tq, tq), 0) // CHUNK
    col_c = lax.broadcasted_iota(jnp.int32, (tq, tq), 1) // CHUNK
    diag_mask = col_c <= row_c
    for qi in range(seq // tq):
        lo, hi = qi * tq, (qi + 1) * tq
        for hh in range(MLA_HEADS_PER_STEP):
            q = q_ref[hh, lo:hi, :]
            s_d = jnp.where(diag_mask, _dot_nt(q, k_ref[hh, lo:hi, :]), NEG_INF)
            m = jnp.max(s_d, axis=-1, keepdims=True)
            if qi > 0:
                s_o = _dot_nt(q, k_ref[hh, :lo, :])
                m = jnp.maximum(m, jnp.max(s_o, axis=-1, keepdims=True))
            p_d = jnp.exp(s_d - m)
            l = jnp.sum(p_d, axis=-1, keepdims=True)
            o = _dot(p_d.astype(BF16), v_ref[hh, lo:hi, :])
            if qi > 0:
                p_o = jnp.exp(s_o - m)
                l = l + jnp.sum(p_o, axis=-1, keepdims=True)
                o = o + _dot(p_o.astype(BF16), v_ref[hh, :lo, :])
            o_ref[lo:hi, hh * MLA_V:(hh + 1) * MLA_V] = (o / l).astype(BF16)


def _mla_attn(q, k, v, batch, seq):
    T = batch * seq
    hps = MLA_HEADS_PER_STEP
    kern = functools.partial(_mla_attn_kernel, seq=seq, tq=MLA_Q_TILE)
    return pl.pallas_call(
        kern,
        grid=(MLA_HEADS // hps, batch),
        in_specs=[
            pl.BlockSpec((hps, seq, MLA_QK), lambda h, b: (h, b, 0)),
            pl.BlockSpec((hps, seq, MLA_QK), lambda h, b: (h, b, 0)),
            pl.BlockSpec((hps, seq, MLA_V), lambda h, b: (h, b, 0)),
        ],
        out_specs=pl.BlockSpec((seq, hps * MLA_V), lambda h, b: (b, h)),
        out_shape=jax.ShapeDtypeStruct((T, MLA_HEADS * MLA_V), BF16),
        compiler_params=_params("parallel", "parallel"),
        name="mla_attn",
    )(q, k, v)


def _proj_dense_ffn_kernel(x_ref, o_ref, wo_ref, gn_ref, wg_ref, wu_ref, wd_ref, out_ref):
    x1 = x_ref[...] + _dot(o_ref[...], wo_ref[...])
    h = _rms(x1, gn_ref[...]).astype(BF16)
    a = (_silu(_dot(h, wg_ref[...])) * _dot(h, wu_ref[...])).astype(BF16)
    out_ref[...] = x1 + _dot(a, wd_ref[...])


def _proj_dense_ffn(x2, o2, w_o, ffn_norm, w_gate, w_up, w_down):
    T = x2.shape[0]
    ff = w_gate.shape[1]
    tm = ROW_TILE
    const = lambda shape: pl.BlockSpec(shape, lambda i: (0, 0), pipeline_mode=pl.Buffered(1))
    row = pl.BlockSpec((tm, D_MODEL), lambda i: (i, 0))
    return pl.pallas_call(
        _proj_dense_ffn_kernel,
        grid=(T // tm,),
        in_specs=[row, row, const((D_MODEL, D_MODEL)), const((1, D_MODEL)), const((D_MODEL, ff)),
                  const((D_MODEL, ff)), const((ff, D_MODEL))],
        out_specs=row,
        out_shape=jax.ShapeDtypeStruct((T, D_MODEL), F32),
        compiler_params=_params("parallel"),
        name="proj_dense_ffn",
    )(x2, o2, w_o.astype(BF16), ffn_norm[None, :], w_gate.astype(BF16), w_up.astype(BF16),
      w_down.astype(BF16))


def _ca_prep_kernel(x_ref, gn_ref, w_ref, g_ref, q_out, k_out, v_out):
    h = _rms(x_ref[...], gn_ref[...]).astype(BF16)
    qkv = _dot(h, w_ref[...])
    lane = lax.broadcasted_iota(jnp.int32, (1, LANES), 1)
    head0 = lane < CA_HEAD_DIM

    def head_normed(x, gain):
        sq = x * x
        ss0 = jnp.sum(jnp.where(head0, sq, 0.0), axis=-1, keepdims=True)
        ss1 = jnp.sum(jnp.where(head0, 0.0, sq), axis=-1, keepdims=True)
        r = jnp.where(head0, lax.rsqrt(ss0 * (1.0 / CA_HEAD_DIM) + RMS_EPS),
                      lax.rsqrt(ss1 * (1.0 / CA_HEAD_DIM) + RMS_EPS))
        return (x * r * gain).astype(BF16)

    for c in range(D_MODEL // LANES):
        lo = c * LANES
        q_out[:, lo:lo + LANES] = head_normed(qkv[:, lo:lo + LANES], g_ref[0:1, :])
        k_out[:, lo:lo + LANES] = head_normed(qkv[:, D_MODEL + lo:D_MODEL + lo + LANES], g_ref[1:2, :])
    v_out[...] = qkv[:, 2 * D_MODEL:].astype(BF16)


def _ca_prep(x2, attn_norm, w_qkv, q_norm, k_norm):
    T = x2.shape[0]
    tm = ROW_TILE
    gains = jnp.stack([jnp.tile(q_norm, 2) * CA_HEAD_DIM ** -0.5, jnp.tile(k_norm, 2)]
                      + [jnp.zeros((LANES,), F32)] * 6)
    full = lambda shape: pl.BlockSpec(shape, lambda i: (0,) * len(shape))
    out = jax.ShapeDtypeStruct((T, D_MODEL), BF16)
    row = pl.BlockSpec((tm, D_MODEL), lambda i: (i, 0))
    return pl.pallas_call(
        _ca_prep_kernel,
        grid=(T // tm,),
        in_specs=[row, full((1, D_MODEL)), full((D_MODEL, 3 * D_MODEL)), full((8, LANES))],
        out_specs=[row, row, row],
        out_shape=[out, out, out],
        compiler_params=_params("parallel"),
        name="ca_prep",
    )(x2, attn_norm[None, :], w_qkv.astype(BF16), gains)


def _band_attn_kernel(q_ref, k_ref, v_ref, w_ref, wg_ref, wu_ref, wd_ref,
                      o_ref, wg_out, wu_out, wd_out, tab_ref, *, seq, tq):
    wg_out[...] = wg_ref[...].astype(BF16)
    wu_out[...] = wu_ref[...].astype(BF16)
    wd_out[...] = wd_ref[...].astype(BF16)

    @pl.when(pl.program_id(1) == 0)
    def _():
        row = lax.broadcasted_iota(jnp.int32, (tq, BAND_KEYS), 0)
        col = lax.broadcasted_iota(jnp.int32, (tq, BAND_KEYS), 1)
        shift = CHUNK.bit_length() - 1
        for hh in range(2):
            for var in range(3):
                w = w_ref[0, 3 * hh + var:3 * hh + var + 1, :]
                t = pltpu.roll(jnp.broadcast_to(w, (tq, BAND_ROLL)), 0, 1, stride=1, stride_axis=0)
                gap = lax.shift_right_logical(row + var * tq, shift) - lax.shift_right_logical(col, shift)
                tab_ref[hh, var] = jnp.where((gap >= 0) & (gap <= LEFT_CHUNKS), t[:, :BAND_KEYS], NEG_INF)

    lane = lax.broadcasted_iota(jnp.int32, (1, LANES), 1)
    head0 = lane < CA_HEAD_DIM

    for qi in range(seq // tq):
        var = min(qi, 2)
        ks = max(qi - 2, 0) * tq
        q = q_ref[qi * tq:(qi + 1) * tq, :]
        k = k_ref[ks:ks + BAND_KEYS, :]
        v = v_ref[ks:ks + BAND_KEYS, :]
        zero = jnp.zeros_like(q)
        qq = jnp.concatenate([jnp.where(head0, q, zero), jnp.where(head0, zero, q)], axis=0)
        s = _dot_nt(qq, k)
        ps, ls = [], []
        for hh in range(2):
            sh = s[hh * tq:(hh + 1) * tq] + tab_ref[hh, var]
            p = jnp.exp(sh - jnp.max(sh, axis=-1, keepdims=True))
            ls.append(jnp.sum(p, axis=-1, keepdims=True))
            ps.append(p.astype(BF16))
        oo = _dot(jnp.concatenate(ps, axis=0), v)
        o = jnp.where(head0, oo[:tq] / ls[0], oo[tq:] / ls[1])
        o_ref[qi * tq:(qi + 1) * tq, :] = o.astype(BF16)


def _band_bias_rows(rel_bias):
    tq = ATTN_Q_TILE
    rb = rel_bias.astype(F32)
    r_max = 3 * tq - 1
    n_far = r_max - MAX_REL
    n_m = r_max + BAND_KEYS
    n_neg = n_m - n_far - rb.shape[1]
    heads = rb.shape[0]
    m = jnp.concatenate([jnp.broadcast_to(rb[:, -1:], (heads, n_far)), rb[:, ::-1],
                         jnp.broadcast_to(rb[:, :1], (heads, n_neg))], axis=1)
    rows = []
    for var in range(3):
        base = r_max - var * tq
        rows.append(jnp.concatenate([m[:, base:base + BAND_KEYS], jnp.zeros((heads, 1), F32),
                                     m[:, base - (tq - 1):base]], axis=1))
    w = jnp.stack(rows, axis=1).reshape(heads // 2, 6, BAND_ROLL)
    return jnp.concatenate([w, jnp.zeros((heads // 2, 2, BAND_ROLL), F32)], axis=1)


def _band_attn(q, k, v, rel_bias, batch, seq, expert_weights):
    T = batch * seq
    tq = ATTN_Q_TILE
    n_steps = (CA_HEADS // 2) * batch
    kern = functools.partial(_band_attn_kernel, seq=seq, tq=tq)
    slab = pl.BlockSpec((seq, LANES), lambda hp, b: (b, hp))
    sliced = [w.reshape(n_steps, w.shape[0] * w.shape[1] // n_steps, w.shape[2]) for w in expert_weights]
    w_specs = [pl.BlockSpec((1,) + w.shape[1:], lambda hp, b: (hp * batch + b, 0, 0)) for w in sliced]
    outs = pl.pallas_call(
        kern,
        grid=(CA_HEADS // 2, batch),
        in_specs=[slab, slab, slab, pl.BlockSpec((1, 8, BAND_ROLL), lambda hp, b: (hp, 0, 0))] + w_specs,
        out_specs=[slab] + w_specs,
        out_shape=[jax.ShapeDtypeStruct((T, D_MODEL), BF16)]
                  + [jax.ShapeDtypeStruct(w.shape, BF16) for w in sliced],
        scratch_shapes=[pltpu.VMEM((2, 3, tq, BAND_KEYS), F32)],
        compiler_params=_params("arbitrary", "arbitrary"),
        name="band_attn",
    )(q, k, v, _band_bias_rows(rel_bias), *sliced)
    return outs[0], [wb.reshape(w.shape) for wb, w in zip(outs[1:], expert_weights)]


def _proj_router_kernel(x_ref, o_ref, wo_ref, gn_ref, rhi_ref, rlo_ref, x1_out, h_out, idx_out, w_out):
    rows = x_ref.shape[0] // ROUTER_SUBBLOCKS
    for blk in range(ROUTER_SUBBLOCKS):
        sl = slice(blk * rows, (blk + 1) * rows)
        x1 = x_ref[sl, :] + _dot(o_ref[sl, :], wo_ref[...])
        x1_out[sl, :] = x1
        h = _rms(x1, gn_ref[...])
        h_out[sl, :] = h
        h_hi = h.astype(BF16)
        h_lo = (h - h_hi.astype(F32)).astype(BF16)
        logits = _dot(h_hi, rhi_ref[...]) + (_dot(h_lo, rhi_ref[...]) + _dot(h_hi, rlo_ref[...]))

        lane = lax.broadcasted_iota(jnp.int32, logits.shape, 1)
        lane_f = lane.astype(F32)
        lg = jnp.where(lane < N_EXPERTS, logits, NEG_INF)
        m1 = jnp.max(lg, axis=-1, keepdims=True)
        i1 = jnp.min(jnp.where(lg == m1, lane_f, float(LANES)), axis=-1, keepdims=True)
        lg2 = jnp.where(lane_f == i1, NEG_INF, lg)
        m2 = jnp.max(lg2, axis=-1, keepdims=True)
        i2 = jnp.min(jnp.where(lg2 == m2, lane_f, float(LANES)), axis=-1, keepdims=True)
        e = jnp.exp(m2 - m1)
        w1 = 1.0 / (1.0 + e)
        w2 = e / (1.0 + e)
        idx_out[sl, :] = jnp.where(lane == 0, i1, jnp.where(lane == 1, i2, 0.0)).astype(jnp.int32)
        w_out[sl, :] = jnp.where(lane == 0, w1, jnp.where(lane == 1, w2, 0.0))


def _proj_router(x2, o2, w_o, ffn_norm, router):
    T = x2.shape[0]
    tm = ROUTER_TILE
    r_pad = jnp.zeros((D_MODEL, LANES), F32).at[:, :N_EXPERTS].set(router)
    r_hi = r_pad.astype(BF16)
    r_lo = (r_pad - r_hi.astype(F32)).astype(BF16)
    full = lambda shape: pl.BlockSpec(shape, lambda i: (0,) * len(shape))
    row = pl.BlockSpec((tm, D_MODEL), lambda i: (i, 0))
    small = pl.BlockSpec((tm, LANES), lambda i: (i, 0))
    return pl.pallas_call(
        _proj_router_kernel,
        grid=(T // tm,),
        in_specs=[row, row, full((D_MODEL, D_MODEL)), full((1, D_MODEL)), full((D_MODEL, LANES)),
                  full((D_MODEL, LANES))],
        out_specs=[row, row, small, small],
        out_shape=[jax.ShapeDtypeStruct((T, D_MODEL), F32), jax.ShapeDtypeStruct((T, D_MODEL), F32),
                   jax.ShapeDtypeStruct((T, LANES), jnp.int32), jax.ShapeDtypeStruct((T, LANES), F32)],
        compiler_params=_params("parallel"),
        name="proj_router",
    )(x2, o2, w_o.astype(BF16), ffn_norm[None, :], r_hi, r_lo)


def _moe_plan(top_idx, tm):
    T = top_idx.shape[0]
    n_assign = T * TOP_K
    n_tiles = n_assign // tm + N_EXPERTS
    plane = T + tm
    e_flat = top_idx.reshape(1, n_assign)
    experts = jnp.arange(N_EXPERTS, dtype=jnp.int32)
    onehot = (e_flat == experts[:, None]).astype(jnp.int32)
    csum = jnp.cumsum(onehot, axis=1)
    counts = csum[:, -1]
    tiles_per = (counts + tm - 1) // tm
    tile_end = jnp.cumsum(tiles_per)
    tile_start = tile_end - tiles_per
    group_start = jnp.cumsum(counts) - counts
    order = jnp.sum(onehot * (csum - 1 + group_start[:, None]), axis=0)
    a = jnp.arange(n_assign, dtype=jnp.int32)
    _, sorted_slot = lax.sort((order, (a % TOP_K) * plane + a // TOP_K), num_keys=1)
    sorted_slot = jnp.concatenate([sorted_slot, jnp.zeros((tm,), jnp.int32)])

    tile_id = jnp.arange(n_tiles, dtype=jnp.int32)
    tile_valid = (tile_id < tile_end[-1]).astype(jnp.int32)
    tile_expert = jnp.minimum(jnp.sum((tile_id[:, None] >= tile_end[None, :]).astype(jnp.int32), axis=1),
                              N_EXPERTS - 1)
    of_tile = lambda per_expert: jnp.sum((tile_expert[:, None] == experts[None, :]) * per_expert[None, :], axis=1)
    rows_before = (tile_id - of_tile(tile_start)) * tm
    n_real = jnp.clip(of_tile(counts) - rows_before, 0, tm) * tile_valid
    first = jnp.clip(of_tile(group_start) + rows_before, 0, n_assign)
    return (sorted_slot.astype(jnp.int32), tile_expert.astype(jnp.int32), tile_valid, first.astype(jnp.int32),
            n_real.astype(jnp.int32))


def _moe_ffn_kernel(te_ref, tv_ref, first_ref, nreal_ref, slot_ref, h_hbm, wg_ref, wu_ref, wd_ref, y_hbm,
                    xbuf, y0, y1, acc_ref, hs_ref, gsem, ssem, *, tm, n_tok):
    i = pl.program_id(0)
    f = pl.program_id(1)
    s = lax.rem(i, 2)
    plane = n_tok + tm
    valid = tv_ref[i] > 0
    prev_valid = (i > 0) & (tv_ref[jnp.maximum(i - 1, 0)] > 0)
    ys = (y0, y1)

    def gather_rows(tile, par):
        base, n_real = first_ref[tile], nreal_ref[tile]

        def copy(j):
            p = slot_ref[base + j]
            tok = jnp.where(j < n_real, jnp.where(p >= plane, p - plane, p), n_tok - 1)
            return pltpu.make_async_copy(h_hbm.at[pl.ds(tok, 1), :], xbuf.at[par, pl.ds(j, 1), :], gsem.at[par])
        return copy

    def scatter_rows(tile, par):
        base, n_real = first_ref[tile], nreal_ref[tile]

        def copy(j):
            p = jnp.where(j < n_real, slot_ref[base + j], par * plane + n_tok + j)
            return pltpu.make_async_copy(ys[par].at[pl.ds(j, 1), :], y_hbm.at[pl.ds(p, 1), :], ssem.at[par])
        return copy

    def wait_gather(slot):
        pltpu.make_async_copy(h_hbm.at[pl.ds(0, tm), :], xbuf.at[0], gsem.at[slot]).wait()

    def wait_scatter(slot):
        pltpu.make_async_copy(y0, y_hbm.at[pl.ds(0, tm), :], ssem.at[slot]).wait()

    def partial_ffn():
        h = hs_ref[...]
        a = (_silu(_dot(h, wg_ref[0])) * _dot(h, wu_ref[0])).astype(BF16)
        return _dot(a, wd_ref[0])

    @pl.when((f == 0) & (i == 0))
    def _():
        y1[...] = jnp.zeros((tm, D_MODEL), F32)
        for k in range(TOP_K):
            cp = pltpu.make_async_copy(y1, y_hbm.at[pl.ds(k * plane + n_tok, tm), :], ssem.at[1])
            cp.start()
            cp.wait()

        first_tile = gather_rows(0, 0)

        def first(j, carry):
            first_tile(j).start()
            return carry
        lax.fori_loop(0, tm, first, 0)

    @pl.when((f == 0) & ((i == 0) | prev_valid))
    def _():
        wait_gather(s)

    @pl.when((f == 1) & (i >= 2) & prev_valid)
    def _():
        wait_scatter(s)

    @pl.when(valid & (f == 0))
    def _():
        hs_ref[...] = xbuf[s].astype(BF16)
        next_tile = gather_rows(i + 1, 1 - s)
        for j in range(tm):
            next_tile(j).start()
        acc_ref[...] = partial_ffn()

    for par in range(2):
        @pl.when(valid & (f == 1) & (i > 0) & (s == par))
        def _(par=par):
            prev_tile = scatter_rows(i - 1, 1 - par)
            for j in range(tm):
                prev_tile(j).start()
            ys[par][...] = acc_ref[...] + partial_ffn()

        @pl.when(jnp.logical_not(valid) & (f == 1) & prev_valid & (s == par))
        def _(par=par):
            prev_tile = scatter_rows(i - 1, 1 - par)

            def last(j, carry):
                prev_tile(j).start()
                return carry
            lax.fori_loop(0, tm, last, 0)
            wait_scatter(1 - par)

    @pl.when(valid & (f == 1) & (i == 0))
    def _():
        y0[...] = acc_ref[...] + partial_ffn()


def _moe_ffn(h2, sorted_slot, tile_expert, tile_valid, tile_first, tile_n_real, w_gate, w_up, w_down, tm):
    n_tok = h2.shape[0]
    n_tiles = tile_expert.shape[0]
    ff = w_gate.shape[2]
    tf = MOE_FF_TILE
    nf = ff // tf
    assert nf == 2
    ff_blk = lambda i, f, tv: f * tv[i] + (nf - 1) * (1 - tv[i])
    kern = functools.partial(_moe_ffn_kernel, tm=tm, n_tok=n_tok)
    return pl.pallas_call(
        kern,
        grid_spec=pltpu.PrefetchScalarGridSpec(
            num_scalar_prefetch=5,
            grid=(n_tiles, nf),
            in_specs=[
                pl.BlockSpec(memory_space=pl.ANY),
                pl.BlockSpec((1, D_MODEL, tf), lambda i, f, te, tv, *_: (te[i], 0, ff_blk(i, f, tv))),
                pl.BlockSpec((1, D_MODEL, tf), lambda i, f, te, tv, *_: (te[i], 0, ff_blk(i, f, tv))),
                pl.BlockSpec((1, tf, D_MODEL), lambda i, f, te, tv, *_: (te[i], ff_blk(i, f, tv), 0)),
            ],
            out_specs=pl.BlockSpec(memory_space=pl.ANY),
            scratch_shapes=[pltpu.VMEM((2, tm, D_MODEL), F32)] + [pltpu.VMEM((tm, D_MODEL), F32)] * 3 + [
                pltpu.VMEM((tm, D_MODEL), BF16), pltpu.SemaphoreType.DMA((2,)), pltpu.SemaphoreType.DMA((2,))],
        ),
        out_shape=jax.ShapeDtypeStruct((TOP_K * (n_tok + tm), D_MODEL), F32),
        compiler_params=_params("arbitrary", "arbitrary"),
        name="moe_ffn",
    )(tile_expert, tile_valid, tile_first, tile_n_real, sorted_slot, h2, w_gate.astype(BF16),
      w_up.astype(BF16), w_down.astype(BF16))


def _moe_sum_kernel(x1_ref, wts_ref, y0_ref, y1_ref, out_ref):
    wts = wts_ref[...]
    out_ref[...] = x1_ref[...] + (wts[:, 0:1] * y0_ref[...] + wts[:, 1:2] * y1_ref[...])


def _moe_sum(x1, wts, y, tm):
    T = x1.shape[0]
    plane_blocks = (T + tm) // tm
    row = pl.BlockSpec((tm, D_MODEL), lambda i: (i, 0))
    return pl.pallas_call(
        _moe_sum_kernel,
        grid=(T // tm,),
        in_specs=[row, pl.BlockSpec((tm, LANES), lambda i: (i, 0)), row,
                  pl.BlockSpec((tm, D_MODEL), lambda i: (plane_blocks + i, 0))],
        out_specs=row,
        out_shape=jax.ShapeDtypeStruct((T, D_MODEL), F32),
        compiler_params=_params("parallel"),
        name="moe_sum",
    )(x1, wts, y, y)


def _mla_dense_layer(x2, pos2, batch, seq, attn_norm, w_in, q_a_norm, kv_a_norm, w_uq, w_ukv, q_norm,
                     k_norm, w_o, ffn_norm, w_gate, w_up, w_down):
    q, k, v = _mla_prep(x2, pos2, attn_norm, w_in, q_a_norm, kv_a_norm, w_uq, w_ukv, q_norm, k_norm)
    o = _mla_attn(q, k, v, batch, seq)
    return _proj_dense_ffn(x2, o, w_o, ffn_norm, w_gate, w_up, w_down)


def _band_moe_layer(x2, batch, seq, attn_norm, w_qkv, q_norm, k_norm, rel_bias, w_o, ffn_norm, router,
                    w_gate, w_up, w_down):
    q, k, v = _ca_prep(x2, attn_norm, w_qkv, q_norm, k_norm)
    o, (w_gate, w_up, w_down) = _band_attn(q, k, v, rel_bias, batch, seq, (w_gate, w_up, w_down))
    x1, h, idx, wts = _proj_router(x2, o, w_o, ffn_norm, router)
    tm = ROW_TILE
    sorted_slot, tile_expert, tile_valid, tile_first, tile_n_real = _moe_plan(idx[:, :TOP_K], tm)
    y = _moe_ffn(h, sorted_slot, tile_expert, tile_valid, tile_first, tile_n_real, w_gate, w_up, w_down, tm)
    return _moe_sum(x1, wts, y, tm)


def kernel(x, positions, l0_attn_norm, l0_mla_w_in, l0_mla_q_a_norm, l0_mla_kv_a_norm, l0_mla_w_uq, l0_mla_w_ukv, l0_mla_q_norm, l0_mla_k_norm, l0_mla_w_o, l0_ffn_norm, l0_ffn_w_gate, l0_ffn_w_up, l0_ffn_w_down, l1_attn_norm, l1_ca_w_qkv, l1_ca_q_norm, l1_ca_k_norm, l1_ca_rel_bias, l1_ca_w_o, l1_ffn_norm, l1_moe_router, l1_moe_w_gate, l1_moe_w_up, l1_moe_w_down, l2_attn_norm, l2_mla_w_in, l2_mla_q_a_norm, l2_mla_kv_a_norm, l2_mla_w_uq, l2_mla_w_ukv, l2_mla_q_norm, l2_mla_k_norm, l2_mla_w_o, l2_ffn_norm, l2_ffn_w_gate, l2_ffn_w_up, l2_ffn_w_down, l3_attn_norm, l3_ca_w_qkv, l3_ca_q_norm, l3_ca_k_norm, l3_ca_rel_bias, l3_ca_w_o, l3_ffn_norm, l3_moe_router, l3_moe_w_gate, l3_moe_w_up, l3_moe_w_down):
    batch, seq, d = x.shape
    assert d == D_MODEL and seq % ATTN_Q_TILE == 0 and seq >= BAND_KEYS
    assert (batch * seq) % ROW_TILE == 0
    x2 = x.reshape(batch * seq, d)
    pos2 = positions.reshape(batch * seq, 1)
    x2 = _mla_dense_layer(x2, pos2, batch, seq, l0_attn_norm, l0_mla_w_in, l0_mla_q_a_norm, l0_mla_kv_a_norm,
                          l0_mla_w_uq, l0_mla_w_ukv, l0_mla_q_norm, l0_mla_k_norm, l0_mla_w_o, l0_ffn_norm,
                          l0_ffn_w_gate, l0_ffn_w_up, l0_ffn_w_down)
    x2 = _band_moe_layer(x2, batch, seq, l1_attn_norm, l1_ca_w_qkv, l1_ca_q_norm, l1_ca_k_norm, l1_ca_rel_bias,
                         l1_ca_w_o, l1_ffn_norm, l1_moe_router, l1_moe_w_gate, l1_moe_w_up, l1_moe_w_down)
    x2 = _mla_dense_layer(x2, pos2, batch, seq, l2_attn_norm, l2_mla_w_in, l2_mla_q_a_norm, l2_mla_kv_a_norm,
                          l2_mla_w_uq, l2_mla_w_ukv, l2_mla_q_norm, l2_mla_k_norm, l2_mla_w_o, l2_ffn_norm,
                          l2_ffn_w_gate, l2_ffn_w_up, l2_ffn_w_down)
    x2 = _band_moe_layer(x2, batch, seq, l3_attn_norm, l3_ca_w_qkv, l3_ca_q_norm, l3_ca_k_norm, l3_ca_rel_bias,
                         l3_ca_w_o, l3_ffn_norm, l3_moe_router, l3_moe_w_gate, l3_moe_w_up, l3_moe_w_down)
    return x2.reshape(batch, seq, d)
```
